```python
import jax, jax.numpy as jnp
from jax import lax
import numpy as np

D_MODEL = 1024
BATCH = 16
SEQ = 2048
DEPTH = 1

CHUNK = 128
A_GROUPS = 4
A_WIDTH = 512
A_GROUP_DIM = A_WIDTH // A_GROUPS
N_HEADS = 8
N_KV_HEADS = 2
HEAD_DIM = 64
Q_DIM = N_HEADS * HEAD_DIM
KV_DIM = N_KV_HEADS * HEAD_DIM
WINDOW = 128
BLOCK = 128
N_BUCKETS = 32
MAX_DISTANCE = 128
D_FF = 2816
CONV_WIDTH = 3
EPS = 1e-6
NEG_INF = -1e30
IN_SIZES = (A_WIDTH, A_WIDTH, Q_DIM, KV_DIM, KV_DIM, D_MODEL, D_MODEL)
IN_DIM = sum(IN_SIZES)

kernel_name = "hybrid_gated_gmlp_swa_convffn"


def rmsnorm(x, g):
    xf = x.astype(jnp.float32)
    r = lax.rsqrt(jnp.mean(xf * xf, axis=-1, keepdims=True) + EPS)
    return (xf * r * g.astype(jnp.float32)).astype(x.dtype)


def band_buckets():
    i = np.arange(BLOCK)[:, None]
    j = np.arange(2 * BLOCK)[None, :]
    dist = i + BLOCK - j
    valid = (dist >= 0) & (dist < WINDOW)
    d = np.clip(dist, 0, None)
    max_exact = N_BUCKETS // 2
    large = max_exact + (np.log(np.maximum(d, 1) / max_exact) / np.log(MAX_DISTANCE / max_exact)
                         * (N_BUCKETS - max_exact)).astype(np.int32)
    large = np.minimum(large, N_BUCKETS - 1)
    buckets = np.where(d < max_exact, d, large).astype(np.int32)
    return buckets, valid


def spatial_gating(u, v, g_sgu, w_s, b_s):
    B, S = v.shape[0], v.shape[1]
    nc = S // CHUNK
    v = rmsnorm(v, g_sgu).reshape(B, nc, CHUNK, A_GROUPS, A_GROUP_DIM)
    causal = jnp.tril(jnp.ones((CHUNK, CHUNK), dtype=w_s.dtype))
    w_masked = w_s * causal[None]
    s = jnp.einsum('gts,bcsgd->bctgd', w_masked, v) + jnp.transpose(b_s)[None, None, :, :, None]
    return u * s.reshape(B, S, A_WIDTH)


def swa_sink_attention(q, k, v, sinks, rel_bias):
    B, S = q.shape[0], q.shape[1]
    nb = S // BLOCK
    G = N_HEADS // N_KV_HEADS
    qb = q.reshape(B, nb, BLOCK, N_KV_HEADS, G, HEAD_DIM)
    kb = k.reshape(B, nb, BLOCK, N_KV_HEADS, HEAD_DIM)
    vb = v.reshape(B, nb, BLOCK, N_KV_HEADS, HEAD_DIM)
    pad = ((0, 0), (1, 0), (0, 0), (0, 0), (0, 0))
    kw = jnp.concatenate([jnp.pad(kb, pad)[:, :-1], kb], axis=2)
    vw = jnp.concatenate([jnp.pad(vb, pad)[:, :-1], vb], axis=2)
    scale = HEAD_DIM ** -0.5
    scores = jnp.einsum('bnqhgd,bnkhd->bnhgqk', qb, kw).astype(jnp.float32) * scale
    buckets, valid = band_buckets()
    bias = rel_bias[buckets].astype(jnp.float32)
    bias = jnp.transpose(bias, (2, 0, 1)).reshape(N_KV_HEADS, G, BLOCK, 2 * BLOCK)
    first_ok = (np.arange(nb)[:, None] > 0) | (np.arange(2 * BLOCK)[None, :] >= BLOCK)
    mask = valid[None, :, :] & first_ok[:, None, :]
    scores = jnp.where(mask[None, :, None, None], scores + bias[None, None], NEG_INF)
    sink = sinks.astype(jnp.float32).reshape(N_KV_HEADS, G)[None, None, :, :, None, None]
    m = jnp.maximum(jnp.max(scores, axis=-1, keepdims=True), sink)
    p = jnp.exp(scores - m)
    probs = p / (jnp.sum(p, axis=-1, keepdims=True) + jnp.exp(sink - m))
    out = jnp.einsum('bnhgqk,bnkhd->bnqhgd', probs.astype(vw.dtype), vw)
    return out.reshape(B, S, Q_DIM)


def causal_depthwise_conv(x, w, b):
    S = x.shape[1]
    xp = jnp.pad(x, ((0, 0), (CONV_WIDTH - 1, 0), (0, 0)))
    y = w[0] * xp[:, 0:S]
    for j in range(1, CONV_WIDTH):
        y = y + w[j] * xp[:, j:j + S]
    return y + b


def setup_inputs(seed: int = 0) -> dict:
    key = jax.random.key(seed)
    ks = jax.random.split(key, 20)
    f32 = jnp.float32
    nrm = lambda k, shape, s: (jax.random.normal(k, shape, f32) * s)
    L = DEPTH
    return {
        "x": nrm(ks[0], (BATCH, SEQ, D_MODEL), 1.0),
        "g_mix": 1.0 + nrm(ks[1], (L, D_MODEL), 0.05),
        "w_in": nrm(ks[2], (L, D_MODEL, IN_DIM), D_MODEL ** -0.5),
        "g_sgu": 1.0 + nrm(ks[3], (L, A_WIDTH), 0.05),
        "w_s": nrm(ks[4], (L, A_GROUPS, CHUNK, CHUNK), CHUNK ** -0.5),
        "b_s": 1.0 + nrm(ks[5], (L, A_GROUPS, CHUNK), 0.1),
        "sinks": nrm(ks[6], (L, N_HEADS), 0.5),
        "rel_bias": nrm(ks[7], (N_BUCKETS, N_HEADS), 0.5),
        "w_pa": nrm(ks[8], (L, A_WIDTH, D_MODEL), A_WIDTH ** -0.5),
        "w_pb": nrm(ks[9], (L, Q_DIM, D_MODEL), Q_DIM ** -0.5),
        "w_out": nrm(ks[10], (L, D_MODEL, D_MODEL), D_MODEL ** -0.5),
        "g_ffn": 1.0 + nrm(ks[11], (L, D_MODEL), 0.05),
        "w_up": nrm(ks[12], (L, D_MODEL, 2 * D_FF), D_MODEL ** -0.5),
        "w_conv": nrm(ks[13], (L, CONV_WIDTH, 2 * D_FF), CONV_WIDTH ** -0.5),
        "b_conv": nrm(ks[14], (L, 2 * D_FF), 0.01),
        "w_down": nrm(ks[15], (L, D_FF, D_MODEL), D_FF ** -0.5),
        "g_final": 1.0 + nrm(ks[16], (D_MODEL,), 0.05),
    }


def reference(x, g_mix, w_in, g_sgu, w_s, b_s, sinks, rel_bias, w_pa, w_pb, w_out,
              g_ffn, w_up, w_conv, b_conv, w_down, g_final):
    splits = [int(c) for c in np.cumsum(IN_SIZES)[:-1]]
    B, S = x.shape[0], x.shape[1]
    for l in range(DEPTH):
        h = rmsnorm(x, g_mix[l])
        proj = jnp.einsum('bsd,de->bse', h, w_in[l])
        pu, pv, q, k, v, gate_a, gate_b = jnp.split(proj, splits, axis=-1)
        y_a = spatial_gating(jax.nn.gelu(pu), jax.nn.gelu(pv), g_sgu[l], w_s[l], b_s[l])
        y_b = swa_sink_attention(q.reshape(B, S, N_HEADS, HEAD_DIM),
                                 k.reshape(B, S, N_KV_HEADS, HEAD_DIM),
                                 v.reshape(B, S, N_KV_HEADS, HEAD_DIM),
                                 sinks[l], rel_bias)
        merged = (jax.nn.sigmoid(gate_a) * jnp.einsum('bse,ed->bsd', y_a, w_pa[l])
                  + jax.nn.sigmoid(gate_b) * jnp.einsum('bse,ed->bsd', y_b, w_pb[l]))
        x = x + jnp.einsum('bsd,de->bse', merged, w_out[l])
        h2 = rmsnorm(x, g_ffn[l])
        up = causal_depthwise_conv(jnp.einsum('bsd,df->bsf', h2, w_up[l]), w_conv[l], b_conv[l])
        gate, val = jnp.split(up, 2, axis=-1)
        x = x + jnp.einsum('bsf,fd->bsd', jax.nn.silu(gate) * val, w_down[l])
    return rmsnorm(x, g_final)
```

```python
import functools
import math

import jax
import jax.numpy as jnp
import numpy as np
from jax import lax
from jax.experimental import pallas as pl
from jax.experimental.pallas import tpu as pltpu

D_MODEL = 1024
CHUNK = 128
A_GROUPS = 4
A_WIDTH = 512
N_HEADS = 8
N_KV_HEADS = 2
HEAD_DIM = 64
Q_DIM = N_HEADS * HEAD_DIM
KV_DIM = N_KV_HEADS * HEAD_DIM
WINDOW = 128
BLOCK = 128
N_BUCKETS = 32
MAX_DISTANCE = 128
D_FF = 2816
CONV_WIDTH = 3
EPS = 1e-6
NEG_INF = -1e30

OFF_PU = 0
OFF_PV = OFF_PU + A_WIDTH
OFF_Q = OFF_PV + A_WIDTH
OFF_K = OFF_Q + Q_DIM
OFF_V = OFF_K + KV_DIM
OFF_GA = OFF_V + KV_DIM
OFF_GB = OFF_GA + D_MODEL
IN_DIM = OFF_GB + D_MODEL

LANES = 128
SUBLANES = 8
MXU_COLS = 256
TILE_ROWS = 256
F_CHUNK = 256
VMEM_LIMIT = 48 * 1024 * 1024

F32 = jnp.float32
BF16 = jnp.bfloat16


def _gelu_tanh(x):
    c = math.sqrt(2.0 / math.pi)
    return x * (0.5 * (1.0 + jnp.tanh(c * (x + 0.044715 * (x * x * x)))))


def _sigmoid(x):
    return 0.5 * (1.0 + jnp.tanh(0.5 * x))


def _dot(a, b):
    return jnp.dot(a, b, preferred_element_type=F32)


def _band_buckets():
    i = np.arange(BLOCK)[:, None]
    j = np.arange(2 * BLOCK)[None, :]
    d = np.clip(i + BLOCK - j, 0, None)
    max_exact = N_BUCKETS // 2
    large = max_exact + (np.log(np.maximum(d, 1) / max_exact) / np.log(MAX_DISTANCE / max_exact)
                         * (N_BUCKETS - max_exact)).astype(np.int32)
    large = np.minimum(large, N_BUCKETS - 1)
    return np.where(d < max_exact, d, large).astype(np.int32)


def _bias_table_kernel(rb_ref, bucket_ref, o_ref):
    h = pl.program_id(0)
    bucket = bucket_ref[...]
    acc = jnp.zeros(bucket.shape, F32)
    for b in range(N_BUCKETS):
        acc = jnp.where(bucket == b, rb_ref[b, h], acc)
    o_ref[...] = acc


def _bias_table(rel_bias):
    buckets = jnp.asarray(_band_buckets())
    return pl.pallas_call(
        _bias_table_kernel,
        grid=(N_HEADS,),
        in_specs=[
            pl.BlockSpec(memory_space=pltpu.SMEM),
            pl.BlockSpec((BLOCK, 2 * BLOCK), lambda h: (0, 0)),
        ],
        out_specs=pl.BlockSpec((None, BLOCK, 2 * BLOCK), lambda h: (h, 0, 0)),
        out_shape=jax.ShapeDtypeStruct((N_HEADS, BLOCK, 2 * BLOCK), F32),
        name="rel_bias_table",
    )(rel_bias, buckets)


def _mixer_kernel(sinks_ref, x_ref, gmix_ref, win_ref, gsgu_ref, ws_ref, bfull_ref, bias_ref,
                  wpa_ref, wpb_ref, wout_ref, o_ref,
                  h_ref, u_ref, vn_ref, qlo_ref, qhi_ref, kd_ref, vd_ref, ya_ref, yb_ref, m_ref):
    tm = x_ref.shape[0]
    nblk = tm // BLOCK
    s_idx = pl.program_id(1)

    @pl.when(s_idx == 0)
    def _():
        kd_ref[:, 0:BLOCK, :] = jnp.zeros((N_KV_HEADS, BLOCK, LANES), BF16)
        vd_ref[:, 0:BLOCK, :] = jnp.zeros((N_KV_HEADS, BLOCK, LANES), BF16)

    x = x_ref[...]
    r = lax.rsqrt(jnp.mean(x * x, axis=-1, keepdims=True) + EPS)
    h_ref[...] = (x * r * gmix_ref[...]).astype(BF16)

    u_ref[...] = _gelu_tanh(_dot(h_ref[...], win_ref[:, OFF_PU:OFF_PU + A_WIDTH]))

    pv = _gelu_tanh(_dot(h_ref[...], win_ref[:, OFF_PV:OFF_PV + A_WIDTH]))
    rv = lax.rsqrt(jnp.mean(pv * pv, axis=-1, keepdims=True) + EPS)
    vn_ref[...] = (pv * rv * gsgu_ref[...]).astype(BF16)

    q = _dot(h_ref[...], win_ref[:, OFF_Q:OFF_Q + Q_DIM]) * (HEAD_DIM ** -0.5)
    lane_q = lax.broadcasted_iota(jnp.int32, (tm, Q_DIM), 1)
    q_is_lo = (lane_q % LANES) < HEAD_DIM
    qlo_ref[...] = jnp.where(q_is_lo, q, 0.0).astype(BF16)
    qhi_ref[...] = jnp.where(q_is_lo, 0.0, q).astype(BF16)

    kvp = _dot(h_ref[...], win_ref[:, OFF_K:OFF_K + 2 * KV_DIM])
    lane_t = lax.broadcasted_iota(jnp.int32, (tm, LANES), 1)
    t_is_lo = lane_t < HEAD_DIM
    k = kvp[:, :KV_DIM]
    v = kvp[:, KV_DIM:]
    k_sw = pltpu.roll(k, HEAD_DIM, axis=1)
    v_sw = pltpu.roll(v, HEAD_DIM, axis=1)
    kd_ref[0, BLOCK:BLOCK + tm, :] = jnp.where(t_is_lo, k, k_sw).astype(BF16)
    kd_ref[1, BLOCK:BLOCK + tm, :] = jnp.where(t_is_lo, k_sw, k).astype(BF16)
    vd_ref[0, BLOCK:BLOCK + tm, :] = jnp.where(t_is_lo, v, v_sw).astype(BF16)
    vd_ref[1, BLOCK:BLOCK + tm, :] = jnp.where(t_is_lo, v_sw, v).astype(BF16)

    row_c = lax.broadcasted_iota(jnp.int32, (CHUNK, CHUNK), 0)
    col_c = lax.broadcasted_iota(jnp.int32, (CHUNK, CHUNK), 1)
    tril = col_c <= row_c
    wmask = [jnp.where(tril, ws_ref[g], 0.0).astype(BF16) for g in range(A_GROUPS)]

    row_i = lax.broadcasted_iota(jnp.int32, (BLOCK, 2 * BLOCK), 0)
    col_j = lax.broadcasted_iota(jnp.int32, (BLOCK, 2 * BLOCK), 1)
    band = (col_j > row_i) & (col_j <= row_i + BLOCK)
    band_first = band & ((col_j >= BLOCK) | (s_idx > 0))
    lane_b = lax.broadcasted_iota(jnp.int32, (BLOCK, LANES), 1)
    b_is_lo = lane_b < HEAD_DIM
    heads_per_kv = N_HEADS // N_KV_HEADS

    for b in range(nblk):
        rows = slice(b * BLOCK, (b + 1) * BLOCK)
        for g in range(A_GROUPS):
            cols = slice(g * CHUNK, (g + 1) * CHUNK)
            s = _dot(wmask[g], vn_ref[rows, cols]) + bfull_ref[:, cols]
            ya_ref[rows, cols] = (u_ref[rows, cols] * s).astype(BF16)

        mask = band_first if b == 0 else band
        for kv in range(N_KV_HEADS):
            q_parts = []
            for g in range(heads_per_kv):
                hh = kv * heads_per_kv + g
                src = qlo_ref if hh % 2 == 0 else qhi_ref
                c = hh // 2
                q_parts.append(src[rows, c * LANES:(c + 1) * LANES])
            q_stack = jnp.concatenate(q_parts, axis=0)
            keys = kd_ref[kv, b * BLOCK:(b + 2) * BLOCK, :]
            sc = lax.dot_general(q_stack, keys, (((1,), (1,)), ((), ())),
                                 preferred_element_type=F32)
            p_parts = []
            inv_l = []
            for g in range(heads_per_kv):
                hh = kv * heads_per_kv + g
                sg = sc[g * BLOCK:(g + 1) * BLOCK] + bias_ref[hh]
                sg = jnp.where(mask, sg, NEG_INF)
                sink = sinks_ref[hh]
                m = jnp.maximum(jnp.max(sg, axis=-1, keepdims=True), sink)
                p = jnp.exp(sg - m)
                l = jnp.sum(p, axis=-1, keepdims=True) + jnp.exp(sink - m)
                p_parts.append(p.astype(BF16))
                inv_l.append(1.0 / l)
            p_stack = jnp.concatenate(p_parts, axis=0)
            vals = vd_ref[kv, b * BLOCK:(b + 2) * BLOCK, :]
            o = _dot(p_stack, vals)
            for cc in range(heads_per_kv // 2):
                o_even = o[(2 * cc) * BLOCK:(2 * cc + 1) * BLOCK] * inv_l[2 * cc]
                o_odd = o[(2 * cc + 1) * BLOCK:(2 * cc + 2) * BLOCK] * inv_l[2 * cc + 1]
                col = kv * (heads_per_kv // 2) + cc
                yb_ref[rows, col * LANES:(col + 1) * LANES] = jnp.where(b_is_lo, o_even, o_odd).astype(BF16)

    kd_ref[:, 0:BLOCK, :] = kd_ref[:, tm:tm + BLOCK, :]
    vd_ref[:, 0:BLOCK, :] = vd_ref[:, tm:tm + BLOCK, :]

    for c in range(D_MODEL // MXU_COLS):
        cs = slice(c * MXU_COLS, (c + 1) * MXU_COLS)
        ga = _dot(h_ref[...], win_ref[:, OFF_GA + c * MXU_COLS:OFF_GA + (c + 1) * MXU_COLS])
        gb = _dot(h_ref[...], win_ref[:, OFF_GB + c * MXU_COLS:OFF_GB + (c + 1) * MXU_COLS])
        pa = _dot(ya_ref[...], wpa_ref[:, cs])
        pb = _dot(yb_ref[...], wpb_ref[:, cs])
        m_ref[:, cs] = (_sigmoid(ga) * pa + _sigmoid(gb) * pb).astype(BF16)

    for c in range(D_MODEL // MXU_COLS):
        cs = slice(c * MXU_COLS, (c + 1) * MXU_COLS)
        o_ref[:, cs] = x_ref[:, cs] + _dot(m_ref[...], wout_ref[:, cs])


def _resident(shape):
    nd = len(shape)
    return pl.BlockSpec(shape, lambda *_: (0,) * nd, pipeline_mode=pl.Buffered(1))


def _token_mixer(x2d, batch, seq, sinks, g_mix, w_in, g_sgu, w_s, b_full, bias_tbl, w_pa, w_pb, w_out):
    tm = TILE_ROWS
    ns = seq // tm
    row_spec = pl.BlockSpec((tm, D_MODEL), lambda b, s: (b * ns + s, 0))
    return pl.pallas_call(
        _mixer_kernel,
        grid=(batch, ns),
        in_specs=[
            pl.BlockSpec(memory_space=pltpu.SMEM),
            row_spec,
            _resident((1, D_MODEL)),
            _resident((D_MODEL, IN_DIM)),
            _resident((1, A_WIDTH)),
            _resident((A_GROUPS, CHUNK, CHUNK)),
            _resident((CHUNK, A_WIDTH)),
            _resident((N_HEADS, BLOCK, 2 * BLOCK)),
            _resident((A_WIDTH, D_MODEL)),
            _resident((Q_DIM, D_MODEL)),
            _resident((D_MODEL, D_MODEL)),
        ],
        out_specs=row_spec,
        out_shape=jax.ShapeDtypeStruct(x2d.shape, F32),
        scratch_shapes=[
            pltpu.VMEM((tm, D_MODEL), BF16),
            pltpu.VMEM((tm, A_WIDTH), F32),
            pltpu.VMEM((tm, A_WIDTH), BF16),
            pltpu.VMEM((tm, Q_DIM), BF16),
            pltpu.VMEM((tm, Q_DIM), BF16),
            pltpu.VMEM((N_KV_HEADS, BLOCK + tm, LANES), BF16),
            pltpu.VMEM((N_KV_HEADS, BLOCK + tm, LANES), BF16),
            pltpu.VMEM((tm, A_WIDTH), BF16),
            pltpu.VMEM((tm, Q_DIM), BF16),
            pltpu.VMEM((tm, D_MODEL), BF16),
        ],
        compiler_params=pltpu.CompilerParams(
            dimension_semantics=("arbitrary", "arbitrary"),
            vmem_limit_bytes=VMEM_LIMIT),
        name="token_mixer",
    )(sinks, x2d, g_mix, w_in, g_sgu, w_s, b_full, bias_tbl, w_pa, w_pb, w_out)


def _ffn_kernel(x_ref, gffn_ref, wup_ref, wconv_ref, bconv_ref, wdown_ref, gfin_ref, o_ref,
                h_ref, up_ref, act_ref):
    tm = x_ref.shape[0]
    s_idx = pl.program_id(1)
    halo = SUBLANES

    @pl.when(s_idx == 0)
    def _():
        up_ref[0:halo, :] = jnp.zeros((halo, 2 * D_FF), F32)

    x = x_ref[...]
    r = lax.rsqrt(jnp.mean(x * x, axis=-1, keepdims=True) + EPS)
    h_ref[...] = (x * r * gffn_ref[...]).astype(BF16)

    def conv(cols):
        y = bconv_ref[:, cols]
        for j in range(CONV_WIDTH):
            shift = CONV_WIDTH - 1 - j
            y = y + wconv_ref[j:j + 1, cols] * up_ref[halo - shift:halo - shift + tm, cols]
        return y

    for c in range(D_FF // F_CHUNK):
        gcols = slice(c * F_CHUNK, (c + 1) * F_CHUNK)
        vcols = slice(D_FF + c * F_CHUNK, D_FF + (c + 1) * F_CHUNK)
        up_ref[halo:halo + tm, gcols] = _dot(h_ref[...], wup_ref[:, gcols])
        up_ref[halo:halo + tm, vcols] = _dot(h_ref[...], wup_ref[:, vcols])
        gate = conv(gcols)
        val = conv(vcols)
        act_ref[:, gcols] = (gate * _sigmoid(gate) * val).astype(BF16)

    up_ref[0:halo, :] = up_ref[tm:tm + halo, :]

    ssq = jnp.zeros((tm, 1), F32)
    for c in range(D_MODEL // MXU_COLS):
        cs = slice(c * MXU_COLS, (c + 1) * MXU_COLS)
        y = x_ref[:, cs] + _dot(act_ref[...], wdown_ref[:, cs])
        ssq = ssq + jnp.sum(y * y, axis=-1, keepdims=True)
        o_ref[:, cs] = y
    rf = lax.rsqrt(ssq * (1.0 / D_MODEL) + EPS)
    o_ref[...] = o_ref[...] * rf * gfin_ref[...]


def _conv_ffn(x2d, batch, seq, g_ffn, w_up, w_conv, b_conv, w_down, g_final):
    tm = TILE_ROWS
    ns = seq // tm
    row_spec = pl.BlockSpec((tm, D_MODEL), lambda b, s: (b * ns + s, 0))
    return pl.pallas_call(
        _ffn_kernel,
        grid=(batch, ns),
        in_specs=[
            row_spec,
            _resident((1, D_MODEL)),
            _resident((D_MODEL, 2 * D_FF)),
            _resident((CONV_WIDTH, 2 * D_FF)),
            _resident((1, 2 * D_FF)),
            _resident((D_FF, D_MODEL)),
            _resident((1, D_MODEL)),
        ],
        out_specs=row_spec,
        out_shape=jax.ShapeDtypeStruct(x2d.shape, F32),
        scratch_shapes=[
            pltpu.VMEM((tm, D_MODEL), BF16),
            pltpu.VMEM((SUBLANES + tm, 2 * D_FF), F32),
            pltpu.VMEM((tm, D_FF), BF16),
        ],
        compiler_params=pltpu.CompilerParams(
            dimension_semantics=("arbitrary", "arbitrary"),
            vmem_limit_bytes=VMEM_LIMIT),
        name="conv_ffn",
    )(x2d, g_ffn, w_up, w_conv, b_conv, w_down, g_final)


def kernel(x, g_mix, w_in, g_sgu, w_s, b_s, sinks, rel_bias, w_pa, w_pb, w_out,
           g_ffn, w_up, w_conv, b_conv, w_down, g_final):
    batch, seq, d = x.shape
    assert w_in.shape[0] == 1 and d == D_MODEL and seq % TILE_ROWS == 0 and w_in.shape[2] == IN_DIM
    bias_tbl = _bias_table(rel_bias)
    x2d = x.reshape(batch * seq, d)
    b_full = jnp.repeat(jnp.transpose(b_s[0]), A_WIDTH // A_GROUPS, axis=1)
    x2d = _token_mixer(
        x2d, batch, seq, sinks[0], g_mix[0][None, :], w_in[0].astype(BF16), g_sgu[0][None, :],
        w_s[0], b_full, bias_tbl, w_pa[0].astype(BF16), w_pb[0].astype(BF16), w_out[0].astype(BF16))
    x2d = _conv_ffn(
        x2d, batch, seq, g_ffn[0][None, :], w_up[0].astype(BF16), w_conv[0], b_conv[0][None, :],
        w_down[0].astype(BF16), g_final[None, :])
    return x2d.reshape(batch, seq, d)
```

```python
import functools
import math

import jax
import jax.numpy as jnp
import numpy as np
from jax import lax
from jax.experimental import pallas as pl
from jax.experimental.pallas import tpu as pltpu

D_MODEL = 1024
CHUNK = 128
A_GROUPS = 4
A_WIDTH = 512
N_HEADS = 8
N_KV_HEADS = 2
HEAD_DIM = 64
Q_DIM = N_HEADS * HEAD_DIM
KV_DIM = N_KV_HEADS * HEAD_DIM
WINDOW = 128
BLOCK = 128
N_BUCKETS = 32
MAX_DISTANCE = 128
D_FF = 2816
CONV_WIDTH = 3
EPS = 1e-6
NEG_INF = -1e30

OFF_PU = 0
OFF_PV = OFF_PU + A_WIDTH
OFF_Q = OFF_PV + A_WIDTH
OFF_K = OFF_Q + Q_DIM
OFF_V = OFF_K + KV_DIM
OFF_GA = OFF_V + KV_DIM
OFF_GB = OFF_GA + D_MODEL
IN_DIM = OFF_GB + D_MODEL

LANES = 128
SUBLANES = 8
MXU_COLS = 256
TILE_ROWS = 256
F_CHUNK = 256
FFN_TILES = 2
VMEM_LIMIT = 56 * 1024 * 1024

F32 = jnp.float32
BF16 = jnp.bfloat16


def _gelu_tanh(x):
    c = math.sqrt(2.0 / math.pi)
    return x * (0.5 * (1.0 + jnp.tanh(c * (x + 0.044715 * (x * x * x)))))


def _sigmoid(x):
    return 0.5 * (1.0 + jnp.tanh(0.5 * x))


def _dot(a, b):
    return jnp.dot(a, b, preferred_element_type=F32)


def _band_buckets():
    i = np.arange(BLOCK)[:, None]
    j = np.arange(2 * BLOCK)[None, :]
    d = np.clip(i + BLOCK - j, 0, None)
    max_exact = N_BUCKETS // 2
    large = max_exact + (np.log(np.maximum(d, 1) / max_exact) / np.log(MAX_DISTANCE / max_exact)
                         * (N_BUCKETS - max_exact)).astype(np.int32)
    large = np.minimum(large, N_BUCKETS - 1)
    return np.where(d < max_exact, d, large).astype(np.int32)


def _bias_table_kernel(rb_ref, bucket_ref, o_ref):
    h = pl.program_id(0)
    bucket = bucket_ref[...]
    acc = jnp.zeros(bucket.shape, F32)
    for b in range(N_BUCKETS):
        acc = jnp.where(bucket == b, rb_ref[b, h], acc)
    o_ref[...] = acc


def _bias_table(rel_bias):
    buckets = jnp.asarray(_band_buckets())
    return pl.pallas_call(
        _bias_table_kernel,
        grid=(N_HEADS,),
        in_specs=[
            pl.BlockSpec(memory_space=pltpu.SMEM),
            pl.BlockSpec((BLOCK, 2 * BLOCK), lambda h: (0, 0)),
        ],
        out_specs=pl.BlockSpec((None, BLOCK, 2 * BLOCK), lambda h: (h, 0, 0)),
        out_shape=jax.ShapeDtypeStruct((N_HEADS, BLOCK, 2 * BLOCK), F32),
        name="rel_bias_table",
    )(rel_bias, buckets)


def _mixer_kernel(sinks_ref, x_ref, gmix_ref, win_ref, gsgu_ref, ws_ref, bfull_ref, bias_ref,
                  wpa_ref, wpb_ref, wout_ref, o_ref,
                  h_ref, u_ref, vn_ref, qlo_ref, qhi_ref, kd_ref, vd_ref, ya_ref, yb_ref, m_ref):
    tm = x_ref.shape[0]
    nblk = tm // BLOCK
    s_idx = pl.program_id(1)

    @pl.when(s_idx == 0)
    def _():
        kd_ref[:, 0:BLOCK, :] = jnp.zeros((N_KV_HEADS, BLOCK, LANES), BF16)
        vd_ref[:, 0:BLOCK, :] = jnp.zeros((N_KV_HEADS, BLOCK, LANES), BF16)

    x = x_ref[...]
    r = lax.rsqrt(jnp.mean(x * x, axis=-1, keepdims=True) + EPS)
    h_ref[...] = (x * r * gmix_ref[...]).astype(BF16)

    u_ref[...] = _gelu_tanh(_dot(h_ref[...], win_ref[:, OFF_PU:OFF_PU + A_WIDTH]))

    pv = _gelu_tanh(_dot(h_ref[...], win_ref[:, OFF_PV:OFF_PV + A_WIDTH]))
    rv = lax.rsqrt(jnp.mean(pv * pv, axis=-1, keepdims=True) + EPS)
    vn_ref[...] = (pv * rv * gsgu_ref[...]).astype(BF16)

    q = _dot(h_ref[...], win_ref[:, OFF_Q:OFF_Q + Q_DIM]) * (HEAD_DIM ** -0.5)
    lane_q = lax.broadcasted_iota(jnp.int32, (tm, Q_DIM), 1)
    q_is_lo = (lane_q % LANES) < HEAD_DIM
    qlo_ref[...] = jnp.where(q_is_lo, q, 0.0).astype(BF16)
    qhi_ref[...] = jnp.where(q_is_lo, 0.0, q).astype(BF16)

    kvp = _dot(h_ref[...], win_ref[:, OFF_K:OFF_K + 2 * KV_DIM])
    lane_t = lax.broadcasted_iota(jnp.int32, (tm, LANES), 1)
    t_is_lo = lane_t < HEAD_DIM
    k = kvp[:, :KV_DIM]
    v = kvp[:, KV_DIM:]
    k_sw = pltpu.roll(k, HEAD_DIM, axis=1)
    v_sw = pltpu.roll(v, HEAD_DIM, axis=1)
    kd_ref[0, BLOCK:BLOCK + tm, :] = jnp.where(t_is_lo, k, k_sw).astype(BF16)
    kd_ref[1, BLOCK:BLOCK + tm, :] = jnp.where(t_is_lo, k_sw, k).astype(BF16)
    vd_ref[0, BLOCK:BLOCK + tm, :] = jnp.where(t_is_lo, v, v_sw).astype(BF16)
    vd_ref[1, BLOCK:BLOCK + tm, :] = jnp.where(t_is_lo, v_sw, v).astype(BF16)

    row_c = lax.broadcasted_iota(jnp.int32, (CHUNK, CHUNK), 0)
    col_c = lax.broadcasted_iota(jnp.int32, (CHUNK, CHUNK), 1)
    tril = col_c <= row_c
    wmask = [jnp.where(tril, ws_ref[g], 0.0).astype(BF16) for g in range(A_GROUPS)]

    row_i = lax.broadcasted_iota(jnp.int32, (BLOCK, 2 * BLOCK), 0)
    col_j = lax.broadcasted_iota(jnp.int32, (BLOCK, 2 * BLOCK), 1)
    band = (col_j > row_i) & (col_j <= row_i + BLOCK)
    band_first = band & ((col_j >= BLOCK) | (s_idx > 0))
    lane_b = lax.broadcasted_iota(jnp.int32, (BLOCK, LANES), 1)
    b_is_lo = lane_b < HEAD_DIM
    heads_per_kv = N_HEADS // N_KV_HEADS

    for b in range(nblk):
        rows = slice(b * BLOCK, (b + 1) * BLOCK)
        for g in range(A_GROUPS):
            cols = slice(g * CHUNK, (g + 1) * CHUNK)
            s = _dot(wmask[g], vn_ref[rows, cols]) + bfull_ref[:, cols]
            ya_ref[rows, cols] = (u_ref[rows, cols] * s).astype(BF16)

        mask = band_first if b == 0 else band
        for kv in range(N_KV_HEADS):
            q_parts = []
            for g in range(heads_per_kv):
                hh = kv * heads_per_kv + g
                src = qlo_ref if hh % 2 == 0 else qhi_ref
                c = hh // 2
                q_parts.append(src[rows, c * LANES:(c + 1) * LANES])
            q_stack = jnp.concatenate(q_parts, axis=0)
            keys = kd_ref[kv, b * BLOCK:(b + 2) * BLOCK, :]
            sc = lax.dot_general(q_stack, keys, (((1,), (1,)), ((), ())),
                                 preferred_element_type=F32)
            p_parts = []
            inv_l = []
            for g in range(heads_per_kv):
                hh = kv * heads_per_kv + g
                sg = sc[g * BLOCK:(g + 1) * BLOCK] + bias_ref[hh]
                sg = jnp.where(mask, sg, NEG_INF)
                sink = sinks_ref[hh]
                m = jnp.maximum(jnp.max(sg, axis=-1, keepdims=True), sink)
                p = jnp.exp(sg - m)
                l = jnp.sum(p, axis=-1, keepdims=True) + jnp.exp(sink - m)
                p_parts.append(p.astype(BF16))
                inv_l.append(1.0 / l)
            p_stack = jnp.concatenate(p_parts, axis=0)
            vals = vd_ref[kv, b * BLOCK:(b + 2) * BLOCK, :]
            o = _dot(p_stack, vals)
            for cc in range(heads_per_kv // 2):
                o_even = o[(2 * cc) * BLOCK:(2 * cc + 1) * BLOCK] * inv_l[2 * cc]
                o_odd = o[(2 * cc + 1) * BLOCK:(2 * cc + 2) * BLOCK] * inv_l[2 * cc + 1]
                col = kv * (heads_per_kv // 2) + cc
                yb_ref[rows, col * LANES:(col + 1) * LANES] = jnp.where(b_is_lo, o_even, o_odd).astype(BF16)

    kd_ref[:, 0:BLOCK, :] = kd_ref[:, tm:tm + BLOCK, :]
    vd_ref[:, 0:BLOCK, :] = vd_ref[:, tm:tm + BLOCK, :]

    for c in range(D_MODEL // MXU_COLS):
        cs = slice(c * MXU_COLS, (c + 1) * MXU_COLS)
        ga = _dot(h_ref[...], win_ref[:, OFF_GA + c * MXU_COLS:OFF_GA + (c + 1) * MXU_COLS])
        gb = _dot(h_ref[...], win_ref[:, OFF_GB + c * MXU_COLS:OFF_GB + (c + 1) * MXU_COLS])
        pa = _dot(ya_ref[...], wpa_ref[:, cs])
        pb = _dot(yb_ref[...], wpb_ref[:, cs])
        m_ref[:, cs] = (_sigmoid(ga) * pa + _sigmoid(gb) * pb).astype(BF16)

    for c in range(D_MODEL // MXU_COLS):
        cs = slice(c * MXU_COLS, (c + 1) * MXU_COLS)
        o_ref[:, cs] = x_ref[:, cs] + _dot(m_ref[...], wout_ref[:, cs])


def _resident(shape):
    nd = len(shape)
    return pl.BlockSpec(shape, lambda *_: (0,) * nd, pipeline_mode=pl.Buffered(1))


def _token_mixer(x2d, batch, seq, sinks, g_mix, w_in, g_sgu, w_s, b_full, bias_tbl, w_pa, w_pb, w_out):
    tm = TILE_ROWS
    ns = seq // tm
    row_spec = pl.BlockSpec((tm, D_MODEL), lambda b, s: (b * ns + s, 0))
    return pl.pallas_call(
        _mixer_kernel,
        grid=(batch, ns),
        in_specs=[
            pl.BlockSpec(memory_space=pltpu.SMEM),
            row_spec,
            _resident((1, D_MODEL)),
            _resident((D_MODEL, IN_DIM)),
            _resident((1, A_WIDTH)),
            _resident((A_GROUPS, CHUNK, CHUNK)),
            _resident((CHUNK, A_WIDTH)),
            _resident((N_HEADS, BLOCK, 2 * BLOCK)),
            _resident((A_WIDTH, D_MODEL)),
            _resident((Q_DIM, D_MODEL)),
            _resident((D_MODEL, D_MODEL)),
        ],
        out_specs=row_spec,
        out_shape=jax.ShapeDtypeStruct(x2d.shape, F32),
        scratch_shapes=[
            pltpu.VMEM((tm, D_MODEL), BF16),
            pltpu.VMEM((tm, A_WIDTH), F32),
            pltpu.VMEM((tm, A_WIDTH), BF16),
            pltpu.VMEM((tm, Q_DIM), BF16),
            pltpu.VMEM((tm, Q_DIM), BF16),
            pltpu.VMEM((N_KV_HEADS, BLOCK + tm, LANES), BF16),
            pltpu.VMEM((N_KV_HEADS, BLOCK + tm, LANES), BF16),
            pltpu.VMEM((tm, A_WIDTH), BF16),
            pltpu.VMEM((tm, Q_DIM), BF16),
            pltpu.VMEM((tm, D_MODEL), BF16),
        ],
        compiler_params=pltpu.CompilerParams(
            dimension_semantics=("arbitrary", "arbitrary"),
            vmem_limit_bytes=VMEM_LIMIT),
        name="token_mixer",
    )(sinks, x2d, g_mix, w_in, g_sgu, w_s, b_full, bias_tbl, w_pa, w_pb, w_out)


def _ffn_kernel(x_ref, gffn_ref, wup_ref, wconv_ref, bconv_ref, wdown_ref, gfin_ref, o_ref,
                skew_in_ref, skew_out_ref, xp_ref, h_ref, carry_ref, act_ref, *, steps_per_seq):
    tm = TILE_ROWS
    n_tiles = x_ref.shape[0] // tm
    n_grp = tm // SUBLANES
    pitch = n_grp + SUBLANES
    n_lane_tiles = D_MODEL // LANES
    n_chunks = D_FF // F_CHUNK
    n_down = D_MODEL // MXU_COLS

    @pl.when(pl.program_id(0) % steps_per_seq == 0)
    def _():
        carry_ref[...] = jnp.zeros(carry_ref.shape, F32)

    def prologue(t):
        for lt in range(n_lane_tiles):
            for k in range(SUBLANES):
                skew_in_ref[t, lt, pitch * k:pitch * k + n_grp, :] = (
                    x_ref[t * tm + n_grp * k:t * tm + n_grp * (k + 1), lt * LANES:(lt + 1) * LANES])
        x = jnp.concatenate(
            [jnp.concatenate([skew_in_ref[t, lt, pl.ds(j, SUBLANES, stride=pitch), :] for lt in range(n_lane_tiles)], axis=1)
             for j in range(n_grp)], axis=0)
        xp_ref[t] = x
        r = lax.rsqrt(jnp.mean(x * x, axis=-1, keepdims=True) + EPS)
        h_ref[t] = (x * r * gffn_ref[...]).astype(BF16)

    first_sublane = lax.broadcasted_iota(jnp.int32, (SUBLANES, F_CHUNK), 0) == 0

    def conv(t, cols):
        up = _dot(h_ref[t], wup_ref[:, cols])
        last = up[tm - SUBLANES:tm]
        last2 = up[tm - 2 * SUBLANES:tm - SUBLANES]

        def wrap(cur, prev):
            return jnp.where(first_sublane, pltpu.roll(prev, 1, axis=0), pltpu.roll(cur, 1, axis=0))

        m1 = wrap(last, carry_ref[1, :, cols])
        m2 = wrap(last2, carry_ref[0, :, cols])
        carry_ref[0, :, cols] = last2
        carry_ref[1, :, cols] = last
        up1 = jnp.concatenate([m1, up[:tm - SUBLANES]], axis=0)
        up2 = jnp.concatenate([m2, m1, up[:tm - 2 * SUBLANES]], axis=0)
        return (bconv_ref[:, cols] + wconv_ref[2:3, cols] * up
                + wconv_ref[1:2, cols] * up1 + wconv_ref[0:1, cols] * up2)

    def up_chunk(t, c):
        gcols = slice(c * F_CHUNK, (c + 1) * F_CHUNK)
        gate = conv(t, gcols)
        val = conv(t, slice(D_FF + c * F_CHUNK, D_FF + (c + 1) * F_CHUNK))
        hg = 0.5 * gate
        act_ref[t, :, gcols] = ((hg + hg * jnp.tanh(hg)) * val).astype(BF16)

    def down_block(t, j):
        cs = slice(j * MXU_COLS, (j + 1) * MXU_COLS)
        return xp_ref[t, :, cs] + _dot(act_ref[t], wdown_ref[:, cs])

    def epilogue(t, ys):
        ssq = sum(jnp.sum(y * y, axis=-1, keepdims=True) for y in ys)
        rf = lax.rsqrt(ssq * (1.0 / D_MODEL) + EPS)
        for jb, y in enumerate(ys):
            cs = slice(jb * MXU_COLS, (jb + 1) * MXU_COLS)
            out = y * rf * gfin_ref[:, cs]
            for ll in range(MXU_COLS // LANES):
                lt = jb * (MXU_COLS // LANES) + ll
                for j in range(n_grp):
                    skew_out_ref[t, lt, pl.ds(j, SUBLANES, stride=pitch), :] = (
                        out[SUBLANES * j:SUBLANES * (j + 1), ll * LANES:(ll + 1) * LANES])
        for lt in range(n_lane_tiles):
            for k in range(SUBLANES):
                o_ref[t * tm + n_grp * k:t * tm + n_grp * (k + 1), lt * LANES:(lt + 1) * LANES] = (
                    skew_out_ref[t, lt, pitch * k:pitch * k + n_grp, :])

    down_at = {(k + 1) * n_chunks // (n_down + 1): k for k in range(n_down)}
    assert len(down_at) == n_down
    prologue(0)
    for t in range(n_tiles):
        if t + 1 < n_tiles:
            prologue(t + 1)
        ys = {}
        for c in range(n_chunks):
            up_chunk(t, c)
            if t >= 1 and c in down_at:
                ys[down_at[c]] = down_block(t - 1, down_at[c])
        if t >= 1:
            epilogue(t - 1, [ys[j] for j in range(n_down)])
    last = n_tiles - 1
    epilogue(last, [down_block(last, j) for j in range(n_down)])


def _conv_ffn(x2d, batch, seq, g_ffn, w_up, w_conv, b_conv, w_down, g_final):
    rows = FFN_TILES * TILE_ROWS
    steps_per_seq = seq // rows
    row_spec = pl.BlockSpec((rows, D_MODEL), lambda i: (i, 0))
    skew_rows = TILE_ROWS + SUBLANES * SUBLANES
    return pl.pallas_call(
        functools.partial(_ffn_kernel, steps_per_seq=steps_per_seq),
        grid=(batch * steps_per_seq,),
        in_specs=[
            row_spec,
            _resident((1, D_MODEL)),
            _resident((D_MODEL, 2 * D_FF)),
            _resident((CONV_WIDTH, 2 * D_FF)),
            _resident((1, 2 * D_FF)),
            _resident((D_FF, D_MODEL)),
            _resident((1, D_MODEL)),
        ],
        out_specs=row_spec,
        out_shape=jax.ShapeDtypeStruct(x2d.shape, F32),
        scratch_shapes=[
            pltpu.VMEM((FFN_TILES, D_MODEL // LANES, skew_rows, LANES), F32),
            pltpu.VMEM((FFN_TILES, D_MODEL // LANES, skew_rows, LANES), F32),
            pltpu.VMEM((FFN_TILES, TILE_ROWS, D_MODEL), F32),
            pltpu.VMEM((FFN_TILES, TILE_ROWS, D_MODEL), BF16),
            pltpu.VMEM((CONV_WIDTH - 1, SUBLANES, 2 * D_FF), F32),
            pltpu.VMEM((FFN_TILES, TILE_ROWS, D_FF), BF16),
        ],
        compiler_params=pltpu.CompilerParams(
            dimension_semantics=("arbitrary",),
            vmem_limit_bytes=VMEM_LIMIT),
        name="conv_ffn",
    )(x2d, g_ffn, w_up, w_conv, b_conv, w_down, g_final)


def kernel(x, g_mix, w_in, g_sgu, w_s, b_s, sinks, rel_bias, w_pa, w_pb, w_out,
           g_ffn, w_up, w_conv, b_conv, w_down, g_final):
    batch, seq, d = x.shape
    assert w_in.shape[0] == 1 and d == D_MODEL and seq % (FFN_TILES * TILE_ROWS) == 0 and w_in.shape[2] == IN_DIM
    bias_tbl = _bias_table(rel_bias)
    x2d = x.reshape(batch * seq, d)
    b_full = jnp.repeat(jnp.transpose(b_s[0]), A_WIDTH // A_GROUPS, axis=1)
    x2d = _token_mixer(
        x2d, batch, seq, sinks[0], g_mix[0][None, :], w_in[0].astype(BF16), g_sgu[0][None, :],
        w_s[0], b_full, bias_tbl, w_pa[0].astype(BF16), w_pb[0].astype(BF16), w_out[0].astype(BF16))
    x2d = _conv_ffn(
        x2d, batch, seq, g_ffn[0][None, :], w_up[0].astype(BF16), w_conv[0], b_conv[0][None, :],
        w_down[0].astype(BF16), g_final[None, :])
    return x2d.reshape(batch, seq, d)
```

```python
import functools
import math

import jax
import jax.numpy as jnp
import numpy as np
from jax import lax
from jax.experimental import pallas as pl
from jax.experimental.pallas import tpu as pltpu

D_MODEL = 1024
CHUNK = 128
A_GROUPS = 4
A_WIDTH = 512
N_HEADS = 8
N_KV_HEADS = 2
HEAD_DIM = 64
Q_DIM = N_HEADS * HEAD_DIM
KV_DIM = N_KV_HEADS * HEAD_DIM
WINDOW = 128
BLOCK = 128
N_BUCKETS = 32
MAX_DISTANCE = 128
D_FF = 2816
CONV_WIDTH = 3
EPS = 1e-6
NEG_INF = -1e30

OFF_PU = 0
OFF_PV = OFF_PU + A_WIDTH
OFF_Q = OFF_PV + A_WIDTH
OFF_K = OFF_Q + Q_DIM
OFF_V = OFF_K + KV_DIM
OFF_GA = OFF_V + KV_DIM
OFF_GB = OFF_GA + D_MODEL
IN_DIM = OFF_GB + D_MODEL

LANES = 128
SUBLANES = 8
MXU_COLS = 256
TILE_ROWS = 256
F_CHUNK = 256
FFN_TILES = 2
MIX_TILES = 2
VMEM_LIMIT = 56 * 1024 * 1024

F32 = jnp.float32
BF16 = jnp.bfloat16


def _gelu_tanh(x):
    c = math.sqrt(2.0 / math.pi)
    return x * (0.5 * (1.0 + jnp.tanh(c * (x + 0.044715 * (x * x * x)))))


def _sigmoid(x):
    return 0.5 * (1.0 + jnp.tanh(0.5 * x))


def _dot(a, b):
    return jnp.dot(a, b, preferred_element_type=F32)


def _band_buckets():
    i = np.arange(BLOCK)[:, None]
    j = np.arange(2 * BLOCK)[None, :]
    d = np.clip(i + BLOCK - j, 0, None)
    max_exact = N_BUCKETS // 2
    large = max_exact + (np.log(np.maximum(d, 1) / max_exact) / np.log(MAX_DISTANCE / max_exact)
                         * (N_BUCKETS - max_exact)).astype(np.int32)
    large = np.minimum(large, N_BUCKETS - 1)
    return np.where(d < max_exact, d, large).astype(np.int32)


def _bias_table_kernel(rb_ref, bucket_ref, o_ref):
    h = pl.program_id(0)
    bucket = bucket_ref[...]
    acc = jnp.zeros(bucket.shape, F32)
    for b in range(N_BUCKETS):
        acc = jnp.where(bucket == b, rb_ref[b, h], acc)
    o_ref[...] = acc


def _bias_table(rel_bias):
    buckets = jnp.asarray(_band_buckets())
    return pl.pallas_call(
        _bias_table_kernel,
        grid=(N_HEADS,),
        in_specs=[
            pl.BlockSpec(memory_space=pltpu.SMEM),
            pl.BlockSpec((BLOCK, 2 * BLOCK), lambda h: (0, 0)),
        ],
        out_specs=pl.BlockSpec((None, BLOCK, 2 * BLOCK), lambda h: (h, 0, 0)),
        out_shape=jax.ShapeDtypeStruct((N_HEADS, BLOCK, 2 * BLOCK), F32),
        name="rel_bias_table",
    )(rel_bias, buckets)


def _mixer_kernel(sinks_ref, x_ref, gmix_ref, win_ref, gsgu_ref, ws_ref, bfull_ref, bias_ref,
                  wpa_ref, wpb_ref, wout_ref, o_ref,
                  h_ref, u_ref, vn_ref, qlo_ref, qhi_ref, kd_ref, vd_ref, ya_ref, yb_ref, m_ref):
    tm = TILE_ROWS
    n_tiles = x_ref.shape[0] // tm
    blocks_per_tile = tm // BLOCK
    s_idx = pl.program_id(1)
    heads_per_kv = N_HEADS // N_KV_HEADS
    n_out = D_MODEL // MXU_COLS

    @pl.when(s_idx == 0)
    def _():
        kd_ref[:, 0:BLOCK, :] = jnp.zeros((N_KV_HEADS, BLOCK, LANES), BF16)
        vd_ref[:, 0:BLOCK, :] = jnp.zeros((N_KV_HEADS, BLOCK, LANES), BF16)

    row_c = lax.broadcasted_iota(jnp.int32, (CHUNK, CHUNK), 0)
    col_c = lax.broadcasted_iota(jnp.int32, (CHUNK, CHUNK), 1)
    tril = col_c <= row_c
    wmask = [jnp.where(tril, ws_ref[g], 0.0).astype(BF16) for g in range(A_GROUPS)]
    row_i = lax.broadcasted_iota(jnp.int32, (BLOCK, 2 * BLOCK), 0)
    col_j = lax.broadcasted_iota(jnp.int32, (BLOCK, 2 * BLOCK), 1)
    band = (col_j > row_i) & (col_j <= row_i + BLOCK)
    band_first = band & ((col_j >= BLOCK) | (s_idx > 0))
    b_is_lo = lax.broadcasted_iota(jnp.int32, (BLOCK, LANES), 1) < HEAD_DIM
    q_is_lo = (lax.broadcasted_iota(jnp.int32, (tm, Q_DIM), 1) % LANES) < HEAD_DIM
    t_is_lo = lax.broadcasted_iota(jnp.int32, (tm, LANES), 1) < HEAD_DIM

    def norm(t):
        x = x_ref[t * tm:(t + 1) * tm, :]
        r = lax.rsqrt(jnp.mean(x * x, axis=-1, keepdims=True) + EPS)
        h_ref[t] = (x * r * gmix_ref[...]).astype(BF16)

    def proj_u(t):
        u_ref[t] = _gelu_tanh(_dot(h_ref[t], win_ref[:, OFF_PU:OFF_PU + A_WIDTH]))

    def proj_v(t):
        pv = _gelu_tanh(_dot(h_ref[t], win_ref[:, OFF_PV:OFF_PV + A_WIDTH]))
        rv = lax.rsqrt(jnp.mean(pv * pv, axis=-1, keepdims=True) + EPS)
        vn_ref[t] = (pv * rv * gsgu_ref[...]).astype(BF16)

    def proj_q(t):
        q = _dot(h_ref[t], win_ref[:, OFF_Q:OFF_Q + Q_DIM]) * (HEAD_DIM ** -0.5)
        qlo_ref[t] = jnp.where(q_is_lo, q, 0.0).astype(BF16)
        qhi_ref[t] = jnp.where(q_is_lo, 0.0, q).astype(BF16)

    def proj_kv(t):
        kvp = _dot(h_ref[t], win_ref[:, OFF_K:OFF_K + 2 * KV_DIM])
        k = kvp[:, :KV_DIM]
        v = kvp[:, KV_DIM:]
        k_sw = pltpu.roll(k, HEAD_DIM, axis=1)
        v_sw = pltpu.roll(v, HEAD_DIM, axis=1)
        rows = slice(BLOCK + t * tm, BLOCK + (t + 1) * tm)
        kd_ref[0, rows, :] = jnp.where(t_is_lo, k, k_sw).astype(BF16)
        kd_ref[1, rows, :] = jnp.where(t_is_lo, k_sw, k).astype(BF16)
        vd_ref[0, rows, :] = jnp.where(t_is_lo, v, v_sw).astype(BF16)
        vd_ref[1, rows, :] = jnp.where(t_is_lo, v_sw, v).astype(BF16)

    def sgu(t, b):
        rows = slice(b * BLOCK, (b + 1) * BLOCK)
        for g in range(A_GROUPS):
            cols = slice(g * CHUNK, (g + 1) * CHUNK)
            s = _dot(wmask[g], vn_ref[t, rows, cols]) + bfull_ref[:, cols]
            ya_ref[t, rows, cols] = (u_ref[t, rows, cols] * s).astype(BF16)

    def attn(t, b, kv):
        rows = slice(b * BLOCK, (b + 1) * BLOCK)
        bb = t * blocks_per_tile + b
        mask = band_first if bb == 0 else band
        q_parts = []
        for g in range(heads_per_kv):
            hh = kv * heads_per_kv + g
            src = qlo_ref if hh % 2 == 0 else qhi_ref
            c = hh // 2
            q_parts.append(src[t, rows, c * LANES:(c + 1) * LANES])
        q_stack = jnp.concatenate(q_parts, axis=0)
        keys = kd_ref[kv, bb * BLOCK:(bb + 2) * BLOCK, :]
        sc = lax.dot_general(q_stack, keys, (((1,), (1,)), ((), ())),
                             preferred_element_type=F32)
        p_parts = []
        inv_l = []
        for g in range(heads_per_kv):
            hh = kv * heads_per_kv + g
            sg = sc[g * BLOCK:(g + 1) * BLOCK] + bias_ref[hh]
            sg = jnp.where(mask, sg, NEG_INF)
            sink = sinks_ref[hh]
            m = jnp.maximum(jnp.max(sg, axis=-1, keepdims=True), sink)
            p = jnp.exp(sg - m)
            l = jnp.sum(p, axis=-1, keepdims=True) + jnp.exp(sink - m)
            p_parts.append(p.astype(BF16))
            inv_l.append(1.0 / l)
        p_stack = jnp.concatenate(p_parts, axis=0)
        vals = vd_ref[kv, bb * BLOCK:(bb + 2) * BLOCK, :]
        o = _dot(p_stack, vals)
        for cc in range(heads_per_kv // 2):
            o_even = o[(2 * cc) * BLOCK:(2 * cc + 1) * BLOCK] * inv_l[2 * cc]
            o_odd = o[(2 * cc + 1) * BLOCK:(2 * cc + 2) * BLOCK] * inv_l[2 * cc + 1]
            col = kv * (heads_per_kv // 2) + cc
            yb_ref[t, rows, col * LANES:(col + 1) * LANES] = jnp.where(b_is_lo, o_even, o_odd).astype(BF16)

    def merge(t, c):
        cs = slice(c * MXU_COLS, (c + 1) * MXU_COLS)
        ga = _dot(h_ref[t], win_ref[:, OFF_GA + c * MXU_COLS:OFF_GA + (c + 1) * MXU_COLS])
        gb = _dot(h_ref[t], win_ref[:, OFF_GB + c * MXU_COLS:OFF_GB + (c + 1) * MXU_COLS])
        pa = _dot(ya_ref[t], wpa_ref[:, cs])
        pb = _dot(yb_ref[t], wpb_ref[:, cs])
        m_ref[t, :, cs] = (_sigmoid(ga) * pa + _sigmoid(gb) * pb).astype(BF16)

    def out_proj(t, c):
        cs = slice(c * MXU_COLS, (c + 1) * MXU_COLS)
        o_ref[t * tm:(t + 1) * tm, cs] = x_ref[t * tm:(t + 1) * tm, cs] + _dot(m_ref[t], wout_ref[:, cs])

    P = functools.partial

    def stage1(t):
        return [P(proj_u, t), P(proj_v, t), P(proj_q, t), P(proj_kv, t)]

    def stage2(t):
        items = []
        for b in range(blocks_per_tile):
            items += [P(sgu, t, b)] + [P(attn, t, b, kv) for kv in range(N_KV_HEADS)]
        return items

    def stage3(t):
        return [P(merge, t, c) for c in range(n_out)]

    def stage4(t):
        return [P(out_proj, t, c) for c in range(n_out)]

    def interleave(heavy, light):
        n, m = len(heavy), len(light)
        done = 0
        for i, item in enumerate(heavy):
            item()
            upto = (i + 1) * m // n
            for light_item in light[done:upto]:
                light_item()
            done = upto

    def run(items):
        for item in items:
            item()

    assert n_tiles == 2
    norm(0)
    run(stage1(0))
    norm(1)
    interleave(stage1(1), stage2(0))
    interleave(stage3(0), stage2(1))
    run(stage3(1))
    run(stage4(0))
    run(stage4(1))

    rows_all = n_tiles * tm
    kd_ref[:, 0:BLOCK, :] = kd_ref[:, rows_all:rows_all + BLOCK, :]
    vd_ref[:, 0:BLOCK, :] = vd_ref[:, rows_all:rows_all + BLOCK, :]


def _resident(shape):
    nd = len(shape)
    return pl.BlockSpec(shape, lambda *_: (0,) * nd, pipeline_mode=pl.Buffered(1))


def _token_mixer(x2d, batch, seq, sinks, g_mix, w_in, g_sgu, w_s, b_full, bias_tbl, w_pa, w_pb, w_out):
    tm = TILE_ROWS
    rows = MIX_TILES * tm
    ns = seq // rows
    row_spec = pl.BlockSpec((rows, D_MODEL), lambda b, s: (b * ns + s, 0))
    return pl.pallas_call(
        _mixer_kernel,
        grid=(batch, ns),
        in_specs=[
            pl.BlockSpec(memory_space=pltpu.SMEM),
            row_spec,
            _resident((1, D_MODEL)),
            _resident((D_MODEL, IN_DIM)),
            _resident((1, A_WIDTH)),
            _resident((A_GROUPS, CHUNK, CHUNK)),
            _resident((CHUNK, A_WIDTH)),
            _resident((N_HEADS, BLOCK, 2 * BLOCK)),
            _resident((A_WIDTH, D_MODEL)),
            _resident((Q_DIM, D_MODEL)),
            _resident((D_MODEL, D_MODEL)),
        ],
        out_specs=row_spec,
        out_shape=jax.ShapeDtypeStruct(x2d.shape, F32),
        scratch_shapes=[
            pltpu.VMEM((MIX_TILES, tm, D_MODEL), BF16),
            pltpu.VMEM((MIX_TILES, tm, A_WIDTH), F32),
            pltpu.VMEM((MIX_TILES, tm, A_WIDTH), BF16),
            pltpu.VMEM((MIX_TILES, tm, Q_DIM), BF16),
            pltpu.VMEM((MIX_TILES, tm, Q_DIM), BF16),
            pltpu.VMEM((N_KV_HEADS, BLOCK + rows, LANES), BF16),
            pltpu.VMEM((N_KV_HEADS, BLOCK + rows, LANES), BF16),
            pltpu.VMEM((MIX_TILES, tm, A_WIDTH), BF16),
            pltpu.VMEM((MIX_TILES, tm, Q_DIM), BF16),
            pltpu.VMEM((MIX_TILES, tm, D_MODEL), BF16),
        ],
        compiler_params=pltpu.CompilerParams(
            dimension_semantics=("arbitrary", "arbitrary"),
            vmem_limit_bytes=VMEM_LIMIT),
        name="token_mixer",
    )(sinks, x2d, g_mix, w_in, g_sgu, w_s, b_full, bias_tbl, w_pa, w_pb, w_out)


def _ffn_kernel(x_ref, gffn_ref, wup_ref, wconv_ref, bconv_ref, wdown_ref, gfin_ref, o_ref,
                skew_in_ref, skew_out_ref, xp_ref, h_ref, carry_ref, act_ref, *, steps_per_seq):
    tm = TILE_ROWS
    n_tiles = x_ref.shape[0] // tm
    n_grp = tm // SUBLANES
    pitch = n_grp + SUBLANES
    n_lane_tiles = D_MODEL // LANES
    n_chunks = D_FF // F_CHUNK
    n_down = D_MODEL // MXU_COLS

    @pl.when(pl.program_id(0) % steps_per_seq == 0)
    def _():
        carry_ref[...] = jnp.zeros(carry_ref.shape, F32)

    def prologue(t):
        for lt in range(n_lane_tiles):
            for k in range(SUBLANES):
                skew_in_ref[t, lt, pitch * k:pitch * k + n_grp, :] = (
                    x_ref[t * tm + n_grp * k:t * tm + n_grp * (k + 1), lt * LANES:(lt + 1) * LANES])
        x = jnp.concatenate(
            [jnp.concatenate([skew_in_ref[t, lt, pl.ds(j, SUBLANES, stride=pitch), :] for lt in range(n_lane_tiles)], axis=1)
             for j in range(n_grp)], axis=0)
        xp_ref[t] = x
        r = lax.rsqrt(jnp.mean(x * x, axis=-1, keepdims=True) + EPS)
        h_ref[t] = (x * r * gffn_ref[...]).astype(BF16)

    first_sublane = lax.broadcasted_iota(jnp.int32, (SUBLANES, F_CHUNK), 0) == 0

    def conv(t, cols):
        up = _dot(h_ref[t], wup_ref[:, cols])
        last = up[tm - SUBLANES:tm]
        last2 = up[tm - 2 * SUBLANES:tm - SUBLANES]

        def wrap(cur, prev):
            return jnp.where(first_sublane, pltpu.roll(prev, 1, axis=0), pltpu.roll(cur, 1, axis=0))

        m1 = wrap(last, carry_ref[1, :, cols])
        m2 = wrap(last2, carry_ref[0, :, cols])
        carry_ref[0, :, cols] = last2
        carry_ref[1, :, cols] = last
        up1 = jnp.concatenate([m1, up[:tm - SUBLANES]], axis=0)
        up2 = jnp.concatenate([m2, m1, up[:tm - 2 * SUBLANES]], axis=0)
        return (bconv_ref[:, cols] + wconv_ref[2:3, cols] * up
                + wconv_ref[1:2, cols] * up1 + wconv_ref[0:1, cols] * up2)

    def up_chunk(t, c):
        gcols = slice(c * F_CHUNK, (c + 1) * F_CHUNK)
        gate = conv(t, gcols)
        val = conv(t, slice(D_FF + c * F_CHUNK, D_FF + (c + 1) * F_CHUNK))
        hg = 0.5 * gate
        act_ref[t, :, gcols] = ((hg + hg * jnp.tanh(hg)) * val).astype(BF16)

    def down_block(t, j):
        cs = slice(j * MXU_COLS, (j + 1) * MXU_COLS)
        return xp_ref[t, :, cs] + _dot(act_ref[t], wdown_ref[:, cs])

    def epilogue(t, ys):
        ssq = sum(jnp.sum(y * y, axis=-1, keepdims=True) for y in ys)
        rf = lax.rsqrt(ssq * (1.0 / D_MODEL) + EPS)
        for jb, y in enumerate(ys):
            cs = slice(jb * MXU_COLS, (jb + 1) * MXU_COLS)
            out = y * rf * gfin_ref[:, cs]
            for ll in range(MXU_COLS // LANES):
                lt = jb * (MXU_COLS // LANES) + ll
                for j in range(n_grp):
                    skew_out_ref[t, lt, pl.ds(j, SUBLANES, stride=pitch), :] = (
                        out[SUBLANES * j:SUBLANES * (j + 1), ll * LANES:(ll + 1) * LANES])
        for lt in range(n_lane_tiles):
            for k in range(SUBLANES):
                o_ref[t * tm + n_grp * k:t * tm + n_grp * (k + 1), lt * LANES:(lt + 1) * LANES] = (
                    skew_out_ref[t, lt, pitch * k:pitch * k + n_grp, :])

    down_at = {(k + 1) * n_chunks // (n_down + 1): k for k in range(n_down)}
    assert len(down_at) == n_down
    prologue(0)
    for t in range(n_tiles):
        if t + 1 < n_tiles:
            prologue(t + 1)
        ys = {}
        for c in range(n_chunks):
            up_chunk(t, c)
            if t >= 1 and c in down_at:
                ys[down_at[c]] = down_block(t - 1, down_at[c])
        if t >= 1:
            epilogue(t - 1, [ys[j] for j in range(n_down)])
    last = n_tiles - 1
    epilogue(last, [down_block(last, j) for j in range(n_down)])


def _conv_ffn(x2d, batch, seq, g_ffn, w_up, w_conv, b_conv, w_down, g_final):
    rows = FFN_TILES * TILE_ROWS
    steps_per_seq = seq // rows
    row_spec = pl.BlockSpec((rows, D_MODEL), lambda i: (i, 0))
    skew_rows = TILE_ROWS + SUBLANES * SUBLANES
    return pl.pallas_call(
        functools.partial(_ffn_kernel, steps_per_seq=steps_per_seq),
        grid=(batch * steps_per_seq,),
        in_specs=[
            row_spec,
            _resident((1, D_MODEL)),
            _resident((D_MODEL, 2 * D_FF)),
            _resident((CONV_WIDTH, 2 * D_FF)),
            _resident((1, 2 * D_FF)),
            _resident((D_FF, D_MODEL)),
            _resident((1, D_MODEL)),
        ],
        out_specs=row_spec,
        out_shape=jax.ShapeDtypeStruct(x2d.shape, F32),
        scratch_shapes=[
            pltpu.VMEM((FFN_TILES, D_MODEL // LANES, skew_rows, LANES), F32),
            pltpu.VMEM((FFN_TILES, D_MODEL // LANES, skew_rows, LANES), F32),
            pltpu.VMEM((FFN_TILES, TILE_ROWS, D_MODEL), F32),
            pltpu.VMEM((FFN_TILES, TILE_ROWS, D_MODEL), BF16),
            pltpu.VMEM((CONV_WIDTH - 1, SUBLANES, 2 * D_FF), F32),
            pltpu.VMEM((FFN_TILES, TILE_ROWS, D_FF), BF16),
        ],
        compiler_params=pltpu.CompilerParams(
            dimension_semantics=("arbitrary",),
            vmem_limit_bytes=VMEM_LIMIT),
        name="conv_ffn",
    )(x2d, g_ffn, w_up, w_conv, b_conv, w_down, g_final)


def kernel(x, g_mix, w_in, g_sgu, w_s, b_s, sinks, rel_bias, w_pa, w_pb, w_out,
           g_ffn, w_up, w_conv, b_conv, w_down, g_final):
    batch, seq, d = x.shape
    assert w_in.shape[0] == 1 and d == D_MODEL and seq % (max(FFN_TILES, MIX_TILES) * TILE_ROWS) == 0 and w_in.shape[2] == IN_DIM
    bias_tbl = _bias_table(rel_bias)
    x2d = x.reshape(batch * seq, d)
    b_full = jnp.repeat(jnp.transpose(b_s[0]), A_WIDTH // A_GROUPS, axis=1)
    x2d = _token_mixer(
        x2d, batch, seq, sinks[0], g_mix[0][None, :], w_in[0].astype(BF16), g_sgu[0][None, :],
        w_s[0], b_full, bias_tbl, w_pa[0].astype(BF16), w_pb[0].astype(BF16), w_out[0].astype(BF16))
    x2d = _conv_ffn(
        x2d, batch, seq, g_ffn[0][None, :], w_up[0].astype(BF16), w_conv[0], b_conv[0][None, :],
        w_down[0].astype(BF16), g_final[None, :])
    return x2d.reshape(batch, seq, d)
```

```python
import functools
import math

import jax
import jax.numpy as jnp
import numpy as np
from jax import lax
from jax.experimental import pallas as pl
from jax.experimental.pallas import tpu as pltpu

D_MODEL = 1024
CHUNK = 128
A_GROUPS = 4
A_WIDTH = 512
N_HEADS = 8
N_KV_HEADS = 2
HEAD_DIM = 64
Q_DIM = N_HEADS * HEAD_DIM
KV_DIM = N_KV_HEADS * HEAD_DIM
WINDOW = 128
BLOCK = 128
N_BUCKETS = 32
MAX_DISTANCE = 128
D_FF = 2816
CONV_WIDTH = 3
EPS = 1e-6
NEG_INF = -1e30

OFF_PU = 0
OFF_PV = OFF_PU + A_WIDTH
OFF_Q = OFF_PV + A_WIDTH
OFF_K = OFF_Q + Q_DIM
OFF_V = OFF_K + KV_DIM
OFF_GA = OFF_V + KV_DIM
OFF_GB = OFF_GA + D_MODEL
IN_DIM = OFF_GB + D_MODEL

LANES = 128
SUBLANES = 8
MXU_COLS = 256
TILE_ROWS = 256
F_CHUNK = 256
FFN_TILES = 4
FFN_SLOTS = 2
XP_SLOTS = 3
MIX_TILES = 4
VMEM_LIMIT = 56 * 1024 * 1024

F32 = jnp.float32
BF16 = jnp.bfloat16


def _gelu_tanh(x):
    c = math.sqrt(2.0 / math.pi)
    return x * (0.5 * (1.0 + jnp.tanh(c * (x + 0.044715 * (x * x * x)))))


def _sigmoid(x):
    return 0.5 * (1.0 + jnp.tanh(0.5 * x))


def _dot(a, b):
    return jnp.dot(a, b, preferred_element_type=F32)


def _band_buckets():
    i = np.arange(BLOCK)[:, None]
    j = np.arange(2 * BLOCK)[None, :]
    d = np.clip(i + BLOCK - j, 0, None)
    max_exact = N_BUCKETS // 2
    large = max_exact + (np.log(np.maximum(d, 1) / max_exact) / np.log(MAX_DISTANCE / max_exact)
                         * (N_BUCKETS - max_exact)).astype(np.int32)
    large = np.minimum(large, N_BUCKETS - 1)
    return np.where(d < max_exact, d, large).astype(np.int32)


def _bias_table_kernel(rb_ref, bucket_ref, o_ref):
    h = pl.program_id(0)
    bucket = bucket_ref[...]
    acc = jnp.zeros(bucket.shape, F32)
    for b in range(N_BUCKETS):
        acc = jnp.where(bucket == b, rb_ref[b, h], acc)
    o_ref[...] = acc


def _bias_table(rel_bias):
    buckets = jnp.asarray(_band_buckets())
    return pl.pallas_call(
        _bias_table_kernel,
        grid=(N_HEADS,),
        in_specs=[
            pl.BlockSpec(memory_space=pltpu.SMEM),
            pl.BlockSpec((BLOCK, 2 * BLOCK), lambda h: (0, 0)),
        ],
        out_specs=pl.BlockSpec((None, BLOCK, 2 * BLOCK), lambda h: (h, 0, 0)),
        out_shape=jax.ShapeDtypeStruct((N_HEADS, BLOCK, 2 * BLOCK), F32),
        name="rel_bias_table",
    )(rel_bias, buckets)


def _mixer_kernel(sinks_ref, x_ref, gmix_ref, win_ref, gsgu_ref, ws_ref, bfull_ref, bias_ref,
                  wpa_ref, wpb_ref, wout_ref, o_ref,
                  h_ref, u_ref, vn_ref, qlo_ref, qhi_ref, kd_ref, vd_ref, ya_ref, yb_ref, m_ref):
    tm = TILE_ROWS
    n_tiles = x_ref.shape[0] // tm
    blocks_per_tile = tm // BLOCK
    s_idx = pl.program_id(1)
    heads_per_kv = N_HEADS // N_KV_HEADS
    n_out = D_MODEL // MXU_COLS

    @pl.when(s_idx == 0)
    def _():
        kd_ref[:, 0:BLOCK, :] = jnp.zeros((N_KV_HEADS, BLOCK, LANES), BF16)
        vd_ref[:, 0:BLOCK, :] = jnp.zeros((N_KV_HEADS, BLOCK, LANES), BF16)

    row_c = lax.broadcasted_iota(jnp.int32, (CHUNK, CHUNK), 0)
    col_c = lax.broadcasted_iota(jnp.int32, (CHUNK, CHUNK), 1)
    tril = col_c <= row_c
    wmask = [jnp.where(tril, ws_ref[g], 0.0).astype(BF16) for g in range(A_GROUPS)]
    row_i = lax.broadcasted_iota(jnp.int32, (BLOCK, 2 * BLOCK), 0)
    col_j = lax.broadcasted_iota(jnp.int32, (BLOCK, 2 * BLOCK), 1)
    band = (col_j > row_i) & (col_j <= row_i + BLOCK)
    band_first = band & ((col_j >= BLOCK) | (s_idx > 0))
    b_is_lo = lax.broadcasted_iota(jnp.int32, (BLOCK, LANES), 1) < HEAD_DIM
    q_is_lo = (lax.broadcasted_iota(jnp.int32, (tm, Q_DIM), 1) % LANES) < HEAD_DIM
    t_is_lo = lax.broadcasted_iota(jnp.int32, (tm, LANES), 1) < HEAD_DIM

    def norm(t):
        x = x_ref[t * tm:(t + 1) * tm, :]
        r = lax.rsqrt(jnp.mean(x * x, axis=-1, keepdims=True) + EPS)
        h_ref[t] = (x * r * gmix_ref[...]).astype(BF16)

    def proj_u(t):
        u_ref[t] = _gelu_tanh(_dot(h_ref[t], win_ref[:, OFF_PU:OFF_PU + A_WIDTH]))

    def proj_v(t):
        pv = _gelu_tanh(_dot(h_ref[t], win_ref[:, OFF_PV:OFF_PV + A_WIDTH]))
        rv = lax.rsqrt(jnp.mean(pv * pv, axis=-1, keepdims=True) + EPS)
        vn_ref[t] = (pv * rv * gsgu_ref[...]).astype(BF16)

    def proj_q(t):
        q = _dot(h_ref[t], win_ref[:, OFF_Q:OFF_Q + Q_DIM]) * (HEAD_DIM ** -0.5)
        qlo_ref[t] = jnp.where(q_is_lo, q, 0.0).astype(BF16)
        qhi_ref[t] = jnp.where(q_is_lo, 0.0, q).astype(BF16)

    def proj_kv(t):
        kvp = _dot(h_ref[t], win_ref[:, OFF_K:OFF_K + 2 * KV_DIM])
        k = kvp[:, :KV_DIM]
        v = kvp[:, KV_DIM:]
        k_sw = pltpu.roll(k, HEAD_DIM, axis=1)
        v_sw = pltpu.roll(v, HEAD_DIM, axis=1)
        rows = slice(BLOCK + t * tm, BLOCK + (t + 1) * tm)
        kd_ref[0, rows, :] = jnp.where(t_is_lo, k, k_sw).astype(BF16)
        kd_ref[1, rows, :] = jnp.where(t_is_lo, k_sw, k).astype(BF16)
        vd_ref[0, rows, :] = jnp.where(t_is_lo, v, v_sw).astype(BF16)
        vd_ref[1, rows, :] = jnp.where(t_is_lo, v_sw, v).astype(BF16)

    def sgu(t, b):
        rows = slice(b * BLOCK, (b + 1) * BLOCK)
        for g in range(A_GROUPS):
            cols = slice(g * CHUNK, (g + 1) * CHUNK)
            s = _dot(wmask[g], vn_ref[t, rows, cols]) + bfull_ref[:, cols]
            ya_ref[t, rows, cols] = (u_ref[t, rows, cols] * s).astype(BF16)

    def attn(t, b, kv):
        rows = slice(b * BLOCK, (b + 1) * BLOCK)
        bb = t * blocks_per_tile + b
        mask = band_first if bb == 0 else band
        q_parts = []
        for g in range(heads_per_kv):
            hh = kv * heads_per_kv + g
            src = qlo_ref if hh % 2 == 0 else qhi_ref
            c = hh // 2
            q_parts.append(src[t, rows, c * LANES:(c + 1) * LANES])
        q_stack = jnp.concatenate(q_parts, axis=0)
        keys = kd_ref[kv, bb * BLOCK:(bb + 2) * BLOCK, :]
        sc = lax.dot_general(q_stack, keys, (((1,), (1,)), ((), ())),
                             preferred_element_type=F32)
        p_parts = []
        inv_l = []
        for g in range(heads_per_kv):
            hh = kv * heads_per_kv + g
            sg = sc[g * BLOCK:(g + 1) * BLOCK] + bias_ref[hh]
            sg = jnp.where(mask, sg, NEG_INF)
            sink = sinks_ref[hh]
            m = jnp.maximum(jnp.max(sg, axis=-1, keepdims=True), sink)
            p = jnp.exp(sg - m)
            l = jnp.sum(p, axis=-1, keepdims=True) + jnp.exp(sink - m)
            p_parts.append(p.astype(BF16))
            inv_l.append(1.0 / l)
        p_stack = jnp.concatenate(p_parts, axis=0)
        vals = vd_ref[kv, bb * BLOCK:(bb + 2) * BLOCK, :]
        o = _dot(p_stack, vals)
        for cc in range(heads_per_kv // 2):
            o_even = o[(2 * cc) * BLOCK:(2 * cc + 1) * BLOCK] * inv_l[2 * cc]
            o_odd = o[(2 * cc + 1) * BLOCK:(2 * cc + 2) * BLOCK] * inv_l[2 * cc + 1]
            col = kv * (heads_per_kv // 2) + cc
            yb_ref[t, rows, col * LANES:(col + 1) * LANES] = jnp.where(b_is_lo, o_even, o_odd).astype(BF16)

    def merge(t, c):
        cs = slice(c * MXU_COLS, (c + 1) * MXU_COLS)
        ga = _dot(h_ref[t], win_ref[:, OFF_GA + c * MXU_COLS:OFF_GA + (c + 1) * MXU_COLS])
        gb = _dot(h_ref[t], win_ref[:, OFF_GB + c * MXU_COLS:OFF_GB + (c + 1) * MXU_COLS])
        pa = _dot(ya_ref[t], wpa_ref[:, cs])
        pb = _dot(yb_ref[t], wpb_ref[:, cs])
        m_ref[t, :, cs] = (_sigmoid(ga) * pa + _sigmoid(gb) * pb).astype(BF16)

    def out_proj(t, c):
        cs = slice(c * MXU_COLS, (c + 1) * MXU_COLS)
        o_ref[t * tm:(t + 1) * tm, cs] = x_ref[t * tm:(t + 1) * tm, cs] + _dot(m_ref[t], wout_ref[:, cs])

    P = functools.partial

    def stage1(t):
        return [P(proj_u, t), P(proj_v, t), P(proj_q, t), P(proj_kv, t)]

    def stage2(t):
        items = []
        for b in range(blocks_per_tile):
            items += [P(sgu, t, b)] + [P(attn, t, b, kv) for kv in range(N_KV_HEADS)]
        return items

    def stage3(t):
        return [P(merge, t, c) for c in range(n_out)]

    def stage4(t):
        return [P(out_proj, t, c) for c in range(n_out)]

    def interleave(heavy, light):
        n, m = len(heavy), len(light)
        done = 0
        for i, item in enumerate(heavy):
            item()
            upto = (i + 1) * m // n
            for light_item in light[done:upto]:
                light_item()
            done = upto

    def run(items):
        for item in items:
            item()

    norm(0)
    run(stage1(0))
    for t in range(n_tiles):
        heavy = []
        if t >= 2:
            heavy += stage4(t - 2)
        if t >= 1:
            heavy += stage3(t - 1)
        if t + 1 < n_tiles:
            norm(t + 1)
            heavy += stage1(t + 1)
        interleave(heavy, stage2(t)) if heavy else run(stage2(t))
    if n_tiles >= 2:
        run(stage4(n_tiles - 2))
    run(stage3(n_tiles - 1))
    run(stage4(n_tiles - 1))

    rows_all = n_tiles * tm
    kd_ref[:, 0:BLOCK, :] = kd_ref[:, rows_all:rows_all + BLOCK, :]
    vd_ref[:, 0:BLOCK, :] = vd_ref[:, rows_all:rows_all + BLOCK, :]


def _resident(shape):
    nd = len(shape)
    return pl.BlockSpec(shape, lambda *_: (0,) * nd, pipeline_mode=pl.Buffered(1))


def _token_mixer(x2d, batch, seq, sinks, g_mix, w_in, g_sgu, w_s, b_full, bias_tbl, w_pa, w_pb, w_out):
    tm = TILE_ROWS
    rows = MIX_TILES * tm
    ns = seq // rows
    row_spec = pl.BlockSpec((rows, D_MODEL), lambda b, s: (b * ns + s, 0))
    return pl.pallas_call(
        _mixer_kernel,
        grid=(batch, ns),
        in_specs=[
            pl.BlockSpec(memory_space=pltpu.SMEM),
            row_spec,
            _resident((1, D_MODEL)),
            _resident((D_MODEL, IN_DIM)),
            _resident((1, A_WIDTH)),
            _resident((A_GROUPS, CHUNK, CHUNK)),
            _resident((CHUNK, A_WIDTH)),
            _resident((N_HEADS, BLOCK, 2 * BLOCK)),
            _resident((A_WIDTH, D_MODEL)),
            _resident((Q_DIM, D_MODEL)),
            _resident((D_MODEL, D_MODEL)),
        ],
        out_specs=row_spec,
        out_shape=jax.ShapeDtypeStruct(x2d.shape, F32),
        scratch_shapes=[
            pltpu.VMEM((MIX_TILES, tm, D_MODEL), BF16),
            pltpu.VMEM((MIX_TILES, tm, A_WIDTH), F32),
            pltpu.VMEM((MIX_TILES, tm, A_WIDTH), BF16),
            pltpu.VMEM((MIX_TILES, tm, Q_DIM), BF16),
            pltpu.VMEM((MIX_TILES, tm, Q_DIM), BF16),
            pltpu.VMEM((N_KV_HEADS, BLOCK + rows, LANES), BF16),
            pltpu.VMEM((N_KV_HEADS, BLOCK + rows, LANES), BF16),
            pltpu.VMEM((MIX_TILES, tm, A_WIDTH), BF16),
            pltpu.VMEM((MIX_TILES, tm, Q_DIM), BF16),
            pltpu.VMEM((MIX_TILES, tm, D_MODEL), BF16),
        ],
        compiler_params=pltpu.CompilerParams(
            dimension_semantics=("arbitrary", "arbitrary"),
            vmem_limit_bytes=VMEM_LIMIT),
        name="token_mixer",
    )(sinks, x2d, g_mix, w_in, g_sgu, w_s, b_full, bias_tbl, w_pa, w_pb, w_out)


def _ffn_kernel(x_ref, gffn_ref, wup_ref, wconv_ref, bconv_ref, wdown_ref, gfin_ref, o_ref,
                skew_in_ref, skew_out_ref, xp_ref, h_ref, carry_ref, act_ref, *, steps_per_seq):
    tm = TILE_ROWS
    n_tiles = x_ref.shape[0] // tm
    n_grp = tm // SUBLANES
    pitch = n_grp + SUBLANES
    n_lane_tiles = D_MODEL // LANES
    n_chunks = D_FF // F_CHUNK
    n_down = D_MODEL // MXU_COLS

    @pl.when(pl.program_id(0) % steps_per_seq == 0)
    def _():
        carry_ref[...] = jnp.zeros(carry_ref.shape, F32)

    def slot(t):
        return t % FFN_SLOTS

    def prologue(t):
        for lt in range(n_lane_tiles):
            for k in range(SUBLANES):
                skew_in_ref[slot(t), lt, pitch * k:pitch * k + n_grp, :] = (
                    x_ref[t * tm + n_grp * k:t * tm + n_grp * (k + 1), lt * LANES:(lt + 1) * LANES])
        x = jnp.concatenate(
            [jnp.concatenate([skew_in_ref[slot(t), lt, pl.ds(j, SUBLANES, stride=pitch), :]
                              for lt in range(n_lane_tiles)], axis=1)
             for j in range(n_grp)], axis=0)
        xp_ref[t % XP_SLOTS] = x
        r = lax.rsqrt(jnp.mean(x * x, axis=-1, keepdims=True) + EPS)
        h_ref[slot(t)] = (x * r * gffn_ref[...]).astype(BF16)

    first_sublane = lax.broadcasted_iota(jnp.int32, (SUBLANES, F_CHUNK), 0) == 0

    def conv(t, cols):
        up = _dot(h_ref[slot(t)], wup_ref[:, cols])
        last = up[tm - SUBLANES:tm]
        last2 = up[tm - 2 * SUBLANES:tm - SUBLANES]

        def wrap(cur, prev):
            return jnp.where(first_sublane, pltpu.roll(prev, 1, axis=0), pltpu.roll(cur, 1, axis=0))

        m1 = wrap(last, carry_ref[1, :, cols])
        m2 = wrap(last2, carry_ref[0, :, cols])
        carry_ref[0, :, cols] = last2
        carry_ref[1, :, cols] = last
        up1 = jnp.concatenate([m1, up[:tm - SUBLANES]], axis=0)
        up2 = jnp.concatenate([m2, m1, up[:tm - 2 * SUBLANES]], axis=0)
        return (bconv_ref[:, cols] + wconv_ref[2:3, cols] * up
                + wconv_ref[1:2, cols] * up1 + wconv_ref[0:1, cols] * up2)

    def up_chunk(t, c):
        gcols = slice(c * F_CHUNK, (c + 1) * F_CHUNK)
        gate = conv(t, gcols)
        val = conv(t, slice(D_FF + c * F_CHUNK, D_FF + (c + 1) * F_CHUNK))
        hg = 0.5 * gate
        act_ref[slot(t), :, gcols] = ((hg + hg * jnp.tanh(hg)) * val).astype(BF16)

    def down_block(t, j):
        cs = slice(j * MXU_COLS, (j + 1) * MXU_COLS)
        return xp_ref[t % XP_SLOTS, :, cs] + _dot(act_ref[slot(t)], wdown_ref[:, cs])

    def epilogue(t, ys):
        ssq = sum(jnp.sum(y * y, axis=-1, keepdims=True) for y in ys)
        rf = lax.rsqrt(ssq * (1.0 / D_MODEL) + EPS)
        for jb, y in enumerate(ys):
            cs = slice(jb * MXU_COLS, (jb + 1) * MXU_COLS)
            out = y * rf * gfin_ref[:, cs]
            for ll in range(MXU_COLS // LANES):
                lt = jb * (MXU_COLS // LANES) + ll
                for j in range(n_grp):
                    skew_out_ref[slot(t), lt, pl.ds(j, SUBLANES, stride=pitch), :] = (
                        out[SUBLANES * j:SUBLANES * (j + 1), ll * LANES:(ll + 1) * LANES])
        for lt in range(n_lane_tiles):
            for k in range(SUBLANES):
                o_ref[t * tm + n_grp * k:t * tm + n_grp * (k + 1), lt * LANES:(lt + 1) * LANES] = (
                    skew_out_ref[slot(t), lt, pitch * k:pitch * k + n_grp, :])

    down_at = {(k + 1) * n_chunks // (n_down + 1): k for k in range(n_down)}
    assert len(down_at) == n_down
    prologue(0)
    for t in range(n_tiles):
        if t + 1 < n_tiles:
            prologue(t + 1)
        ys = {}
        for c in range(n_chunks):
            up_chunk(t, c)
            if t >= 1 and c in down_at:
                ys[down_at[c]] = down_block(t - 1, down_at[c])
        if t >= 1:
            epilogue(t - 1, [ys[j] for j in range(n_down)])
    last = n_tiles - 1
    epilogue(last, [down_block(last, j) for j in range(n_down)])


def _conv_ffn(x2d, batch, seq, g_ffn, w_up, w_conv, b_conv, w_down, g_final):
    rows = FFN_TILES * TILE_ROWS
    steps_per_seq = seq // rows
    row_spec = pl.BlockSpec((rows, D_MODEL), lambda i: (i, 0))
    skew_rows = TILE_ROWS + SUBLANES * SUBLANES
    return pl.pallas_call(
        functools.partial(_ffn_kernel, steps_per_seq=steps_per_seq),
        grid=(batch * steps_per_seq,),
        in_specs=[
            row_spec,
            _resident((1, D_MODEL)),
            _resident((D_MODEL, 2 * D_FF)),
            _resident((CONV_WIDTH, 2 * D_FF)),
            _resident((1, 2 * D_FF)),
            _resident((D_FF, D_MODEL)),
            _resident((1, D_MODEL)),
        ],
        out_specs=row_spec,
        out_shape=jax.ShapeDtypeStruct(x2d.shape, F32),
        scratch_shapes=[
            pltpu.VMEM((FFN_SLOTS, D_MODEL // LANES, skew_rows, LANES), F32),
            pltpu.VMEM((FFN_SLOTS, D_MODEL // LANES, skew_rows, LANES), F32),
            pltpu.VMEM((XP_SLOTS, TILE_ROWS, D_MODEL), F32),
            pltpu.VMEM((FFN_SLOTS, TILE_ROWS, D_MODEL), BF16),
            pltpu.VMEM((CONV_WIDTH - 1, SUBLANES, 2 * D_FF), F32),
            pltpu.VMEM((FFN_SLOTS, TILE_ROWS, D_FF), BF16),
        ],
        compiler_params=pltpu.CompilerParams(
            dimension_semantics=("arbitrary",),
            vmem_limit_bytes=VMEM_LIMIT),
        name="conv_ffn",
    )(x2d, g_ffn, w_up, w_conv, b_conv, w_down, g_final)


def kernel(x, g_mix, w_in, g_sgu, w_s, b_s, sinks, rel_bias, w_pa, w_pb, w_out,
           g_ffn, w_up, w_conv, b_conv, w_down, g_final):
    batch, seq, d = x.shape
    assert w_in.shape[0] == 1 and d == D_MODEL and seq % (max(FFN_TILES, MIX_TILES) * TILE_ROWS) == 0 and w_in.shape[2] == IN_DIM
    bias_tbl = _bias_table(rel_bias)
    x2d = x.reshape(batch * seq, d)
    b_full = jnp.repeat(jnp.transpose(b_s[0]), A_WIDTH // A_GROUPS, axis=1)
    x2d = _token_mixer(
        x2d, batch, seq, sinks[0], g_mix[0][None, :], w_in[0].astype(BF16), g_sgu[0][None, :],
        w_s[0], b_full, bias_tbl, w_pa[0].astype(BF16), w_pb[0].astype(BF16), w_out[0].astype(BF16))
    x2d = _conv_ffn(
        x2d, batch, seq, g_ffn[0][None, :], w_up[0].astype(BF16), w_conv[0], b_conv[0][None, :],
        w_down[0].astype(BF16), g_final[None, :])
    return x2d.reshape(batch, seq, d)
```

```python
import functools
import math

import jax
import jax.numpy as jnp
import numpy as np
from jax import lax
from jax.experimental import pallas as pl
from jax.experimental.pallas import tpu as pltpu

D_MODEL = 1024
CHUNK = 128
A_GROUPS = 4
A_WIDTH = 512
N_HEADS = 8
N_KV_HEADS = 2
HEAD_DIM = 64
Q_DIM = N_HEADS * HEAD_DIM
KV_DIM = N_KV_HEADS * HEAD_DIM
WINDOW = 128
BLOCK = 128
N_BUCKETS = 32
MAX_DISTANCE = 128
D_FF = 2816
CONV_WIDTH = 3
EPS = 1e-6
NEG_INF = -1e30

OFF_PU = 0
OFF_PV = OFF_PU + A_WIDTH
OFF_Q = OFF_PV + A_WIDTH
OFF_K = OFF_Q + Q_DIM
OFF_V = OFF_K + KV_DIM
OFF_GA = OFF_V + KV_DIM
OFF_GB = OFF_GA + D_MODEL
IN_DIM = OFF_GB + D_MODEL

LANES = 128
SUBLANES = 8
MXU_COLS = 256
TILE_ROWS = 512
F_CHUNK = 256
FFN_TILES = 2
FFN_SLOTS = 2
XP_SLOTS = min(3, FFN_TILES)
MIX_TILES = 2
VMEM_LIMIT = 56 * 1024 * 1024

F32 = jnp.float32
BF16 = jnp.bfloat16


def _gelu_tanh(x):
    c = math.sqrt(2.0 / math.pi)
    return x * (0.5 * (1.0 + jnp.tanh(c * (x + 0.044715 * (x * x * x)))))


def _sigmoid(x):
    return 0.5 * (1.0 + jnp.tanh(0.5 * x))


def _dot(a, b):
    return jnp.dot(a, b, preferred_element_type=F32)


def _band_buckets():
    i = np.arange(BLOCK)[:, None]
    j = np.arange(2 * BLOCK)[None, :]
    d = np.clip(i + BLOCK - j, 0, None)
    max_exact = N_BUCKETS // 2
    large = max_exact + (np.log(np.maximum(d, 1) / max_exact) / np.log(MAX_DISTANCE / max_exact)
                         * (N_BUCKETS - max_exact)).astype(np.int32)
    large = np.minimum(large, N_BUCKETS - 1)
    return np.where(d < max_exact, d, large).astype(np.int32)


def _bias_table_kernel(rb_ref, bucket_ref, o_ref):
    h = pl.program_id(0)
    bucket = bucket_ref[...]
    acc = jnp.zeros(bucket.shape, F32)
    for b in range(N_BUCKETS):
        acc = jnp.where(bucket == b, rb_ref[b, h], acc)
    o_ref[...] = acc


def _bias_table(rel_bias):
    buckets = jnp.asarray(_band_buckets())
    return pl.pallas_call(
        _bias_table_kernel,
        grid=(N_HEADS,),
        in_specs=[
            pl.BlockSpec(memory_space=pltpu.SMEM),
            pl.BlockSpec((BLOCK, 2 * BLOCK), lambda h: (0, 0)),
        ],
        out_specs=pl.BlockSpec((None, BLOCK, 2 * BLOCK), lambda h: (h, 0, 0)),
        out_shape=jax.ShapeDtypeStruct((N_HEADS, BLOCK, 2 * BLOCK), F32),
        name="rel_bias_table",
    )(rel_bias, buckets)


def _mixer_kernel(sinks_ref, x_ref, gmix_ref, win_ref, gsgu_ref, ws_ref, bfull_ref, bias_ref,
                  wpa_ref, wpb_ref, wout_ref, o_ref,
                  h_ref, u_ref, vn_ref, qlo_ref, qhi_ref, kd_ref, vd_ref, ya_ref, yb_ref, m_ref):
    tm = TILE_ROWS
    n_tiles = x_ref.shape[0] // tm
    blocks_per_tile = tm // BLOCK
    s_idx = pl.program_id(1)
    heads_per_kv = N_HEADS // N_KV_HEADS
    n_out = D_MODEL // MXU_COLS

    @pl.when(s_idx == 0)
    def _():
        kd_ref[:, 0:BLOCK, :] = jnp.zeros((N_KV_HEADS, BLOCK, LANES), BF16)
        vd_ref[:, 0:BLOCK, :] = jnp.zeros((N_KV_HEADS, BLOCK, LANES), BF16)

    row_c = lax.broadcasted_iota(jnp.int32, (CHUNK, CHUNK), 0)
    col_c = lax.broadcasted_iota(jnp.int32, (CHUNK, CHUNK), 1)
    tril = col_c <= row_c
    wmask = [jnp.where(tril, ws_ref[g], 0.0).astype(BF16) for g in range(A_GROUPS)]
    row_i = lax.broadcasted_iota(jnp.int32, (BLOCK, 2 * BLOCK), 0)
    col_j = lax.broadcasted_iota(jnp.int32, (BLOCK, 2 * BLOCK), 1)
    band = (col_j > row_i) & (col_j <= row_i + BLOCK)
    band_first = band & ((col_j >= BLOCK) | (s_idx > 0))
    b_is_lo = lax.broadcasted_iota(jnp.int32, (BLOCK, LANES), 1) < HEAD_DIM
    q_is_lo = (lax.broadcasted_iota(jnp.int32, (tm, Q_DIM), 1) % LANES) < HEAD_DIM
    t_is_lo = lax.broadcasted_iota(jnp.int32, (tm, LANES), 1) < HEAD_DIM

    def norm(t):
        x = x_ref[t * tm:(t + 1) * tm, :]
        r = lax.rsqrt(jnp.mean(x * x, axis=-1, keepdims=True) + EPS)
        h_ref[t] = (x * r * gmix_ref[...]).astype(BF16)

    def proj_u(t):
        u_ref[t] = _gelu_tanh(_dot(h_ref[t], win_ref[:, OFF_PU:OFF_PU + A_WIDTH]))

    def proj_v(t):
        pv = _gelu_tanh(_dot(h_ref[t], win_ref[:, OFF_PV:OFF_PV + A_WIDTH]))
        rv = lax.rsqrt(jnp.mean(pv * pv, axis=-1, keepdims=True) + EPS)
        vn_ref[t] = (pv * rv * gsgu_ref[...]).astype(BF16)

    def proj_q(t):
        q = _dot(h_ref[t], win_ref[:, OFF_Q:OFF_Q + Q_DIM]) * (HEAD_DIM ** -0.5)
        qlo_ref[t] = jnp.where(q_is_lo, q, 0.0).astype(BF16)
        qhi_ref[t] = jnp.where(q_is_lo, 0.0, q).astype(BF16)

    def proj_kv(t):
        kvp = _dot(h_ref[t], win_ref[:, OFF_K:OFF_K + 2 * KV_DIM])
        k = kvp[:, :KV_DIM]
        v = kvp[:, KV_DIM:]
        k_sw = pltpu.roll(k, HEAD_DIM, axis=1)
        v_sw = pltpu.roll(v, HEAD_DIM, axis=1)
        rows = slice(BLOCK + t * tm, BLOCK + (t + 1) * tm)
        kd_ref[0, rows, :] = jnp.where(t_is_lo, k, k_sw).astype(BF16)
        kd_ref[1, rows, :] = jnp.where(t_is_lo, k_sw, k).astype(BF16)
        vd_ref[0, rows, :] = jnp.where(t_is_lo, v, v_sw).astype(BF16)
        vd_ref[1, rows, :] = jnp.where(t_is_lo, v_sw, v).astype(BF16)

    def sgu(t, b):
        rows = slice(b * BLOCK, (b + 1) * BLOCK)
        for g in range(A_GROUPS):
            cols = slice(g * CHUNK, (g + 1) * CHUNK)
            s = _dot(wmask[g], vn_ref[t, rows, cols]) + bfull_ref[:, cols]
            ya_ref[t, rows, cols] = (u_ref[t, rows, cols] * s).astype(BF16)

    def attn(t, b, kv):
        rows = slice(b * BLOCK, (b + 1) * BLOCK)
        bb = t * blocks_per_tile + b
        mask = band_first if bb == 0 else band
        q_parts = []
        for g in range(heads_per_kv):
            hh = kv * heads_per_kv + g
            src = qlo_ref if hh % 2 == 0 else qhi_ref
            c = hh // 2
            q_parts.append(src[t, rows, c * LANES:(c + 1) * LANES])
        q_stack = jnp.concatenate(q_parts, axis=0)
        keys = kd_ref[kv, bb * BLOCK:(bb + 2) * BLOCK, :]
        sc = lax.dot_general(q_stack, keys, (((1,), (1,)), ((), ())),
                             preferred_element_type=F32)
        p_parts = []
        inv_l = []
        for g in range(heads_per_kv):
            hh = kv * heads_per_kv + g
            sg = sc[g * BLOCK:(g + 1) * BLOCK] + bias_ref[hh]
            sg = jnp.where(mask, sg, NEG_INF)
            sink = sinks_ref[hh]
            m = jnp.maximum(jnp.max(sg, axis=-1, keepdims=True), sink)
            p = jnp.exp(sg - m)
            l = jnp.sum(p, axis=-1, keepdims=True) + jnp.exp(sink - m)
            p_parts.append(p.astype(BF16))
            inv_l.append(1.0 / l)
        p_stack = jnp.concatenate(p_parts, axis=0)
        vals = vd_ref[kv, bb * BLOCK:(bb + 2) * BLOCK, :]
        o = _dot(p_stack, vals)
        for cc in range(heads_per_kv // 2):
            o_even = o[(2 * cc) * BLOCK:(2 * cc + 1) * BLOCK] * inv_l[2 * cc]
            o_odd = o[(2 * cc + 1) * BLOCK:(2 * cc + 2) * BLOCK] * inv_l[2 * cc + 1]
            col = kv * (heads_per_kv // 2) + cc
            yb_ref[t, rows, col * LANES:(col + 1) * LANES] = jnp.where(b_is_lo, o_even, o_odd).astype(BF16)

    def merge(t, c):
        cs = slice(c * MXU_COLS, (c + 1) * MXU_COLS)
        ga = _dot(h_ref[t], win_ref[:, OFF_GA + c * MXU_COLS:OFF_GA + (c + 1) * MXU_COLS])
        gb = _dot(h_ref[t], win_ref[:, OFF_GB + c * MXU_COLS:OFF_GB + (c + 1) * MXU_COLS])
        pa = _dot(ya_ref[t], wpa_ref[:, cs])
        pb = _dot(yb_ref[t], wpb_ref[:, cs])
        m_ref[t, :, cs] = (_sigmoid(ga) * pa + _sigmoid(gb) * pb).astype(BF16)

    def out_proj(t, c):
        cs = slice(c * MXU_COLS, (c + 1) * MXU_COLS)
        o_ref[t * tm:(t + 1) * tm, cs] = x_ref[t * tm:(t + 1) * tm, cs] + _dot(m_ref[t], wout_ref[:, cs])

    P = functools.partial

    def stage1(t):
        return [P(proj_u, t), P(proj_v, t), P(proj_q, t), P(proj_kv, t)]

    def stage2(t):
        items = []
        for b in range(blocks_per_tile):
            items += [P(sgu, t, b)] + [P(attn, t, b, kv) for kv in range(N_KV_HEADS)]
        return items

    def stage3(t):
        return [P(merge, t, c) for c in range(n_out)]

    def stage4(t):
        return [P(out_proj, t, c) for c in range(n_out)]

    def interleave(heavy, light):
        n, m = len(heavy), len(light)
        done = 0
        for i, item in enumerate(heavy):
            item()
            upto = (i + 1) * m // n
            for light_item in light[done:upto]:
                light_item()
            done = upto

    def run(items):
        for item in items:
            item()

    norm(0)
    run(stage1(0))
    for t in range(n_tiles):
        heavy = []
        if t >= 2:
            heavy += stage4(t - 2)
        if t >= 1:
            heavy += stage3(t - 1)
        if t + 1 < n_tiles:
            norm(t + 1)
            heavy += stage1(t + 1)
        interleave(heavy, stage2(t)) if heavy else run(stage2(t))
    if n_tiles >= 2:
        run(stage4(n_tiles - 2))
    run(stage3(n_tiles - 1))
    run(stage4(n_tiles - 1))

    rows_all = n_tiles * tm
    kd_ref[:, 0:BLOCK, :] = kd_ref[:, rows_all:rows_all + BLOCK, :]
    vd_ref[:, 0:BLOCK, :] = vd_ref[:, rows_all:rows_all + BLOCK, :]


def _resident(shape):
    nd = len(shape)
    return pl.BlockSpec(shape, lambda *_: (0,) * nd, pipeline_mode=pl.Buffered(1))


def _token_mixer(x2d, batch, seq, sinks, g_mix, w_in, g_sgu, w_s, b_full, bias_tbl, w_pa, w_pb, w_out):
    tm = TILE_ROWS
    rows = MIX_TILES * tm
    ns = seq // rows
    row_spec = pl.BlockSpec((rows, D_MODEL), lambda b, s: (b * ns + s, 0))
    return pl.pallas_call(
        _mixer_kernel,
        grid=(batch, ns),
        in_specs=[
            pl.BlockSpec(memory_space=pltpu.SMEM),
            row_spec,
            _resident((1, D_MODEL)),
            _resident((D_MODEL, IN_DIM)),
            _resident((1, A_WIDTH)),
            _resident((A_GROUPS, CHUNK, CHUNK)),
            _resident((CHUNK, A_WIDTH)),
            _resident((N_HEADS, BLOCK, 2 * BLOCK)),
            _resident((A_WIDTH, D_MODEL)),
            _resident((Q_DIM, D_MODEL)),
            _resident((D_MODEL, D_MODEL)),
        ],
        out_specs=row_spec,
        out_shape=jax.ShapeDtypeStruct(x2d.shape, F32),
        scratch_shapes=[
            pltpu.VMEM((MIX_TILES, tm, D_MODEL), BF16),
            pltpu.VMEM((MIX_TILES, tm, A_WIDTH), F32),
            pltpu.VMEM((MIX_TILES, tm, A_WIDTH), BF16),
            pltpu.VMEM((MIX_TILES, tm, Q_DIM), BF16),
            pltpu.VMEM((MIX_TILES, tm, Q_DIM), BF16),
            pltpu.VMEM((N_KV_HEADS, BLOCK + rows, LANES), BF16),
            pltpu.VMEM((N_KV_HEADS, BLOCK + rows, LANES), BF16),
            pltpu.VMEM((MIX_TILES, tm, A_WIDTH), BF16),
            pltpu.VMEM((MIX_TILES, tm, Q_DIM), BF16),
            pltpu.VMEM((MIX_TILES, tm, D_MODEL), BF16),
        ],
        compiler_params=pltpu.CompilerParams(
            dimension_semantics=("arbitrary", "arbitrary"),
            vmem_limit_bytes=VMEM_LIMIT),
        name="token_mixer",
    )(sinks, x2d, g_mix, w_in, g_sgu, w_s, b_full, bias_tbl, w_pa, w_pb, w_out)


def _ffn_kernel(x_ref, gffn_ref, wup_ref, wconv_ref, bconv_ref, wdown_ref, gfin_ref, o_ref,
                skew_ref, xp_ref, h_ref, carry_ref, act_ref, *, steps_per_seq):
    tm = TILE_ROWS
    n_tiles = x_ref.shape[0] // tm
    n_grp = tm // SUBLANES
    pitch = n_grp + SUBLANES
    n_lane_tiles = D_MODEL // LANES
    n_chunks = D_FF // F_CHUNK
    n_down = D_MODEL // MXU_COLS

    @pl.when(pl.program_id(0) % steps_per_seq == 0)
    def _():
        carry_ref[...] = jnp.zeros(carry_ref.shape, F32)

    def slot(t):
        return t % FFN_SLOTS

    def prologue(t):
        for lt in range(n_lane_tiles):
            for k in range(SUBLANES):
                skew_ref[slot(t), lt, pitch * k:pitch * k + n_grp, :] = (
                    x_ref[t * tm + n_grp * k:t * tm + n_grp * (k + 1), lt * LANES:(lt + 1) * LANES])
        x = jnp.concatenate(
            [jnp.concatenate([skew_ref[slot(t), lt, pl.ds(j, SUBLANES, stride=pitch), :]
                              for lt in range(n_lane_tiles)], axis=1)
             for j in range(n_grp)], axis=0)
        xp_ref[t % XP_SLOTS] = x
        r = lax.rsqrt(jnp.mean(x * x, axis=-1, keepdims=True) + EPS)
        h_ref[slot(t)] = (x * r * gffn_ref[...]).astype(BF16)

    first_sublane = lax.broadcasted_iota(jnp.int32, (SUBLANES, F_CHUNK), 0) == 0

    def conv(t, cols):
        up = _dot(h_ref[slot(t)], wup_ref[:, cols])
        last = up[tm - SUBLANES:tm]
        last2 = up[tm - 2 * SUBLANES:tm - SUBLANES]

        def wrap(cur, prev):
            return jnp.where(first_sublane, pltpu.roll(prev, 1, axis=0), pltpu.roll(cur, 1, axis=0))

        m1 = wrap(last, carry_ref[1, :, cols])
        m2 = wrap(last2, carry_ref[0, :, cols])
        carry_ref[0, :, cols] = last2
        carry_ref[1, :, cols] = last
        up1 = jnp.concatenate([m1, up[:tm - SUBLANES]], axis=0)
        up2 = jnp.concatenate([m2, m1, up[:tm - 2 * SUBLANES]], axis=0)
        return (bconv_ref[:, cols] + wconv_ref[2:3, cols] * up
                + wconv_ref[1:2, cols] * up1 + wconv_ref[0:1, cols] * up2)

    def up_chunk(t, c):
        gcols = slice(c * F_CHUNK, (c + 1) * F_CHUNK)
        gate = conv(t, gcols)
        val = conv(t, slice(D_FF + c * F_CHUNK, D_FF + (c + 1) * F_CHUNK))
        hg = 0.5 * gate
        act_ref[slot(t), :, gcols] = ((hg + hg * jnp.tanh(hg)) * val).astype(BF16)

    def down_block(t, j):
        cs = slice(j * MXU_COLS, (j + 1) * MXU_COLS)
        return xp_ref[t % XP_SLOTS, :, cs] + _dot(act_ref[slot(t)], wdown_ref[:, cs])

    def epilogue(t, ys):
        ssq = sum(jnp.sum(y * y, axis=-1, keepdims=True) for y in ys)
        rf = lax.rsqrt(ssq * (1.0 / D_MODEL) + EPS)
        for jb, y in enumerate(ys):
            cs = slice(jb * MXU_COLS, (jb + 1) * MXU_COLS)
            out = y * rf * gfin_ref[:, cs]
            for ll in range(MXU_COLS // LANES):
                lt = jb * (MXU_COLS // LANES) + ll
                for j in range(n_grp):
                    skew_ref[slot(t), lt, pl.ds(j, SUBLANES, stride=pitch), :] = (
                        out[SUBLANES * j:SUBLANES * (j + 1), ll * LANES:(ll + 1) * LANES])
        for lt in range(n_lane_tiles):
            for k in range(SUBLANES):
                o_ref[t * tm + n_grp * k:t * tm + n_grp * (k + 1), lt * LANES:(lt + 1) * LANES] = (
                    skew_ref[slot(t), lt, pitch * k:pitch * k + n_grp, :])

    down_at = {(k + 1) * n_chunks // (n_down + 1): k for k in range(n_down)}
    assert len(down_at) == n_down
    prologue(0)
    for t in range(n_tiles):
        if t + 1 < n_tiles:
            prologue(t + 1)
        ys = {}
        for c in range(n_chunks):
            up_chunk(t, c)
            if t >= 1 and c in down_at:
                ys[down_at[c]] = down_block(t - 1, down_at[c])
        if t >= 1:
            epilogue(t - 1, [ys[j] for j in range(n_down)])
    last = n_tiles - 1
    epilogue(last, [down_block(last, j) for j in range(n_down)])


def _conv_ffn(x2d, batch, seq, g_ffn, w_up, w_conv, b_conv, w_down, g_final):
    rows = FFN_TILES * TILE_ROWS
    steps_per_seq = seq // rows
    row_spec = pl.BlockSpec((rows, D_MODEL), lambda i: (i, 0))
    skew_rows = TILE_ROWS + SUBLANES * SUBLANES
    return pl.pallas_call(
        functools.partial(_ffn_kernel, steps_per_seq=steps_per_seq),
        grid=(batch * steps_per_seq,),
        in_specs=[
            row_spec,
            _resident((1, D_MODEL)),
            _resident((D_MODEL, 2 * D_FF)),
            _resident((CONV_WIDTH, 2 * D_FF)),
            _resident((1, 2 * D_FF)),
            _resident((D_FF, D_MODEL)),
            _resident((1, D_MODEL)),
        ],
        out_specs=row_spec,
        out_shape=jax.ShapeDtypeStruct(x2d.shape, F32),
        scratch_shapes=[
            pltpu.VMEM((FFN_SLOTS, D_MODEL // LANES, skew_rows, LANES), F32),
            pltpu.VMEM((XP_SLOTS, TILE_ROWS, D_MODEL), F32),
            pltpu.VMEM((FFN_SLOTS, TILE_ROWS, D_MODEL), BF16),
            pltpu.VMEM((CONV_WIDTH - 1, SUBLANES, 2 * D_FF), F32),
            pltpu.VMEM((FFN_SLOTS, TILE_ROWS, D_FF), BF16),
        ],
        compiler_params=pltpu.CompilerParams(
            dimension_semantics=("arbitrary",),
            vmem_limit_bytes=VMEM_LIMIT),
        name="conv_ffn",
    )(x2d, g_ffn, w_up, w_conv, b_conv, w_down, g_final)


def kernel(x, g_mix, w_in, g_sgu, w_s, b_s, sinks, rel_bias, w_pa, w_pb, w_out,
           g_ffn, w_up, w_conv, b_conv, w_down, g_final):
    batch, seq, d = x.shape
    assert w_in.shape[0] == 1 and d == D_MODEL and seq % (max(FFN_TILES, MIX_TILES) * TILE_ROWS) == 0 and w_in.shape[2] == IN_DIM
    bias_tbl = _bias_table(rel_bias)
    x2d = x.reshape(batch * seq, d)
    b_full = jnp.repeat(jnp.transpose(b_s[0]), A_WIDTH // A_GROUPS, axis=1)
    x2d = _token_mixer(
        x2d, batch, seq, sinks[0], g_mix[0][None, :], w_in[0].astype(BF16), g_sgu[0][None, :],
        w_s[0], b_full, bias_tbl, w_pa[0].astype(BF16), w_pb[0].astype(BF16), w_out[0].astype(BF16))
    x2d = _conv_ffn(
        x2d, batch, seq, g_ffn[0][None, :], w_up[0].astype(BF16), w_conv[0], b_conv[0][None, :],
        w_down[0].astype(BF16), g_final[None, :])
    return x2d.reshape(batch, seq, d)
```

```python
import functools
import math

import jax
import jax.numpy as jnp
import numpy as np
from jax import lax
from jax.experimental import pallas as pl
from jax.experimental.pallas import tpu as pltpu

D_MODEL = 1024
CHUNK = 128
A_GROUPS = 4
A_WIDTH = 512
N_HEADS = 8
N_KV_HEADS = 2
HEAD_DIM = 64
Q_DIM = N_HEADS * HEAD_DIM
KV_DIM = N_KV_HEADS * HEAD_DIM
WINDOW = 128
BLOCK = 128
N_BUCKETS = 32
MAX_DISTANCE = 128
D_FF = 2816
CONV_WIDTH = 3
EPS = 1e-6
NEG_INF = -1e30

OFF_PU = 0
OFF_PV = OFF_PU + A_WIDTH
OFF_Q = OFF_PV + A_WIDTH
OFF_K = OFF_Q + Q_DIM
OFF_V = OFF_K + KV_DIM
OFF_GA = OFF_V + KV_DIM
OFF_GB = OFF_GA + D_MODEL
IN_DIM = OFF_GB + D_MODEL

LANES = 128
SUBLANES = 8
MXU_COLS = 256
TILE_ROWS = 512
F_CHUNK = 256
FFN_TILES = 2
FFN_SLOTS = 2
XP_SLOTS = min(3, FFN_TILES)
MIX_TILES = 2
VMEM_LIMIT = 56 * 1024 * 1024

F32 = jnp.float32
BF16 = jnp.bfloat16


def _gelu_tanh(x):
    c = math.sqrt(2.0 / math.pi)
    return x * (0.5 * (1.0 + jnp.tanh(c * (x + 0.044715 * (x * x * x)))))


def _sigmoid(x):
    return 0.5 * (1.0 + jnp.tanh(0.5 * x))


def _dot(a, b):
    return jnp.dot(a, b, preferred_element_type=F32)


def _band_buckets():
    i = np.arange(BLOCK)[:, None]
    j = np.arange(2 * BLOCK)[None, :]
    d = np.clip(i + BLOCK - j, 0, None)
    max_exact = N_BUCKETS // 2
    large = max_exact + (np.log(np.maximum(d, 1) / max_exact) / np.log(MAX_DISTANCE / max_exact)
                         * (N_BUCKETS - max_exact)).astype(np.int32)
    large = np.minimum(large, N_BUCKETS - 1)
    return np.where(d < max_exact, d, large).astype(np.int32)


def _bias_table_kernel(rb_ref, bucket_ref, o_ref):
    h = pl.program_id(0)
    bucket = bucket_ref[...]
    acc = jnp.zeros(bucket.shape, F32)
    for b in range(N_BUCKETS):
        acc = jnp.where(bucket == b, rb_ref[b, h], acc)
    o_ref[...] = acc


def _bias_table(rel_bias):
    buckets = jnp.asarray(_band_buckets())
    return pl.pallas_call(
        _bias_table_kernel,
        grid=(N_HEADS,),
        in_specs=[
            pl.BlockSpec(memory_space=pltpu.SMEM),
            pl.BlockSpec((BLOCK, 2 * BLOCK), lambda h: (0, 0)),
        ],
        out_specs=pl.BlockSpec((None, BLOCK, 2 * BLOCK), lambda h: (h, 0, 0)),
        out_shape=jax.ShapeDtypeStruct((N_HEADS, BLOCK, 2 * BLOCK), F32),
        name="rel_bias_table",
    )(rel_bias, buckets)


def _mixer_kernel(sinks_ref, x_ref, gmix_ref, win_ref, gsgu_ref, ws_ref, bfull_ref, bias_ref,
                  wpa_ref, wpb_ref, wout_ref, wup_f32_ref, wdown_f32_ref, o_ref, wup_bf16_ref, wdown_bf16_ref,
                  h_ref, u_ref, vn_ref, qlo_ref, qhi_ref, kd_ref, vd_ref, ya_ref, yb_ref, m_ref):
    wup_bf16_ref[...] = wup_f32_ref[...].astype(BF16)
    wdown_bf16_ref[...] = wdown_f32_ref[...].astype(BF16)

    tm = TILE_ROWS
    n_tiles = x_ref.shape[0] // tm
    blocks_per_tile = tm // BLOCK
    s_idx = pl.program_id(1)
    heads_per_kv = N_HEADS // N_KV_HEADS
    n_out = D_MODEL // MXU_COLS

    @pl.when(s_idx == 0)
    def _():
        kd_ref[:, 0:BLOCK, :] = jnp.zeros((N_KV_HEADS, BLOCK, LANES), BF16)
        vd_ref[:, 0:BLOCK, :] = jnp.zeros((N_KV_HEADS, BLOCK, LANES), BF16)

    row_c = lax.broadcasted_iota(jnp.int32, (CHUNK, CHUNK), 0)
    col_c = lax.broadcasted_iota(jnp.int32, (CHUNK, CHUNK), 1)
    tril = col_c <= row_c
    wmask = [jnp.where(tril, ws_ref[g], 0.0).astype(BF16) for g in range(A_GROUPS)]
    row_i = lax.broadcasted_iota(jnp.int32, (BLOCK, 2 * BLOCK), 0)
    col_j = lax.broadcasted_iota(jnp.int32, (BLOCK, 2 * BLOCK), 1)
    band = (col_j > row_i) & (col_j <= row_i + BLOCK)
    band_first = band & ((col_j >= BLOCK) | (s_idx > 0))
    b_is_lo = lax.broadcasted_iota(jnp.int32, (BLOCK, LANES), 1) < HEAD_DIM
    q_is_lo = (lax.broadcasted_iota(jnp.int32, (tm, Q_DIM), 1) % LANES) < HEAD_DIM
    t_is_lo = lax.broadcasted_iota(jnp.int32, (tm, LANES), 1) < HEAD_DIM

    def norm(t):
        x = x_ref[t * tm:(t + 1) * tm, :]
        r = lax.rsqrt(jnp.mean(x * x, axis=-1, keepdims=True) + EPS)
        h_ref[t] = (x * r * gmix_ref[...]).astype(BF16)

    def proj_u(t):
        u_ref[t] = _gelu_tanh(_dot(h_ref[t], win_ref[:, OFF_PU:OFF_PU + A_WIDTH]))

    def proj_v(t):
        pv = _gelu_tanh(_dot(h_ref[t], win_ref[:, OFF_PV:OFF_PV + A_WIDTH]))
        rv = lax.rsqrt(jnp.mean(pv * pv, axis=-1, keepdims=True) + EPS)
        vn_ref[t] = (pv * rv * gsgu_ref[...]).astype(BF16)

    def proj_q(t):
        q = _dot(h_ref[t], win_ref[:, OFF_Q:OFF_Q + Q_DIM]) * (HEAD_DIM ** -0.5)
        qlo_ref[t] = jnp.where(q_is_lo, q, 0.0).astype(BF16)
        qhi_ref[t] = jnp.where(q_is_lo, 0.0, q).astype(BF16)

    def proj_kv(t):
        kvp = _dot(h_ref[t], win_ref[:, OFF_K:OFF_K + 2 * KV_DIM])
        k = kvp[:, :KV_DIM]
        v = kvp[:, KV_DIM:]
        k_sw = pltpu.roll(k, HEAD_DIM, axis=1)
        v_sw = pltpu.roll(v, HEAD_DIM, axis=1)
        rows = slice(BLOCK + t * tm, BLOCK + (t + 1) * tm)
        kd_ref[0, rows, :] = jnp.where(t_is_lo, k, k_sw).astype(BF16)
        kd_ref[1, rows, :] = jnp.where(t_is_lo, k_sw, k).astype(BF16)
        vd_ref[0, rows, :] = jnp.where(t_is_lo, v, v_sw).astype(BF16)
        vd_ref[1, rows, :] = jnp.where(t_is_lo, v_sw, v).astype(BF16)

    def sgu(t, g):
        cols = slice(g * CHUNK, (g + 1) * CHUNK)
        v_chunks = [vn_ref[t, b * CHUNK:(b + 1) * CHUNK, cols] for b in range(tm // CHUNK)]
        s_all = _dot(wmask[g], jnp.concatenate(v_chunks, axis=1))
        for b in range(tm // CHUNK):
            rows = slice(b * CHUNK, (b + 1) * CHUNK)
            s = s_all[:, b * CHUNK:(b + 1) * CHUNK] + bfull_ref[:, cols]
            ya_ref[t, rows, cols] = (u_ref[t, rows, cols] * s).astype(BF16)

    def attn(t, b, kv):
        rows = slice(b * BLOCK, (b + 1) * BLOCK)
        bb = t * blocks_per_tile + b
        mask = band_first if bb == 0 else band
        q_parts = []
        for g in range(heads_per_kv):
            hh = kv * heads_per_kv + g
            src = qlo_ref if hh % 2 == 0 else qhi_ref
            c = hh // 2
            q_parts.append(src[t, rows, c * LANES:(c + 1) * LANES])
        q_stack = jnp.concatenate(q_parts, axis=0)
        keys = kd_ref[kv, bb * BLOCK:(bb + 2) * BLOCK, :]
        sc = lax.dot_general(q_stack, keys, (((1,), (1,)), ((), ())),
                             preferred_element_type=F32)
        p_parts = []
        inv_l = []
        for g in range(heads_per_kv):
            hh = kv * heads_per_kv + g
            sg = sc[g * BLOCK:(g + 1) * BLOCK] + bias_ref[hh]
            sg = jnp.where(mask, sg, NEG_INF)
            sink = sinks_ref[hh]
            m = jnp.maximum(jnp.max(sg, axis=-1, keepdims=True), sink)
            p = jnp.exp(sg - m)
            l = jnp.sum(p, axis=-1, keepdims=True) + jnp.exp(sink - m)
            p_parts.append(p.astype(BF16))
            inv_l.append(1.0 / l)
        p_stack = jnp.concatenate(p_parts, axis=0)
        vals = vd_ref[kv, bb * BLOCK:(bb + 2) * BLOCK, :]
        o = _dot(p_stack, vals)
        for cc in range(heads_per_kv // 2):
            o_even = o[(2 * cc) * BLOCK:(2 * cc + 1) * BLOCK] * inv_l[2 * cc]
            o_odd = o[(2 * cc + 1) * BLOCK:(2 * cc + 2) * BLOCK] * inv_l[2 * cc + 1]
            col = kv * (heads_per_kv // 2) + cc
            yb_ref[t, rows, col * LANES:(col + 1) * LANES] = jnp.where(b_is_lo, o_even, o_odd).astype(BF16)

    def merge(t, c):
        cs = slice(c * MXU_COLS, (c + 1) * MXU_COLS)
        ga = _dot(h_ref[t], win_ref[:, OFF_GA + c * MXU_COLS:OFF_GA + (c + 1) * MXU_COLS])
        gb = _dot(h_ref[t], win_ref[:, OFF_GB + c * MXU_COLS:OFF_GB + (c + 1) * MXU_COLS])
        pa = _dot(ya_ref[t], wpa_ref[:, cs])
        pb = _dot(yb_ref[t], wpb_ref[:, cs])
        m_ref[t, :, cs] = (_sigmoid(ga) * pa + _sigmoid(gb) * pb).astype(BF16)

    def out_proj(t, c):
        cs = slice(c * MXU_COLS, (c + 1) * MXU_COLS)
        o_ref[t * tm:(t + 1) * tm, cs] = x_ref[t * tm:(t + 1) * tm, cs] + _dot(m_ref[t], wout_ref[:, cs])

    P = functools.partial

    def stage1(t):
        return [P(proj_u, t), P(proj_v, t), P(proj_q, t), P(proj_kv, t)]

    def stage2(t):
        sgu_items = [P(sgu, t, g) for g in range(A_GROUPS)]
        attn_items = [P(attn, t, b, kv) for b in range(blocks_per_tile) for kv in range(N_KV_HEADS)]
        items = []
        for i, item in enumerate(attn_items):
            items += sgu_items[i * A_GROUPS // len(attn_items):(i + 1) * A_GROUPS // len(attn_items)] + [item]
        return items

    def stage3(t):
        return [P(merge, t, c) for c in range(n_out)]

    def stage4(t):
        return [P(out_proj, t, c) for c in range(n_out)]

    def interleave(heavy, light):
        n, m = len(heavy), len(light)
        done = 0
        for i, item in enumerate(heavy):
            item()
            upto = (i + 1) * m // n
            for light_item in light[done:upto]:
                light_item()
            done = upto

    def run(items):
        for item in items:
            item()

    norm(0)
    run(stage1(0))
    for t in range(n_tiles):
        heavy = []
        if t >= 2:
            heavy += stage4(t - 2)
        if t >= 1:
            heavy += stage3(t - 1)
        if t + 1 < n_tiles:
            norm(t + 1)
            heavy += stage1(t + 1)
        interleave(heavy, stage2(t)) if heavy else run(stage2(t))
    if n_tiles >= 2:
        run(stage4(n_tiles - 2))
    run(stage3(n_tiles - 1))
    run(stage4(n_tiles - 1))

    rows_all = n_tiles * tm
    kd_ref[:, 0:BLOCK, :] = kd_ref[:, rows_all:rows_all + BLOCK, :]
    vd_ref[:, 0:BLOCK, :] = vd_ref[:, rows_all:rows_all + BLOCK, :]


def _resident(shape):
    nd = len(shape)
    return pl.BlockSpec(shape, lambda *_: (0,) * nd, pipeline_mode=pl.Buffered(1))


def _token_mixer(x2d, batch, seq, sinks, g_mix, w_in, g_sgu, w_s, b_full, bias_tbl, w_pa, w_pb, w_out,
                 w_up_f32, w_down_f32):
    tm = TILE_ROWS
    rows = MIX_TILES * tm
    ns = seq // rows
    n_steps = batch * ns
    row_spec = pl.BlockSpec((rows, D_MODEL), lambda b, s: (b * ns + s, 0))
    up_slab = D_MODEL // n_steps
    down_slab = 16 * 11
    n_down_slabs = D_FF // down_slab
    assert D_MODEL % n_steps == 0 and up_slab % 16 == 0 and D_FF % down_slab == 0 and n_down_slabs <= n_steps
    up_spec = pl.BlockSpec((up_slab, 2 * D_FF), lambda b, s: (b * ns + s, 0))
    down_spec = pl.BlockSpec((down_slab, D_MODEL), lambda b, s: (jnp.minimum(b * ns + s, n_down_slabs - 1), 0))
    return pl.pallas_call(
        _mixer_kernel,
        grid=(batch, ns),
        in_specs=[
            pl.BlockSpec(memory_space=pltpu.SMEM),
            row_spec,
            _resident((1, D_MODEL)),
            _resident((D_MODEL, IN_DIM)),
            _resident((1, A_WIDTH)),
            _resident((A_GROUPS, CHUNK, CHUNK)),
            _resident((CHUNK, A_WIDTH)),
            _resident((N_HEADS, BLOCK, 2 * BLOCK)),
            _resident((A_WIDTH, D_MODEL)),
            _resident((Q_DIM, D_MODEL)),
            _resident((D_MODEL, D_MODEL)),
            up_spec,
            down_spec,
        ],
        out_specs=[row_spec, up_spec, down_spec],
        out_shape=[jax.ShapeDtypeStruct(x2d.shape, F32),
                   jax.ShapeDtypeStruct(w_up_f32.shape, BF16),
                   jax.ShapeDtypeStruct(w_down_f32.shape, BF16)],
        scratch_shapes=[
            pltpu.VMEM((MIX_TILES, tm, D_MODEL), BF16),
            pltpu.VMEM((MIX_TILES, tm, A_WIDTH), F32),
            pltpu.VMEM((MIX_TILES, tm, A_WIDTH), BF16),
            pltpu.VMEM((MIX_TILES, tm, Q_DIM), BF16),
            pltpu.VMEM((MIX_TILES, tm, Q_DIM), BF16),
            pltpu.VMEM((N_KV_HEADS, BLOCK + rows, LANES), BF16),
            pltpu.VMEM((N_KV_HEADS, BLOCK + rows, LANES), BF16),
            pltpu.VMEM((MIX_TILES, tm, A_WIDTH), BF16),
            pltpu.VMEM((MIX_TILES, tm, Q_DIM), BF16),
            pltpu.VMEM((MIX_TILES, tm, D_MODEL), BF16),
        ],
        compiler_params=pltpu.CompilerParams(
            dimension_semantics=("arbitrary", "arbitrary"),
            vmem_limit_bytes=VMEM_LIMIT),
        name="token_mixer",
    )(sinks, x2d, g_mix, w_in, g_sgu, w_s, b_full, bias_tbl, w_pa, w_pb, w_out, w_up_f32, w_down_f32)


def _ffn_kernel(x_ref, gffn_ref, wup_ref, wconv_ref, bconv_ref, wdown_ref, gfin_ref, o_ref,
                skew_ref, xp_ref, h_ref, carry_ref, act_ref, *, steps_per_seq):
    tm = TILE_ROWS
    n_tiles = x_ref.shape[0] // tm
    n_grp = tm // SUBLANES
    pitch = n_grp + SUBLANES
    n_lane_tiles = D_MODEL // LANES
    n_chunks = D_FF // F_CHUNK
    n_down = D_MODEL // MXU_COLS

    @pl.when(pl.program_id(0) % steps_per_seq == 0)
    def _():
        carry_ref[...] = jnp.zeros(carry_ref.shape, F32)

    def slot(t):
        return t % FFN_SLOTS

    def prologue(t):
        for lt in range(n_lane_tiles):
            for k in range(SUBLANES):
                skew_ref[slot(t), lt, pitch * k:pitch * k + n_grp, :] = (
                    x_ref[t * tm + n_grp * k:t * tm + n_grp * (k + 1), lt * LANES:(lt + 1) * LANES])
        x = jnp.concatenate(
            [jnp.concatenate([skew_ref[slot(t), lt, pl.ds(j, SUBLANES, stride=pitch), :]
                              for lt in range(n_lane_tiles)], axis=1)
             for j in range(n_grp)], axis=0)
        xp_ref[t % XP_SLOTS] = x
        r = lax.rsqrt(jnp.mean(x * x, axis=-1, keepdims=True) + EPS)
        h_ref[slot(t)] = (x * r * gffn_ref[...]).astype(BF16)

    first_sublane = lax.broadcasted_iota(jnp.int32, (SUBLANES, F_CHUNK), 0) == 0

    def conv(t, cols):
        up = _dot(h_ref[slot(t)], wup_ref[:, cols])
        last = up[tm - SUBLANES:tm]
        last2 = up[tm - 2 * SUBLANES:tm - SUBLANES]

        def wrap(cur, prev):
            return jnp.where(first_sublane, pltpu.roll(prev, 1, axis=0), pltpu.roll(cur, 1, axis=0))

        m1 = wrap(last, carry_ref[1, :, cols])
        m2 = wrap(last2, carry_ref[0, :, cols])
        carry_ref[0, :, cols] = last2
        carry_ref[1, :, cols] = last
        up1 = jnp.concatenate([m1, up[:tm - SUBLANES]], axis=0)
        up2 = jnp.concatenate([m2, m1, up[:tm - 2 * SUBLANES]], axis=0)
        return (bconv_ref[:, cols] + wconv_ref[2:3, cols] * up
                + wconv_ref[1:2, cols] * up1 + wconv_ref[0:1, cols] * up2)

    def up_chunk(t, c):
        gcols = slice(c * F_CHUNK, (c + 1) * F_CHUNK)
        gate = conv(t, gcols)
        val = conv(t, slice(D_FF + c * F_CHUNK, D_FF + (c + 1) * F_CHUNK))
        hg = 0.5 * gate
        act_ref[slot(t), :, gcols] = ((hg + hg * jnp.tanh(hg)) * val).astype(BF16)

    def down_block(t, j):
        cs = slice(j * MXU_COLS, (j + 1) * MXU_COLS)
        return xp_ref[t % XP_SLOTS, :, cs] + _dot(act_ref[slot(t)], wdown_ref[:, cs])

    def epilogue(t, ys):
        ssq = sum(jnp.sum(y * y, axis=-1, keepdims=True) for y in ys)
        rf = lax.rsqrt(ssq * (1.0 / D_MODEL) + EPS)
        for jb, y in enumerate(ys):
            cs = slice(jb * MXU_COLS, (jb + 1) * MXU_COLS)
            out = y * rf * gfin_ref[:, cs]
            for ll in range(MXU_COLS // LANES):
                lt = jb * (MXU_COLS // LANES) + ll
                for j in range(n_grp):
                    skew_ref[slot(t), lt, pl.ds(j, SUBLANES, stride=pitch), :] = (
                        out[SUBLANES * j:SUBLANES * (j + 1), ll * LANES:(ll + 1) * LANES])
        for lt in range(n_lane_tiles):
            for k in range(SUBLANES):
                o_ref[t * tm + n_grp * k:t * tm + n_grp * (k + 1), lt * LANES:(lt + 1) * LANES] = (
                    skew_ref[slot(t), lt, pitch * k:pitch * k + n_grp, :])

    down_at = {(k + 1) * n_chunks // (n_down + 1): k for k in range(n_down)}
    assert len(down_at) == n_down
    prologue(0)
    for t in range(n_tiles):
        if t + 1 < n_tiles:
            prologue(t + 1)
        ys = {}
        for c in range(n_chunks):
            up_chunk(t, c)
            if t >= 1 and c in down_at:
                ys[down_at[c]] = down_block(t - 1, down_at[c])
        if t >= 1:
            epilogue(t - 1, [ys[j] for j in range(n_down)])
    last = n_tiles - 1
    epilogue(last, [down_block(last, j) for j in range(n_down)])


def _conv_ffn(x2d, batch, seq, g_ffn, w_up, w_conv, b_conv, w_down, g_final):
    rows = FFN_TILES * TILE_ROWS
    steps_per_seq = seq // rows
    row_spec = pl.BlockSpec((rows, D_MODEL), lambda i: (i, 0))
    skew_rows = TILE_ROWS + SUBLANES * SUBLANES
    return pl.pallas_call(
        functools.partial(_ffn_kernel, steps_per_seq=steps_per_seq),
        grid=(batch * steps_per_seq,),
        in_specs=[
            row_spec,
            _resident((1, D_MODEL)),
            _resident((D_MODEL, 2 * D_FF)),
            _resident((CONV_WIDTH, 2 * D_FF)),
            _resident((1, 2 * D_FF)),
            _resident((D_FF, D_MODEL)),
            _resident((1, D_MODEL)),
        ],
        out_specs=row_spec,
        out_shape=jax.ShapeDtypeStruct(x2d.shape, F32),
        scratch_shapes=[
            pltpu.VMEM((FFN_SLOTS, D_MODEL // LANES, skew_rows, LANES), F32),
            pltpu.VMEM((XP_SLOTS, TILE_ROWS, D_MODEL), F32),
            pltpu.VMEM((FFN_SLOTS, TILE_ROWS, D_MODEL), BF16),
            pltpu.VMEM((CONV_WIDTH - 1, SUBLANES, 2 * D_FF), F32),
            pltpu.VMEM((FFN_SLOTS, TILE_ROWS, D_FF), BF16),
        ],
        compiler_params=pltpu.CompilerParams(
            dimension_semantics=("arbitrary",),
            vmem_limit_bytes=VMEM_LIMIT),
        name="conv_ffn",
    )(x2d, g_ffn, w_up, w_conv, b_conv, w_down, g_final)


def kernel(x, g_mix, w_in, g_sgu, w_s, b_s, sinks, rel_bias, w_pa, w_pb, w_out,
           g_ffn, w_up, w_conv, b_conv, w_down, g_final):
    batch, seq, d = x.shape
    assert w_in.shape[0] == 1 and d == D_MODEL and seq % (max(FFN_TILES, MIX_TILES) * TILE_ROWS) == 0 and w_in.shape[2] == IN_DIM
    bias_tbl = _bias_table(rel_bias)
    x2d = x.reshape(batch * seq, d)
    b_full = jnp.repeat(jnp.transpose(b_s[0]), A_WIDTH // A_GROUPS, axis=1)
    x2d, w_up_bf16, w_down_bf16 = _token_mixer(
        x2d, batch, seq, sinks[0], g_mix[0][None, :], w_in[0].astype(BF16), g_sgu[0][None, :],
        w_s[0], b_full, bias_tbl, w_pa[0].astype(BF16), w_pb[0].astype(BF16), w_out[0].astype(BF16),
        w_up[0], w_down[0])
    x2d = _conv_ffn(
        x2d, batch, seq, g_ffn[0][None, :], w_up_bf16, w_conv[0], b_conv[0][None, :],
        w_down_bf16, g_final[None, :])
    return x2d.reshape(batch, seq, d)
```

```python
import functools
import math

import jax
import jax.numpy as jnp
import numpy as np
from jax import lax
from jax.experimental import pallas as pl
from jax.experimental.pallas import tpu as pltpu

D_MODEL = 1024
CHUNK = 128
A_GROUPS = 4
A_WIDTH = 512
N_HEADS = 8
N_KV_HEADS = 2
HEAD_DIM = 64
Q_DIM = N_HEADS * HEAD_DIM
KV_DIM = N_KV_HEADS * HEAD_DIM
WINDOW = 128
BLOCK = 128
N_BUCKETS = 32
MAX_DISTANCE = 128
D_FF = 2816
CONV_WIDTH = 3
EPS = 1e-6
NEG_INF = -1e30

OFF_PU = 0
OFF_PV = OFF_PU + A_WIDTH
OFF_Q = OFF_PV + A_WIDTH
OFF_K = OFF_Q + Q_DIM
OFF_V = OFF_K + KV_DIM
OFF_GA = OFF_V + KV_DIM
OFF_GB = OFF_GA + D_MODEL
IN_DIM = OFF_GB + D_MODEL

LANES = 128
SUBLANES = 8
MXU_COLS = 256
TILE_ROWS = 512
F_CHUNK = 256
FFN_TILES = 2
FFN_SLOTS = 2
MIX_TILES = 2
VMEM_LIMIT = 56 * 1024 * 1024

F32 = jnp.float32
BF16 = jnp.bfloat16


def _gelu_tanh(x):
    c = math.sqrt(2.0 / math.pi)
    return x * (0.5 * (1.0 + jnp.tanh(c * (x + 0.044715 * (x * x * x)))))


def _sigmoid(x):
    return 0.5 * (1.0 + jnp.tanh(0.5 * x))


def _dot(a, b):
    return jnp.dot(a, b, preferred_element_type=F32)


def _band_buckets():
    i = np.arange(BLOCK)[:, None]
    j = np.arange(2 * BLOCK)[None, :]
    d = np.clip(i + BLOCK - j, 0, None)
    max_exact = N_BUCKETS // 2
    large = max_exact + (np.log(np.maximum(d, 1) / max_exact) / np.log(MAX_DISTANCE / max_exact)
                         * (N_BUCKETS - max_exact)).astype(np.int32)
    large = np.minimum(large, N_BUCKETS - 1)
    return np.where(d < max_exact, d, large).astype(np.int32)


def _bias_table_kernel(rb_ref, bucket_ref, o_ref):
    h = pl.program_id(0)
    bucket = bucket_ref[...]
    acc = jnp.zeros(bucket.shape, F32)
    for b in range(N_BUCKETS):
        acc = jnp.where(bucket == b, rb_ref[b, h], acc)
    o_ref[...] = acc


def _bias_table(rel_bias):
    buckets = jnp.asarray(_band_buckets())
    return pl.pallas_call(
        _bias_table_kernel,
        grid=(N_HEADS,),
        in_specs=[
            pl.BlockSpec(memory_space=pltpu.SMEM),
            pl.BlockSpec((BLOCK, 2 * BLOCK), lambda h: (0, 0)),
        ],
        out_specs=pl.BlockSpec((None, BLOCK, 2 * BLOCK), lambda h: (h, 0, 0)),
        out_shape=jax.ShapeDtypeStruct((N_HEADS, BLOCK, 2 * BLOCK), F32),
        name="rel_bias_table",
    )(rel_bias, buckets)


def _mixer_kernel(sinks_ref, x_ref, gmix_ref, win_ref, gsgu_ref, ws_ref, bfull_ref, bias_ref,
                  wpa_ref, wpb_ref, wout_ref, wup_f32_ref, wdown_f32_ref, o_ref, wup_bf16_ref, wdown_bf16_ref,
                  h_ref, u_ref, vn_ref, qlo_ref, qhi_ref, kd_ref, vd_ref, ya_ref, yb_ref, m_ref, perm_ref):
    wup_bf16_ref[...] = wup_f32_ref[...].astype(BF16)
    wdown_bf16_ref[...] = wdown_f32_ref[...].astype(BF16)

    tm = TILE_ROWS
    n_tiles = x_ref.shape[0] // tm
    blocks_per_tile = tm // BLOCK
    s_idx = pl.program_id(1)
    heads_per_kv = N_HEADS // N_KV_HEADS
    n_out = D_MODEL // MXU_COLS

    @pl.when(s_idx == 0)
    def _():
        kd_ref[:, 0:BLOCK, :] = jnp.zeros((N_KV_HEADS, BLOCK, LANES), BF16)
        vd_ref[:, 0:BLOCK, :] = jnp.zeros((N_KV_HEADS, BLOCK, LANES), BF16)

    row_c = lax.broadcasted_iota(jnp.int32, (CHUNK, CHUNK), 0)
    col_c = lax.broadcasted_iota(jnp.int32, (CHUNK, CHUNK), 1)
    tril = col_c <= row_c
    wmask = [jnp.where(tril, ws_ref[g], 0.0).astype(BF16) for g in range(A_GROUPS)]
    row_i = lax.broadcasted_iota(jnp.int32, (BLOCK, 2 * BLOCK), 0)
    col_j = lax.broadcasted_iota(jnp.int32, (BLOCK, 2 * BLOCK), 1)
    band = (col_j > row_i) & (col_j <= row_i + BLOCK)
    band_first = band & ((col_j >= BLOCK) | (s_idx > 0))
    b_is_lo = lax.broadcasted_iota(jnp.int32, (BLOCK, LANES), 1) < HEAD_DIM
    q_is_lo = (lax.broadcasted_iota(jnp.int32, (tm, Q_DIM), 1) % LANES) < HEAD_DIM
    t_is_lo = lax.broadcasted_iota(jnp.int32, (tm, LANES), 1) < HEAD_DIM

    def norm(t):
        x = x_ref[t * tm:(t + 1) * tm, :]
        r = lax.rsqrt(jnp.mean(x * x, axis=-1, keepdims=True) + EPS)
        h_ref[t] = (x * r * gmix_ref[...]).astype(BF16)

    def proj_u(t):
        u_ref[t] = _gelu_tanh(_dot(h_ref[t], win_ref[:, OFF_PU:OFF_PU + A_WIDTH]))

    def proj_v(t):
        pv = _gelu_tanh(_dot(h_ref[t], win_ref[:, OFF_PV:OFF_PV + A_WIDTH]))
        rv = lax.rsqrt(jnp.mean(pv * pv, axis=-1, keepdims=True) + EPS)
        vn_ref[t] = (pv * rv * gsgu_ref[...]).astype(BF16)

    def proj_q(t):
        q = _dot(h_ref[t], win_ref[:, OFF_Q:OFF_Q + Q_DIM]) * (HEAD_DIM ** -0.5)
        qlo_ref[t] = jnp.where(q_is_lo, q, 0.0).astype(BF16)
        qhi_ref[t] = jnp.where(q_is_lo, 0.0, q).astype(BF16)

    def proj_kv(t):
        kvp = _dot(h_ref[t], win_ref[:, OFF_K:OFF_K + 2 * KV_DIM])
        k = kvp[:, :KV_DIM]
        v = kvp[:, KV_DIM:]
        k_sw = pltpu.roll(k, HEAD_DIM, axis=1)
        v_sw = pltpu.roll(v, HEAD_DIM, axis=1)
        rows = slice(BLOCK + t * tm, BLOCK + (t + 1) * tm)
        kd_ref[0, rows, :] = jnp.where(t_is_lo, k, k_sw).astype(BF16)
        kd_ref[1, rows, :] = jnp.where(t_is_lo, k_sw, k).astype(BF16)
        vd_ref[0, rows, :] = jnp.where(t_is_lo, v, v_sw).astype(BF16)
        vd_ref[1, rows, :] = jnp.where(t_is_lo, v_sw, v).astype(BF16)

    def sgu(t, g):
        cols = slice(g * CHUNK, (g + 1) * CHUNK)
        v_chunks = [vn_ref[t, b * CHUNK:(b + 1) * CHUNK, cols] for b in range(tm // CHUNK)]
        s_all = _dot(wmask[g], jnp.concatenate(v_chunks, axis=1))
        for b in range(tm // CHUNK):
            rows = slice(b * CHUNK, (b + 1) * CHUNK)
            s = s_all[:, b * CHUNK:(b + 1) * CHUNK] + bfull_ref[:, cols]
            ya_ref[t, rows, cols] = (u_ref[t, rows, cols] * s).astype(BF16)

    def attn(t, b, kv):
        rows = slice(b * BLOCK, (b + 1) * BLOCK)
        bb = t * blocks_per_tile + b
        mask = band_first if bb == 0 else band
        q_parts = []
        for g in range(heads_per_kv):
            hh = kv * heads_per_kv + g
            src = qlo_ref if hh % 2 == 0 else qhi_ref
            c = hh // 2
            q_parts.append(src[t, rows, c * LANES:(c + 1) * LANES])
        q_stack = jnp.concatenate(q_parts, axis=0)
        keys = kd_ref[kv, bb * BLOCK:(bb + 2) * BLOCK, :]
        sc = lax.dot_general(q_stack, keys, (((1,), (1,)), ((), ())),
                             preferred_element_type=F32)
        p_parts = []
        inv_l = []
        for g in range(heads_per_kv):
            hh = kv * heads_per_kv + g
            sg = sc[g * BLOCK:(g + 1) * BLOCK] + bias_ref[hh]
            sg = jnp.where(mask, sg, NEG_INF)
            sink = sinks_ref[hh]
            m = jnp.maximum(jnp.max(sg, axis=-1, keepdims=True), sink)
            p = jnp.exp(sg - m)
            l = jnp.sum(p, axis=-1, keepdims=True) + jnp.exp(sink - m)
            p_parts.append(p.astype(BF16))
            inv_l.append(1.0 / l)
        p_stack = jnp.concatenate(p_parts, axis=0)
        vals = vd_ref[kv, bb * BLOCK:(bb + 2) * BLOCK, :]
        o = _dot(p_stack, vals)
        for cc in range(heads_per_kv // 2):
            o_even = o[(2 * cc) * BLOCK:(2 * cc + 1) * BLOCK] * inv_l[2 * cc]
            o_odd = o[(2 * cc + 1) * BLOCK:(2 * cc + 2) * BLOCK] * inv_l[2 * cc + 1]
            col = kv * (heads_per_kv // 2) + cc
            yb_ref[t, rows, col * LANES:(col + 1) * LANES] = jnp.where(b_is_lo, o_even, o_odd).astype(BF16)

    def merge(t, c):
        cs = slice(c * MXU_COLS, (c + 1) * MXU_COLS)
        ga = _dot(h_ref[t], win_ref[:, OFF_GA + c * MXU_COLS:OFF_GA + (c + 1) * MXU_COLS])
        gb = _dot(h_ref[t], win_ref[:, OFF_GB + c * MXU_COLS:OFF_GB + (c + 1) * MXU_COLS])
        pa = _dot(ya_ref[t], wpa_ref[:, cs])
        pb = _dot(yb_ref[t], wpb_ref[:, cs])
        m_ref[t, :, cs] = (_sigmoid(ga) * pa + _sigmoid(gb) * pb).astype(BF16)

    def out_proj(t, c):
        cs = slice(c * MXU_COLS, (c + 1) * MXU_COLS)
        val = x_ref[t * tm:(t + 1) * tm, cs] + _dot(m_ref[t], wout_ref[:, cs])
        n_grp = tm // SUBLANES
        for ll in range(MXU_COLS // LANES):
            lt = c * (MXU_COLS // LANES) + ll
            for g in range(n_grp):
                k, j0 = divmod(SUBLANES * g, n_grp)
                perm_ref[lt, pl.ds(SUBLANES * j0 + k, SUBLANES, stride=SUBLANES), :] = (
                    val[SUBLANES * g:SUBLANES * (g + 1), ll * LANES:(ll + 1) * LANES])
            o_ref[t * tm:(t + 1) * tm, lt * LANES:(lt + 1) * LANES] = perm_ref[lt]

    P = functools.partial

    def stage1(t):
        return [P(proj_u, t), P(proj_v, t), P(proj_q, t), P(proj_kv, t)]

    def stage2(t):
        sgu_items = [P(sgu, t, g) for g in range(A_GROUPS)]
        attn_items = [P(attn, t, b, kv) for b in range(blocks_per_tile) for kv in range(N_KV_HEADS)]
        items = []
        for i, item in enumerate(attn_items):
            items += sgu_items[i * A_GROUPS // len(attn_items):(i + 1) * A_GROUPS // len(attn_items)] + [item]
        return items

    def stage3(t):
        return [P(merge, t, c) for c in range(n_out)]

    def stage4(t):
        return [P(out_proj, t, c) for c in range(n_out)]

    def interleave(heavy, light):
        n, m = len(heavy), len(light)
        done = 0
        for i, item in enumerate(heavy):
            item()
            upto = (i + 1) * m // n
            for light_item in light[done:upto]:
                light_item()
            done = upto

    def run(items):
        for item in items:
            item()

    norm(0)
    run(stage1(0))
    for t in range(n_tiles):
        heavy = []
        if t >= 2:
            heavy += stage4(t - 2)
        if t >= 1:
            heavy += stage3(t - 1)
        if t + 1 < n_tiles:
            norm(t + 1)
            heavy += stage1(t + 1)
        interleave(heavy, stage2(t)) if heavy else run(stage2(t))
    if n_tiles >= 2:
        run(stage4(n_tiles - 2))
    run(stage3(n_tiles - 1))
    run(stage4(n_tiles - 1))

    rows_all = n_tiles * tm
    kd_ref[:, 0:BLOCK, :] = kd_ref[:, rows_all:rows_all + BLOCK, :]
    vd_ref[:, 0:BLOCK, :] = vd_ref[:, rows_all:rows_all + BLOCK, :]


def _resident(shape):
    nd = len(shape)
    return pl.BlockSpec(shape, lambda *_: (0,) * nd, pipeline_mode=pl.Buffered(1))


def _token_mixer(x2d, batch, seq, sinks, g_mix, w_in, g_sgu, w_s, b_full, bias_tbl, w_pa, w_pb, w_out,
                 w_up_f32, w_down_f32):
    tm = TILE_ROWS
    rows = MIX_TILES * tm
    ns = seq // rows
    n_steps = batch * ns
    row_spec = pl.BlockSpec((rows, D_MODEL), lambda b, s: (b * ns + s, 0))
    up_slab = D_MODEL // n_steps
    down_slab = 16 * 11
    n_down_slabs = D_FF // down_slab
    assert D_MODEL % n_steps == 0 and up_slab % 16 == 0 and D_FF % down_slab == 0 and n_down_slabs <= n_steps
    up_spec = pl.BlockSpec((up_slab, 2 * D_FF), lambda b, s: (b * ns + s, 0))
    down_spec = pl.BlockSpec((down_slab, D_MODEL), lambda b, s: (jnp.minimum(b * ns + s, n_down_slabs - 1), 0))
    return pl.pallas_call(
        _mixer_kernel,
        grid=(batch, ns),
        in_specs=[
            pl.BlockSpec(memory_space=pltpu.SMEM),
            row_spec,
            _resident((1, D_MODEL)),
            _resident((D_MODEL, IN_DIM)),
            _resident((1, A_WIDTH)),
            _resident((A_GROUPS, CHUNK, CHUNK)),
            _resident((CHUNK, A_WIDTH)),
            _resident((N_HEADS, BLOCK, 2 * BLOCK)),
            _resident((A_WIDTH, D_MODEL)),
            _resident((Q_DIM, D_MODEL)),
            _resident((D_MODEL, D_MODEL)),
            up_spec,
            down_spec,
        ],
        out_specs=[row_spec, up_spec, down_spec],
        out_shape=[jax.ShapeDtypeStruct(x2d.shape, F32),
                   jax.ShapeDtypeStruct(w_up_f32.shape, BF16),
                   jax.ShapeDtypeStruct(w_down_f32.shape, BF16)],
        scratch_shapes=[
            pltpu.VMEM((MIX_TILES, tm, D_MODEL), BF16),
            pltpu.VMEM((MIX_TILES, tm, A_WIDTH), F32),
            pltpu.VMEM((MIX_TILES, tm, A_WIDTH), BF16),
            pltpu.VMEM((MIX_TILES, tm, Q_DIM), BF16),
            pltpu.VMEM((MIX_TILES, tm, Q_DIM), BF16),
            pltpu.VMEM((N_KV_HEADS, BLOCK + rows, LANES), BF16),
            pltpu.VMEM((N_KV_HEADS, BLOCK + rows, LANES), BF16),
            pltpu.VMEM((MIX_TILES, tm, A_WIDTH), BF16),
            pltpu.VMEM((MIX_TILES, tm, Q_DIM), BF16),
            pltpu.VMEM((MIX_TILES, tm, D_MODEL), BF16),
            pltpu.VMEM((D_MODEL // LANES, tm, LANES), F32),
        ],
        compiler_params=pltpu.CompilerParams(
            dimension_semantics=("arbitrary", "arbitrary"),
            vmem_limit_bytes=VMEM_LIMIT),
        name="token_mixer",
    )(sinks, x2d, g_mix, w_in, g_sgu, w_s, b_full, bias_tbl, w_pa, w_pb, w_out, w_up_f32, w_down_f32)


def _ffn_kernel(x_ref, gffn_ref, wup_ref, wconv_ref, bconv_ref, wdown_ref, gfin_ref, o_ref,
                skew_ref, h_ref, carry_ref, act_ref, *, steps_per_seq):
    tm = TILE_ROWS
    n_tiles = x_ref.shape[0] // tm
    n_grp = tm // SUBLANES
    pitch = n_grp + SUBLANES
    n_lane_tiles = D_MODEL // LANES
    n_chunks = D_FF // F_CHUNK
    n_down = D_MODEL // MXU_COLS

    @pl.when(pl.program_id(0) % steps_per_seq == 0)
    def _():
        carry_ref[...] = jnp.zeros(carry_ref.shape, F32)

    def slot(t):
        return t % FFN_SLOTS

    def prologue(t):
        x = x_ref[t * tm:(t + 1) * tm, :]
        r = lax.rsqrt(jnp.mean(x * x, axis=-1, keepdims=True) + EPS)
        h_ref[slot(t)] = (x * r * gffn_ref[...]).astype(BF16)

    first_sublane = lax.broadcasted_iota(jnp.int32, (SUBLANES, F_CHUNK), 0) == 0

    def conv(t, cols):
        up = _dot(h_ref[slot(t)], wup_ref[:, cols])
        last = up[tm - SUBLANES:tm]
        last2 = up[tm - 2 * SUBLANES:tm - SUBLANES]

        def wrap(cur, prev):
            return jnp.where(first_sublane, pltpu.roll(prev, 1, axis=0), pltpu.roll(cur, 1, axis=0))

        m1 = wrap(last, carry_ref[1, :, cols])
        m2 = wrap(last2, carry_ref[0, :, cols])
        carry_ref[0, :, cols] = last2
        carry_ref[1, :, cols] = last
        up1 = jnp.concatenate([m1, up[:tm - SUBLANES]], axis=0)
        up2 = jnp.concatenate([m2, m1, up[:tm - 2 * SUBLANES]], axis=0)
        return (bconv_ref[:, cols] + wconv_ref[2:3, cols] * up
                + wconv_ref[1:2, cols] * up1 + wconv_ref[0:1, cols] * up2)

    def up_chunk(t, c):
        gcols = slice(c * F_CHUNK, (c + 1) * F_CHUNK)
        gate = conv(t, gcols)
        val = conv(t, slice(D_FF + c * F_CHUNK, D_FF + (c + 1) * F_CHUNK))
        hg = 0.5 * gate
        act_ref[slot(t), :, gcols] = ((hg + hg * jnp.tanh(hg)) * val).astype(BF16)

    def down_block(t, j):
        cs = slice(j * MXU_COLS, (j + 1) * MXU_COLS)
        return x_ref[t * tm:(t + 1) * tm, cs] + _dot(act_ref[slot(t)], wdown_ref[:, cs])

    def epilogue(t, ys):
        ssq = sum(jnp.sum(y * y, axis=-1, keepdims=True) for y in ys)
        rf = lax.rsqrt(ssq * (1.0 / D_MODEL) + EPS)
        for jb, y in enumerate(ys):
            cs = slice(jb * MXU_COLS, (jb + 1) * MXU_COLS)
            out = y * rf * gfin_ref[:, cs]
            for ll in range(MXU_COLS // LANES):
                lt = jb * (MXU_COLS // LANES) + ll
                for j in range(n_grp):
                    skew_ref[slot(t), lt, pl.ds(j, SUBLANES, stride=pitch), :] = (
                        out[SUBLANES * j:SUBLANES * (j + 1), ll * LANES:(ll + 1) * LANES])
        for lt in range(n_lane_tiles):
            for k in range(SUBLANES):
                o_ref[t * tm + n_grp * k:t * tm + n_grp * (k + 1), lt * LANES:(lt + 1) * LANES] = (
                    skew_ref[slot(t), lt, pitch * k:pitch * k + n_grp, :])

    down_at = {(k + 1) * n_chunks // (n_down + 1): k for k in range(n_down)}
    assert len(down_at) == n_down
    prologue(0)
    for t in range(n_tiles):
        if t + 1 < n_tiles:
            prologue(t + 1)
        ys = {}
        for c in range(n_chunks):
            up_chunk(t, c)
            if t >= 1 and c in down_at:
                ys[down_at[c]] = down_block(t - 1, down_at[c])
        if t >= 1:
            epilogue(t - 1, [ys[j] for j in range(n_down)])
    last = n_tiles - 1
    epilogue(last, [down_block(last, j) for j in range(n_down)])


def _conv_ffn(x2d, batch, seq, g_ffn, w_up, w_conv, b_conv, w_down, g_final):
    rows = FFN_TILES * TILE_ROWS
    steps_per_seq = seq // rows
    row_spec = pl.BlockSpec((rows, D_MODEL), lambda i: (i, 0))
    skew_rows = TILE_ROWS + SUBLANES * SUBLANES
    return pl.pallas_call(
        functools.partial(_ffn_kernel, steps_per_seq=steps_per_seq),
        grid=(batch * steps_per_seq,),
        in_specs=[
            row_spec,
            _resident((1, D_MODEL)),
            _resident((D_MODEL, 2 * D_FF)),
            _resident((CONV_WIDTH, 2 * D_FF)),
            _resident((1, 2 * D_FF)),
            _resident((D_FF, D_MODEL)),
            _resident((1, D_MODEL)),
        ],
        out_specs=row_spec,
        out_shape=jax.ShapeDtypeStruct(x2d.shape, F32),
        scratch_shapes=[
            pltpu.VMEM((FFN_SLOTS, D_MODEL // LANES, skew_rows, LANES), F32),
            pltpu.VMEM((FFN_SLOTS, TILE_ROWS, D_MODEL), BF16),
            pltpu.VMEM((CONV_WIDTH - 1, SUBLANES, 2 * D_FF), F32),
            pltpu.VMEM((FFN_SLOTS, TILE_ROWS, D_FF), BF16),
        ],
        compiler_params=pltpu.CompilerParams(
            dimension_semantics=("arbitrary",),
            vmem_limit_bytes=VMEM_LIMIT),
        name="conv_ffn",
    )(x2d, g_ffn, w_up, w_conv, b_conv, w_down, g_final)


def kernel(x, g_mix, w_in, g_sgu, w_s, b_s, sinks, rel_bias, w_pa, w_pb, w_out,
           g_ffn, w_up, w_conv, b_conv, w_down, g_final):
    batch, seq, d = x.shape
    assert w_in.shape[0] == 1 and d == D_MODEL and seq % (max(FFN_TILES, MIX_TILES) * TILE_ROWS) == 0 and w_in.shape[2] == IN_DIM
    bias_tbl = _bias_table(rel_bias)
    x2d = x.reshape(batch * seq, d)
    b_full = jnp.repeat(jnp.transpose(b_s[0]), A_WIDTH // A_GROUPS, axis=1)
    x2d, w_up_bf16, w_down_bf16 = _token_mixer(
        x2d, batch, seq, sinks[0], g_mix[0][None, :], w_in[0].astype(BF16), g_sgu[0][None, :],
        w_s[0], b_full, bias_tbl, w_pa[0].astype(BF16), w_pb[0].astype(BF16), w_out[0].astype(BF16),
        w_up[0], w_down[0])
    x2d = _conv_ffn(
        x2d, batch, seq, g_ffn[0][None, :], w_up_bf16, w_conv[0], b_conv[0][None, :],
        w_down_bf16, g_final[None, :])
    return x2d.reshape(batch, seq, d)
```

```python
import functools
import math

import jax
import jax.numpy as jnp
import numpy as np
from jax import lax
from jax.experimental import pallas as pl
from jax.experimental.pallas import tpu as pltpu

D_MODEL = 1024
CHUNK = 128
A_GROUPS = 4
A_WIDTH = 512
N_HEADS = 8
N_KV_HEADS = 2
HEAD_DIM = 64
Q_DIM = N_HEADS * HEAD_DIM
KV_DIM = N_KV_HEADS * HEAD_DIM
WINDOW = 128
BLOCK = 128
N_BUCKETS = 32
MAX_DISTANCE = 128
D_FF = 2816
CONV_WIDTH = 3
EPS = 1e-6
NEG_INF = -1e30

OFF_PU = 0
OFF_PV = OFF_PU + A_WIDTH
OFF_Q = OFF_PV + A_WIDTH
OFF_K = OFF_Q + Q_DIM
OFF_V = OFF_K + KV_DIM
OFF_GA = OFF_V + KV_DIM
OFF_GB = OFF_GA + D_MODEL
IN_DIM = OFF_GB + D_MODEL

LANES = 128
SUBLANES = 8
MXU_COLS = 256
TILE_ROWS = 512
F_CHUNK = 256
FFN_TILES = 2
FFN_SLOTS = 2
MIX_TILES = 2
VMEM_LIMIT = 56 * 1024 * 1024

F32 = jnp.float32
BF16 = jnp.bfloat16


def _gelu_tanh(x):
    c = math.sqrt(2.0 / math.pi)
    return x * (0.5 * (1.0 + jnp.tanh(c * (x + 0.044715 * (x * x * x)))))


def _sigmoid(x):
    return 0.5 * (1.0 + jnp.tanh(0.5 * x))


def _dot(a, b):
    return jnp.dot(a, b, preferred_element_type=F32)


def _band_buckets():
    i = np.arange(BLOCK)[:, None]
    j = np.arange(2 * BLOCK)[None, :]
    d = np.clip(i + BLOCK - j, 0, None)
    max_exact = N_BUCKETS // 2
    large = max_exact + (np.log(np.maximum(d, 1) / max_exact) / np.log(MAX_DISTANCE / max_exact)
                         * (N_BUCKETS - max_exact)).astype(np.int32)
    large = np.minimum(large, N_BUCKETS - 1)
    return np.where(d < max_exact, d, large).astype(np.int32)


def _bias_table_kernel(rb_ref, bucket_ref, o_ref):
    h = pl.program_id(0)
    bucket = bucket_ref[...]
    acc = jnp.zeros(bucket.shape, F32)
    for b in range(N_BUCKETS):
        acc = jnp.where(bucket == b, rb_ref[b, h], acc)
    o_ref[...] = acc


def _bias_table(rel_bias):
    buckets = jnp.asarray(_band_buckets())
    return pl.pallas_call(
        _bias_table_kernel,
        grid=(N_HEADS,),
        in_specs=[
            pl.BlockSpec(memory_space=pltpu.SMEM),
            pl.BlockSpec((BLOCK, 2 * BLOCK), lambda h: (0, 0)),
        ],
        out_specs=pl.BlockSpec((None, BLOCK, 2 * BLOCK), lambda h: (h, 0, 0)),
        out_shape=jax.ShapeDtypeStruct((N_HEADS, BLOCK, 2 * BLOCK), F32),
        name="rel_bias_table",
    )(rel_bias, buckets)


def _mixer_kernel(sinks_ref, x_ref, gmix_ref, win_ref, gsgu_ref, ws_ref, bfull_ref, bias_ref,
                  wpa_ref, wpb_ref, wout_ref, wup_f32_ref, wdown_f32_ref, o_ref, wup_bf16_ref, wdown_bf16_ref,
                  h_ref, u_ref, vn_ref, qlo_ref, qhi_ref, kd_ref, vd_ref, ya_ref, yb_ref, m_ref):
    wup_bf16_ref[...] = wup_f32_ref[...].astype(BF16)
    wdown_bf16_ref[...] = wdown_f32_ref[...].astype(BF16)

    tm = TILE_ROWS
    n_tiles = x_ref.shape[0] // tm
    blocks_per_tile = tm // BLOCK
    s_idx = pl.program_id(1)
    heads_per_kv = N_HEADS // N_KV_HEADS
    n_out = D_MODEL // MXU_COLS

    @pl.when(s_idx == 0)
    def _():
        kd_ref[:, 0:BLOCK, :] = jnp.zeros((N_KV_HEADS, BLOCK, LANES), BF16)
        vd_ref[:, 0:BLOCK, :] = jnp.zeros((N_KV_HEADS, BLOCK, LANES), BF16)

    row_c = lax.broadcasted_iota(jnp.int32, (CHUNK, CHUNK), 0)
    col_c = lax.broadcasted_iota(jnp.int32, (CHUNK, CHUNK), 1)
    tril = col_c <= row_c
    wmask = [jnp.where(tril, ws_ref[g], 0.0).astype(BF16) for g in range(A_GROUPS)]
    row_i = lax.broadcasted_iota(jnp.int32, (BLOCK, 2 * BLOCK), 0)
    col_j = lax.broadcasted_iota(jnp.int32, (BLOCK, 2 * BLOCK), 1)
    band = (col_j > row_i) & (col_j <= row_i + BLOCK)
    band_first = band & ((col_j >= BLOCK) | (s_idx > 0))
    b_is_lo = lax.broadcasted_iota(jnp.int32, (BLOCK, LANES), 1) < HEAD_DIM
    q_is_lo = (lax.broadcasted_iota(jnp.int32, (tm, Q_DIM), 1) % LANES) < HEAD_DIM
    t_is_lo = lax.broadcasted_iota(jnp.int32, (tm, LANES), 1) < HEAD_DIM

    def norm(t):
        x = x_ref[t * tm:(t + 1) * tm, :]
        r = lax.rsqrt(jnp.mean(x * x, axis=-1, keepdims=True) + EPS)
        h_ref[t] = (x * r * gmix_ref[...]).astype(BF16)

    def proj_u(t):
        u_ref[t] = _gelu_tanh(_dot(h_ref[t], win_ref[:, OFF_PU:OFF_PU + A_WIDTH]))

    def proj_v(t):
        pv = _gelu_tanh(_dot(h_ref[t], win_ref[:, OFF_PV:OFF_PV + A_WIDTH]))
        rv = lax.rsqrt(jnp.mean(pv * pv, axis=-1, keepdims=True) + EPS)
        vn_ref[t] = (pv * rv * gsgu_ref[...]).astype(BF16)

    def proj_q(t):
        q = _dot(h_ref[t], win_ref[:, OFF_Q:OFF_Q + Q_DIM]) * (HEAD_DIM ** -0.5)
        qlo_ref[t] = jnp.where(q_is_lo, q, 0.0).astype(BF16)
        qhi_ref[t] = jnp.where(q_is_lo, 0.0, q).astype(BF16)

    def proj_kv(t):
        kvp = _dot(h_ref[t], win_ref[:, OFF_K:OFF_K + 2 * KV_DIM])
        k = kvp[:, :KV_DIM]
        v = kvp[:, KV_DIM:]
        k_sw = pltpu.roll(k, HEAD_DIM, axis=1)
        v_sw = pltpu.roll(v, HEAD_DIM, axis=1)
        rows = slice(BLOCK + t * tm, BLOCK + (t + 1) * tm)
        kd_ref[0, rows, :] = jnp.where(t_is_lo, k, k_sw).astype(BF16)
        kd_ref[1, rows, :] = jnp.where(t_is_lo, k_sw, k).astype(BF16)
        vd_ref[0, rows, :] = jnp.where(t_is_lo, v, v_sw).astype(BF16)
        vd_ref[1, rows, :] = jnp.where(t_is_lo, v_sw, v).astype(BF16)

    def sgu(t, g):
        cols = slice(g * CHUNK, (g + 1) * CHUNK)
        v_chunks = [vn_ref[t, b * CHUNK:(b + 1) * CHUNK, cols] for b in range(tm // CHUNK)]
        s_all = _dot(wmask[g], jnp.concatenate(v_chunks, axis=1))
        for b in range(tm // CHUNK):
            rows = slice(b * CHUNK, (b + 1) * CHUNK)
            s = s_all[:, b * CHUNK:(b + 1) * CHUNK] + bfull_ref[:, cols]
            ya_ref[t, rows, cols] = (u_ref[t, rows, cols] * s).astype(BF16)

    def attn(t, b, kv):
        rows = slice(b * BLOCK, (b + 1) * BLOCK)
        bb = t * blocks_per_tile + b
        mask = band_first if bb == 0 else band
        q_parts = []
        for g in range(heads_per_kv):
            hh = kv * heads_per_kv + g
            src = qlo_ref if hh % 2 == 0 else qhi_ref
            c = hh // 2
            q_parts.append(src[t, rows, c * LANES:(c + 1) * LANES])
        q_stack = jnp.concatenate(q_parts, axis=0)
        keys = kd_ref[kv, bb * BLOCK:(bb + 2) * BLOCK, :]
        sc = lax.dot_general(q_stack, keys, (((1,), (1,)), ((), ())),
                             preferred_element_type=F32)
        p_parts = []
        inv_l = []
        for g in range(heads_per_kv):
            hh = kv * heads_per_kv + g
            sg = sc[g * BLOCK:(g + 1) * BLOCK] + bias_ref[hh]
            sg = jnp.where(mask, sg, NEG_INF)
            sink = sinks_ref[hh]
            m = jnp.maximum(jnp.max(sg, axis=-1, keepdims=True), sink)
            p = jnp.exp(sg - m)
            l = jnp.sum(p, axis=-1, keepdims=True) + jnp.exp(sink - m)
            p_parts.append(p.astype(BF16))
            inv_l.append(1.0 / l)
        p_stack = jnp.concatenate(p_parts, axis=0)
        vals = vd_ref[kv, bb * BLOCK:(bb + 2) * BLOCK, :]
        o = _dot(p_stack, vals)
        for cc in range(heads_per_kv // 2):
            o_even = o[(2 * cc) * BLOCK:(2 * cc + 1) * BLOCK] * inv_l[2 * cc]
            o_odd = o[(2 * cc + 1) * BLOCK:(2 * cc + 2) * BLOCK] * inv_l[2 * cc + 1]
            col = kv * (heads_per_kv // 2) + cc
            yb_ref[t, rows, col * LANES:(col + 1) * LANES] = jnp.where(b_is_lo, o_even, o_odd).astype(BF16)

    def merge(t, c):
        cs = slice(c * MXU_COLS, (c + 1) * MXU_COLS)
        ga = _dot(h_ref[t], win_ref[:, OFF_GA + c * MXU_COLS:OFF_GA + (c + 1) * MXU_COLS])
        gb = _dot(h_ref[t], win_ref[:, OFF_GB + c * MXU_COLS:OFF_GB + (c + 1) * MXU_COLS])
        pa = _dot(ya_ref[t], wpa_ref[:, cs])
        pb = _dot(yb_ref[t], wpb_ref[:, cs])
        m_ref[t, :, cs] = (_sigmoid(ga) * pa + _sigmoid(gb) * pb).astype(BF16)

    def out_proj(t, c):
        cs = slice(c * MXU_COLS, (c + 1) * MXU_COLS)
        val = x_ref[t * tm:(t + 1) * tm, cs] + _dot(m_ref[t], wout_ref[:, cs])
        n_grp = tm // SUBLANES
        n_lt = D_MODEL // LANES
        for ll in range(MXU_COLS // LANES):
            lt = c * (MXU_COLS // LANES) + ll
            for g in range(n_grp):
                k, j0 = divmod(SUBLANES * g, n_grp)
                o_ref[t * n_lt + lt, pl.ds(SUBLANES * j0 + k, SUBLANES, stride=SUBLANES), :] = (
                    val[SUBLANES * g:SUBLANES * (g + 1), ll * LANES:(ll + 1) * LANES])

    P = functools.partial

    def stage1(t):
        return [P(proj_u, t), P(proj_v, t), P(proj_q, t), P(proj_kv, t)]

    def stage2(t):
        sgu_items = [P(sgu, t, g) for g in range(A_GROUPS)]
        attn_items = [P(attn, t, b, kv) for b in range(blocks_per_tile) for kv in range(N_KV_HEADS)]
        items = []
        for i, item in enumerate(attn_items):
            items += sgu_items[i * A_GROUPS // len(attn_items):(i + 1) * A_GROUPS // len(attn_items)] + [item]
        return items

    def stage3(t):
        return [P(merge, t, c) for c in range(n_out)]

    def stage4(t):
        return [P(out_proj, t, c) for c in range(n_out)]

    def interleave(heavy, light):
        n, m = len(heavy), len(light)
        done = 0
        for i, item in enumerate(heavy):
            item()
            upto = (i + 1) * m // n
            for light_item in light[done:upto]:
                light_item()
            done = upto

    def run(items):
        for item in items:
            item()

    norm(0)
    run(stage1(0))
    for t in range(n_tiles):
        heavy = []
        if t >= 2:
            heavy += stage4(t - 2)
        if t >= 1:
            heavy += stage3(t - 1)
        if t + 1 < n_tiles:
            norm(t + 1)
            heavy += stage1(t + 1)
        interleave(heavy, stage2(t)) if heavy else run(stage2(t))
    if n_tiles >= 2:
        run(stage4(n_tiles - 2))
    run(stage3(n_tiles - 1))
    run(stage4(n_tiles - 1))

    rows_all = n_tiles * tm
    kd_ref[:, 0:BLOCK, :] = kd_ref[:, rows_all:rows_all + BLOCK, :]
    vd_ref[:, 0:BLOCK, :] = vd_ref[:, rows_all:rows_all + BLOCK, :]


def _resident(shape):
    nd = len(shape)
    return pl.BlockSpec(shape, lambda *_: (0,) * nd, pipeline_mode=pl.Buffered(1))


def _token_mixer(x2d, batch, seq, sinks, g_mix, w_in, g_sgu, w_s, b_full, bias_tbl, w_pa, w_pb, w_out,
                 w_up_f32, w_down_f32):
    tm = TILE_ROWS
    rows = MIX_TILES * tm
    ns = seq // rows
    n_steps = batch * ns
    n_lt = D_MODEL // LANES
    row_spec = pl.BlockSpec((rows, D_MODEL), lambda b, s: (b * ns + s, 0))
    up_slab = D_MODEL // n_steps
    down_slab = 16 * 11
    n_down_slabs = D_FF // down_slab
    assert D_MODEL % n_steps == 0 and up_slab % 16 == 0 and D_FF % down_slab == 0 and n_down_slabs <= n_steps
    up_spec = pl.BlockSpec((up_slab, 2 * D_FF), lambda b, s: (b * ns + s, 0))
    down_spec = pl.BlockSpec((down_slab, D_MODEL), lambda b, s: (jnp.minimum(b * ns + s, n_down_slabs - 1), 0))
    return pl.pallas_call(
        _mixer_kernel,
        grid=(batch, ns),
        in_specs=[
            pl.BlockSpec(memory_space=pltpu.SMEM),
            row_spec,
            _resident((1, D_MODEL)),
            _resident((D_MODEL, IN_DIM)),
            _resident((1, A_WIDTH)),
            _resident((A_GROUPS, CHUNK, CHUNK)),
            _resident((CHUNK, A_WIDTH)),
            _resident((N_HEADS, BLOCK, 2 * BLOCK)),
            _resident((A_WIDTH, D_MODEL)),
            _resident((Q_DIM, D_MODEL)),
            _resident((D_MODEL, D_MODEL)),
            up_spec,
            down_spec,
        ],
        out_specs=[pl.BlockSpec((MIX_TILES * n_lt, tm, LANES), lambda b, s: (b * ns + s, 0, 0)), up_spec, down_spec],
        out_shape=[jax.ShapeDtypeStruct((x2d.shape[0] // tm * n_lt, tm, LANES), F32),
                   jax.ShapeDtypeStruct(w_up_f32.shape, BF16),
                   jax.ShapeDtypeStruct(w_down_f32.shape, BF16)],
        scratch_shapes=[
            pltpu.VMEM((MIX_TILES, tm, D_MODEL), BF16),
            pltpu.VMEM((MIX_TILES, tm, A_WIDTH), F32),
            pltpu.VMEM((MIX_TILES, tm, A_WIDTH), BF16),
            pltpu.VMEM((MIX_TILES, tm, Q_DIM), BF16),
            pltpu.VMEM((MIX_TILES, tm, Q_DIM), BF16),
            pltpu.VMEM((N_KV_HEADS, BLOCK + rows, LANES), BF16),
            pltpu.VMEM((N_KV_HEADS, BLOCK + rows, LANES), BF16),
            pltpu.VMEM((MIX_TILES, tm, A_WIDTH), BF16),
            pltpu.VMEM((MIX_TILES, tm, Q_DIM), BF16),
            pltpu.VMEM((MIX_TILES, tm, D_MODEL), BF16),
        ],
        compiler_params=pltpu.CompilerParams(
            dimension_semantics=("arbitrary", "arbitrary"),
            vmem_limit_bytes=VMEM_LIMIT),
        name="token_mixer",
    )(sinks, x2d, g_mix, w_in, g_sgu, w_s, b_full, bias_tbl, w_pa, w_pb, w_out, w_up_f32, w_down_f32)


def _ffn_kernel(x_ref, gffn_ref, wup_ref, wconv_ref, bconv_ref, wdown_ref, gfin_ref, o_ref,
                skew_ref, h_ref, carry_ref, act_ref, *, steps_per_seq):
    tm = TILE_ROWS
    n_tiles = o_ref.shape[0] // tm
    n_grp = tm // SUBLANES
    pitch = n_grp + SUBLANES
    n_lane_tiles = D_MODEL // LANES
    n_chunks = D_FF // F_CHUNK
    n_down = D_MODEL // MXU_COLS

    @pl.when(pl.program_id(0) % steps_per_seq == 0)
    def _():
        carry_ref[...] = jnp.zeros(carry_ref.shape, F32)

    def slot(t):
        return t % FFN_SLOTS

    def x_cols(t, lt0, lt1):
        return jnp.concatenate([x_ref[t * n_lane_tiles + lt] for lt in range(lt0, lt1)], axis=1)

    def prologue(t):
        x = x_cols(t, 0, n_lane_tiles)
        r = lax.rsqrt(jnp.mean(x * x, axis=-1, keepdims=True) + EPS)
        h_ref[slot(t)] = (x * r * gffn_ref[...]).astype(BF16)

    first_sublane = lax.broadcasted_iota(jnp.int32, (SUBLANES, F_CHUNK), 0) == 0

    def conv(t, cols):
        up = _dot(h_ref[slot(t)], wup_ref[:, cols])
        last = up[tm - SUBLANES:tm]
        last2 = up[tm - 2 * SUBLANES:tm - SUBLANES]

        def wrap(cur, prev):
            return jnp.where(first_sublane, pltpu.roll(prev, 1, axis=0), pltpu.roll(cur, 1, axis=0))

        m1 = wrap(last, carry_ref[1, :, cols])
        m2 = wrap(last2, carry_ref[0, :, cols])
        carry_ref[0, :, cols] = last2
        carry_ref[1, :, cols] = last
        up1 = jnp.concatenate([m1, up[:tm - SUBLANES]], axis=0)
        up2 = jnp.concatenate([m2, m1, up[:tm - 2 * SUBLANES]], axis=0)
        return (bconv_ref[:, cols] + wconv_ref[2:3, cols] * up
                + wconv_ref[1:2, cols] * up1 + wconv_ref[0:1, cols] * up2)

    def up_chunk(t, c):
        gcols = slice(c * F_CHUNK, (c + 1) * F_CHUNK)
        gate = conv(t, gcols)
        val = conv(t, slice(D_FF + c * F_CHUNK, D_FF + (c + 1) * F_CHUNK))
        hg = 0.5 * gate
        act_ref[slot(t), :, gcols] = ((hg + hg * jnp.tanh(hg)) * val).astype(BF16)

    def down_block(t, j):
        cs = slice(j * MXU_COLS, (j + 1) * MXU_COLS)
        lts = MXU_COLS // LANES
        return x_cols(t, j * lts, (j + 1) * lts) + _dot(act_ref[slot(t)], wdown_ref[:, cs])

    def epilogue(t, ys):
        ssq = sum(jnp.sum(y * y, axis=-1, keepdims=True) for y in ys)
        rf = lax.rsqrt(ssq * (1.0 / D_MODEL) + EPS)
        for jb, y in enumerate(ys):
            cs = slice(jb * MXU_COLS, (jb + 1) * MXU_COLS)
            out = y * rf * gfin_ref[:, cs]
            for ll in range(MXU_COLS // LANES):
                lt = jb * (MXU_COLS // LANES) + ll
                for j in range(n_grp):
                    skew_ref[slot(t), lt, pl.ds(j, SUBLANES, stride=pitch), :] = (
                        out[SUBLANES * j:SUBLANES * (j + 1), ll * LANES:(ll + 1) * LANES])
        for lt in range(n_lane_tiles):
            for k in range(SUBLANES):
                o_ref[t * tm + n_grp * k:t * tm + n_grp * (k + 1), lt * LANES:(lt + 1) * LANES] = (
                    skew_ref[slot(t), lt, pitch * k:pitch * k + n_grp, :])

    down_at = {(k + 1) * n_chunks // (n_down + 1): k for k in range(n_down)}
    assert len(down_at) == n_down
    prologue(0)
    for t in range(n_tiles):
        if t + 1 < n_tiles:
            prologue(t + 1)
        ys = {}
        for c in range(n_chunks):
            up_chunk(t, c)
            if t >= 1 and c in down_at:
                ys[down_at[c]] = down_block(t - 1, down_at[c])
        if t >= 1:
            epilogue(t - 1, [ys[j] for j in range(n_down)])
    last = n_tiles - 1
    epilogue(last, [down_block(last, j) for j in range(n_down)])


def _conv_ffn(x2d, batch, seq, g_ffn, w_up, w_conv, b_conv, w_down, g_final):
    rows = FFN_TILES * TILE_ROWS
    steps_per_seq = seq // rows
    row_spec = pl.BlockSpec((rows, D_MODEL), lambda i: (i, 0))
    skew_rows = TILE_ROWS + SUBLANES * SUBLANES
    return pl.pallas_call(
        functools.partial(_ffn_kernel, steps_per_seq=steps_per_seq),
        grid=(batch * steps_per_seq,),
        in_specs=[
            pl.BlockSpec((FFN_TILES * (D_MODEL // LANES), TILE_ROWS, LANES), lambda i: (i, 0, 0)),
            _resident((1, D_MODEL)),
            _resident((D_MODEL, 2 * D_FF)),
            _resident((CONV_WIDTH, 2 * D_FF)),
            _resident((1, 2 * D_FF)),
            _resident((D_FF, D_MODEL)),
            _resident((1, D_MODEL)),
        ],
        out_specs=row_spec,
        out_shape=jax.ShapeDtypeStruct((x2d.shape[0] // (D_MODEL // LANES) * TILE_ROWS, D_MODEL), F32),
        scratch_shapes=[
            pltpu.VMEM((FFN_SLOTS, D_MODEL // LANES, skew_rows, LANES), F32),
            pltpu.VMEM((FFN_SLOTS, TILE_ROWS, D_MODEL), BF16),
            pltpu.VMEM((CONV_WIDTH - 1, SUBLANES, 2 * D_FF), F32),
            pltpu.VMEM((FFN_SLOTS, TILE_ROWS, D_FF), BF16),
        ],
        compiler_params=pltpu.CompilerParams(
            dimension_semantics=("arbitrary",),
            vmem_limit_bytes=VMEM_LIMIT),
        name="conv_ffn",
    )(x2d, g_ffn, w_up, w_conv, b_conv, w_down, g_final)


def kernel(x, g_mix, w_in, g_sgu, w_s, b_s, sinks, rel_bias, w_pa, w_pb, w_out,
           g_ffn, w_up, w_conv, b_conv, w_down, g_final):
    batch, seq, d = x.shape
    assert w_in.shape[0] == 1 and d == D_MODEL and seq % (max(FFN_TILES, MIX_TILES) * TILE_ROWS) == 0 and w_in.shape[2] == IN_DIM
    bias_tbl = _bias_table(rel_bias)
    x2d = x.reshape(batch * seq, d)
    b_full = jnp.repeat(jnp.transpose(b_s[0]), A_WIDTH // A_GROUPS, axis=1)
    x2d, w_up_bf16, w_down_bf16 = _token_mixer(
        x2d, batch, seq, sinks[0], g_mix[0][None, :], w_in[0].astype(BF16), g_sgu[0][None, :],
        w_s[0], b_full, bias_tbl, w_pa[0].astype(BF16), w_pb[0].astype(BF16), w_out[0].astype(BF16),
        w_up[0], w_down[0])
    x2d = _conv_ffn(
        x2d, batch, seq, g_ffn[0][None, :], w_up_bf16, w_conv[0], b_conv[0][None, :],
        w_down_bf16, g_final[None, :])
    return x2d.reshape(batch, seq, d)
```

```python
import functools
import math

import jax
import jax.numpy as jnp
import numpy as np
from jax import lax
from jax.experimental import pallas as pl
from jax.experimental.pallas import tpu as pltpu

D_MODEL = 1024
CHUNK = 128
A_GROUPS = 4
A_WIDTH = 512
N_HEADS = 8
N_KV_HEADS = 2
HEAD_DIM = 64
Q_DIM = N_HEADS * HEAD_DIM
KV_DIM = N_KV_HEADS * HEAD_DIM
WINDOW = 128
BLOCK = 128
N_BUCKETS = 32
MAX_DISTANCE = 128
D_FF = 2816
CONV_WIDTH = 3
EPS = 1e-6
NEG_INF = -1e30

OFF_PU = 0
OFF_PV = OFF_PU + A_WIDTH
OFF_Q = OFF_PV + A_WIDTH
OFF_K = OFF_Q + Q_DIM
OFF_V = OFF_K + KV_DIM
OFF_GA = OFF_V + KV_DIM
OFF_GB = OFF_GA + D_MODEL
IN_DIM = OFF_GB + D_MODEL

LANES = 128
SUBLANES = 8
MXU_COLS = 256
TILE_ROWS = 512
F_CHUNK = 256
FFN_TILES = 2
FFN_SLOTS = 2
MIX_TILES = 2
VMEM_LIMIT = 56 * 1024 * 1024

F32 = jnp.float32
BF16 = jnp.bfloat16


def _gelu_tanh(x):
    c = math.sqrt(2.0 / math.pi)
    return x * (0.5 * (1.0 + jnp.tanh(c * (x + 0.044715 * (x * x * x)))))


def _sigmoid(x):
    return 0.5 * (1.0 + jnp.tanh(0.5 * x))


def _dot(a, b):
    return jnp.dot(a, b, preferred_element_type=F32)


def _band_buckets():
    i = np.arange(BLOCK)[:, None]
    j = np.arange(2 * BLOCK)[None, :]
    d = np.clip(i + BLOCK - j, 0, None)
    max_exact = N_BUCKETS // 2
    large = max_exact + (np.log(np.maximum(d, 1) / max_exact) / np.log(MAX_DISTANCE / max_exact)
                         * (N_BUCKETS - max_exact)).astype(np.int32)
    large = np.minimum(large, N_BUCKETS - 1)
    return np.where(d < max_exact, d, large).astype(np.int32)


def _bias_table_kernel(rb_ref, bucket_ref, o_ref):
    h = pl.program_id(0)
    bucket = bucket_ref[...]
    acc = jnp.zeros(bucket.shape, F32)
    for b in range(N_BUCKETS):
        acc = jnp.where(bucket == b, rb_ref[b, h], acc)
    o_ref[...] = acc


def _bias_table(rel_bias):
    buckets = jnp.asarray(_band_buckets())
    return pl.pallas_call(
        _bias_table_kernel,
        grid=(N_HEADS,),
        in_specs=[
            pl.BlockSpec(memory_space=pltpu.SMEM),
            pl.BlockSpec((BLOCK, 2 * BLOCK), lambda h: (0, 0)),
        ],
        out_specs=pl.BlockSpec((None, BLOCK, 2 * BLOCK), lambda h: (h, 0, 0)),
        out_shape=jax.ShapeDtypeStruct((N_HEADS, BLOCK, 2 * BLOCK), F32),
        name="rel_bias_table",
    )(rel_bias, buckets)


def _mixer_kernel(sinks_ref, x_ref, gmix_ref, win_ref, gsgu_ref, ws_ref, bfull_ref, bias_ref,
                  wpa_ref, wpb_ref, wout_ref, wup_f32_ref, wdown_f32_ref, o_ref, wup_bf16_ref, wdown_bf16_ref,
                  h_ref, u_ref, vn_ref, qlo_ref, qhi_ref, kd_ref, vd_ref, ya_ref, yb_ref, m_ref):
    wup_bf16_ref[...] = wup_f32_ref[...].astype(BF16)
    wdown_bf16_ref[...] = wdown_f32_ref[...].astype(BF16)

    tm = TILE_ROWS
    n_tiles = x_ref.shape[0] // tm
    blocks_per_tile = tm // BLOCK
    s_idx = pl.program_id(1)
    heads_per_kv = N_HEADS // N_KV_HEADS
    n_out = D_MODEL // MXU_COLS

    @pl.when(s_idx == 0)
    def _():
        kd_ref[:, 0:BLOCK, :] = jnp.zeros((N_KV_HEADS, BLOCK, LANES), BF16)
        vd_ref[:, 0:BLOCK, :] = jnp.zeros((N_KV_HEADS, BLOCK, LANES), BF16)

    row_c = lax.broadcasted_iota(jnp.int32, (CHUNK, CHUNK), 0)
    col_c = lax.broadcasted_iota(jnp.int32, (CHUNK, CHUNK), 1)
    tril = col_c <= row_c
    wmask = [jnp.where(tril, ws_ref[g], 0.0).astype(BF16) for g in range(A_GROUPS)]
    row_i = lax.broadcasted_iota(jnp.int32, (BLOCK, 2 * BLOCK), 0)
    col_j = lax.broadcasted_iota(jnp.int32, (BLOCK, 2 * BLOCK), 1)
    band = (col_j > row_i) & (col_j <= row_i + BLOCK)
    band_first = band & ((col_j >= BLOCK) | (s_idx > 0))
    b_is_lo = lax.broadcasted_iota(jnp.int32, (BLOCK, LANES), 1) < HEAD_DIM
    q_is_lo = (lax.broadcasted_iota(jnp.int32, (tm, Q_DIM), 1) % LANES) < HEAD_DIM
    t_is_lo = lax.broadcasted_iota(jnp.int32, (tm, LANES), 1) < HEAD_DIM

    def norm(t):
        x = x_ref[t * tm:(t + 1) * tm, :]
        r = lax.rsqrt(jnp.mean(x * x, axis=-1, keepdims=True) + EPS)
        h_ref[t] = (x * r * gmix_ref[...]).astype(BF16)

    def proj_u(t):
        u_ref[t] = _gelu_tanh(_dot(h_ref[t], win_ref[:, OFF_PU:OFF_PU + A_WIDTH]))

    def proj_v(t):
        pv = _gelu_tanh(_dot(h_ref[t], win_ref[:, OFF_PV:OFF_PV + A_WIDTH]))
        rv = lax.rsqrt(jnp.mean(pv * pv, axis=-1, keepdims=True) + EPS)
        vn_ref[t] = (pv * rv * gsgu_ref[...]).astype(BF16)

    def proj_q(t):
        q = _dot(h_ref[t], win_ref[:, OFF_Q:OFF_Q + Q_DIM]) * (HEAD_DIM ** -0.5)
        qlo_ref[t] = jnp.where(q_is_lo, q, 0.0).astype(BF16)
        qhi_ref[t] = jnp.where(q_is_lo, 0.0, q).astype(BF16)

    def proj_kv(t):
        kvp = _dot(h_ref[t], win_ref[:, OFF_K:OFF_K + 2 * KV_DIM])
        k = kvp[:, :KV_DIM]
        v = kvp[:, KV_DIM:]
        k_sw = pltpu.roll(k, HEAD_DIM, axis=1)
        v_sw = pltpu.roll(v, HEAD_DIM, axis=1)
        rows = slice(BLOCK + t * tm, BLOCK + (t + 1) * tm)
        kd_ref[0, rows, :] = jnp.where(t_is_lo, k, k_sw).astype(BF16)
        kd_ref[1, rows, :] = jnp.where(t_is_lo, k_sw, k).astype(BF16)
        vd_ref[0, rows, :] = jnp.where(t_is_lo, v, v_sw).astype(BF16)
        vd_ref[1, rows, :] = jnp.where(t_is_lo, v_sw, v).astype(BF16)

    def sgu(t, g):
        cols = slice(g * CHUNK, (g + 1) * CHUNK)
        v_chunks = [vn_ref[t, b * CHUNK:(b + 1) * CHUNK, cols] for b in range(tm // CHUNK)]
        s_all = _dot(wmask[g], jnp.concatenate(v_chunks, axis=1))
        for b in range(tm // CHUNK):
            rows = slice(b * CHUNK, (b + 1) * CHUNK)
            s = s_all[:, b * CHUNK:(b + 1) * CHUNK] + bfull_ref[:, cols]
            ya_ref[t, rows, cols] = (u_ref[t, rows, cols] * s).astype(BF16)

    probs = {}

    def attn_scores(t, b, kv):
        rows = slice(b * BLOCK, (b + 1) * BLOCK)
        bb = t * blocks_per_tile + b
        mask = band_first if bb == 0 else band
        q_parts = []
        for g in range(heads_per_kv):
            hh = kv * heads_per_kv + g
            src = qlo_ref if hh % 2 == 0 else qhi_ref
            c = hh // 2
            q_parts.append(src[t, rows, c * LANES:(c + 1) * LANES])
        q_stack = jnp.concatenate(q_parts, axis=0)
        keys = kd_ref[kv, bb * BLOCK:(bb + 2) * BLOCK, :]
        sc = lax.dot_general(q_stack, keys, (((1,), (1,)), ((), ())),
                             preferred_element_type=F32)
        p_parts = []
        inv_l = []
        for g in range(heads_per_kv):
            hh = kv * heads_per_kv + g
            sg = sc[g * BLOCK:(g + 1) * BLOCK] + bias_ref[hh]
            sg = jnp.where(mask, sg, NEG_INF)
            sink = sinks_ref[hh]
            m = jnp.maximum(jnp.max(sg, axis=-1, keepdims=True), sink)
            p = jnp.exp(sg - m)
            l = jnp.sum(p, axis=-1, keepdims=True) + jnp.exp(sink - m)
            p_parts.append(p.astype(BF16))
            inv_l.append(1.0 / l)
        probs[t, b, kv] = (jnp.concatenate(p_parts, axis=0), inv_l)

    def attn_values(t, b, kv):
        rows = slice(b * BLOCK, (b + 1) * BLOCK)
        bb = t * blocks_per_tile + b
        p_stack, inv_l = probs.pop((t, b, kv))
        vals = vd_ref[kv, bb * BLOCK:(bb + 2) * BLOCK, :]
        o = _dot(p_stack, vals)
        for cc in range(heads_per_kv // 2):
            o_even = o[(2 * cc) * BLOCK:(2 * cc + 1) * BLOCK] * inv_l[2 * cc]
            o_odd = o[(2 * cc + 1) * BLOCK:(2 * cc + 2) * BLOCK] * inv_l[2 * cc + 1]
            col = kv * (heads_per_kv // 2) + cc
            yb_ref[t, rows, col * LANES:(col + 1) * LANES] = jnp.where(b_is_lo, o_even, o_odd).astype(BF16)

    def merge(t, c):
        cs = slice(c * MXU_COLS, (c + 1) * MXU_COLS)
        ga = _dot(h_ref[t], win_ref[:, OFF_GA + c * MXU_COLS:OFF_GA + (c + 1) * MXU_COLS])
        gb = _dot(h_ref[t], win_ref[:, OFF_GB + c * MXU_COLS:OFF_GB + (c + 1) * MXU_COLS])
        pa = _dot(ya_ref[t], wpa_ref[:, cs])
        pb = _dot(yb_ref[t], wpb_ref[:, cs])
        m_ref[t, :, cs] = (_sigmoid(ga) * pa + _sigmoid(gb) * pb).astype(BF16)

    def out_proj(t, c):
        cs = slice(c * MXU_COLS, (c + 1) * MXU_COLS)
        val = x_ref[t * tm:(t + 1) * tm, cs] + _dot(m_ref[t], wout_ref[:, cs])
        n_grp = tm // SUBLANES
        n_lt = D_MODEL // LANES
        for ll in range(MXU_COLS // LANES):
            lt = c * (MXU_COLS // LANES) + ll
            for g in range(n_grp):
                k, j0 = divmod(SUBLANES * g, n_grp)
                o_ref[t * n_lt + lt, pl.ds(SUBLANES * j0 + k, SUBLANES, stride=SUBLANES), :] = (
                    val[SUBLANES * g:SUBLANES * (g + 1), ll * LANES:(ll + 1) * LANES])

    P = functools.partial

    def stage1(t):
        return [P(proj_u, t), P(proj_v, t), P(proj_q, t), P(proj_kv, t)]

    def stage2(t):
        sgu_items = [P(sgu, t, g) for g in range(A_GROUPS)]
        blocks = [(b, kv) for b in range(blocks_per_tile) for kv in range(N_KV_HEADS)]
        attn_items = [P(attn_scores, t, *blocks[0])]
        for i, blk in enumerate(blocks):
            if i + 1 < len(blocks):
                attn_items.append(P(attn_scores, t, *blocks[i + 1]))
            attn_items.append(P(attn_values, t, *blk))
        items = []
        for i, item in enumerate(attn_items):
            items += sgu_items[i * A_GROUPS // len(attn_items):(i + 1) * A_GROUPS // len(attn_items)] + [item]
        return items

    def stage3(t):
        return [P(merge, t, c) for c in range(n_out)]

    def stage4(t):
        return [P(out_proj, t, c) for c in range(n_out)]

    def interleave(heavy, light):
        n, m = len(heavy), len(light)
        done = 0
        for i, item in enumerate(heavy):
            item()
            upto = (i + 1) * m // n
            for light_item in light[done:upto]:
                light_item()
            done = upto

    def run(items):
        for item in items:
            item()

    norm(0)
    run(stage1(0))
    for t in range(n_tiles):
        heavy = []
        if t >= 2:
            heavy += stage4(t - 2)
        if t >= 1:
            heavy += stage3(t - 1)
        if t + 1 < n_tiles:
            norm(t + 1)
            heavy += stage1(t + 1)
        interleave(heavy, stage2(t)) if heavy else run(stage2(t))
    if n_tiles >= 2:
        run(stage4(n_tiles - 2))
    run(stage3(n_tiles - 1))
    run(stage4(n_tiles - 1))

    rows_all = n_tiles * tm
    kd_ref[:, 0:BLOCK, :] = kd_ref[:, rows_all:rows_all + BLOCK, :]
    vd_ref[:, 0:BLOCK, :] = vd_ref[:, rows_all:rows_all + BLOCK, :]


def _resident(shape):
    nd = len(shape)
    return pl.BlockSpec(shape, lambda *_: (0,) * nd, pipeline_mode=pl.Buffered(1))


def _token_mixer(x2d, batch, seq, sinks, g_mix, w_in, g_sgu, w_s, b_full, bias_tbl, w_pa, w_pb, w_out,
                 w_up_f32, w_down_f32):
    tm = TILE_ROWS
    rows = MIX_TILES * tm
    ns = seq // rows
    n_steps = batch * ns
    n_lt = D_MODEL // LANES
    row_spec = pl.BlockSpec((rows, D_MODEL), lambda b, s: (b * ns + s, 0))
    up_slab = D_MODEL // n_steps
    down_slab = 16 * 11
    n_down_slabs = D_FF // down_slab
    assert D_MODEL % n_steps == 0 and up_slab % 16 == 0 and D_FF % down_slab == 0 and n_down_slabs <= n_steps
    up_spec = pl.BlockSpec((up_slab, 2 * D_FF), lambda b, s: (b * ns + s, 0))
    down_spec = pl.BlockSpec((down_slab, D_MODEL), lambda b, s: (jnp.minimum(b * ns + s, n_down_slabs - 1), 0))
    return pl.pallas_call(
        _mixer_kernel,
        grid=(batch, ns),
        in_specs=[
            pl.BlockSpec(memory_space=pltpu.SMEM),
            row_spec,
            _resident((1, D_MODEL)),
            _resident((D_MODEL, IN_DIM)),
            _resident((1, A_WIDTH)),
            _resident((A_GROUPS, CHUNK, CHUNK)),
            _resident((CHUNK, A_WIDTH)),
            _resident((N_HEADS, BLOCK, 2 * BLOCK)),
            _resident((A_WIDTH, D_MODEL)),
            _resident((Q_DIM, D_MODEL)),
            _resident((D_MODEL, D_MODEL)),
            up_spec,
            down_spec,
        ],
        out_specs=[pl.BlockSpec((MIX_TILES * n_lt, tm, LANES), lambda b, s: (b * ns + s, 0, 0)), up_spec, down_spec],
        out_shape=[jax.ShapeDtypeStruct((x2d.shape[0] // tm * n_lt, tm, LANES), F32),
                   jax.ShapeDtypeStruct(w_up_f32.shape, BF16),
                   jax.ShapeDtypeStruct(w_down_f32.shape, BF16)],
        scratch_shapes=[
            pltpu.VMEM((MIX_TILES, tm, D_MODEL), BF16),
            pltpu.VMEM((MIX_TILES, tm, A_WIDTH), F32),
            pltpu.VMEM((MIX_TILES, tm, A_WIDTH), BF16),
            pltpu.VMEM((MIX_TILES, tm, Q_DIM), BF16),
            pltpu.VMEM((MIX_TILES, tm, Q_DIM), BF16),
            pltpu.VMEM((N_KV_HEADS, BLOCK + rows, LANES), BF16),
            pltpu.VMEM((N_KV_HEADS, BLOCK + rows, LANES), BF16),
            pltpu.VMEM((MIX_TILES, tm, A_WIDTH), BF16),
            pltpu.VMEM((MIX_TILES, tm, Q_DIM), BF16),
            pltpu.VMEM((MIX_TILES, tm, D_MODEL), BF16),
        ],
        compiler_params=pltpu.CompilerParams(
            dimension_semantics=("arbitrary", "arbitrary"),
            vmem_limit_bytes=VMEM_LIMIT),
        name="token_mixer",
    )(sinks, x2d, g_mix, w_in, g_sgu, w_s, b_full, bias_tbl, w_pa, w_pb, w_out, w_up_f32, w_down_f32)


def _ffn_kernel(x_ref, gffn_ref, wup_ref, wconv_ref, bconv_ref, wdown_ref, gfin_ref, o_ref,
                skew_ref, h_ref, carry_ref, act_ref, *, steps_per_seq):
    tm = TILE_ROWS
    n_tiles = o_ref.shape[0] // tm
    n_grp = tm // SUBLANES
    pitch = n_grp + SUBLANES
    n_lane_tiles = D_MODEL // LANES
    n_chunks = D_FF // F_CHUNK
    n_down = D_MODEL // MXU_COLS

    @pl.when(pl.program_id(0) % steps_per_seq == 0)
    def _():
        carry_ref[...] = jnp.zeros(carry_ref.shape, F32)

    def slot(t):
        return t % FFN_SLOTS

    def x_cols(t, lt0, lt1):
        return jnp.concatenate([x_ref[t * n_lane_tiles + lt] for lt in range(lt0, lt1)], axis=1)

    def prologue(t):
        x = x_cols(t, 0, n_lane_tiles)
        r = lax.rsqrt(jnp.mean(x * x, axis=-1, keepdims=True) + EPS)
        h_ref[slot(t)] = (x * r * gffn_ref[...]).astype(BF16)

    first_sublane = lax.broadcasted_iota(jnp.int32, (SUBLANES, F_CHUNK), 0) == 0

    def conv(t, cols):
        up = _dot(h_ref[slot(t)], wup_ref[:, cols])
        last = up[tm - SUBLANES:tm]
        last2 = up[tm - 2 * SUBLANES:tm - SUBLANES]

        def wrap(cur, prev):
            return jnp.where(first_sublane, pltpu.roll(prev, 1, axis=0), pltpu.roll(cur, 1, axis=0))

        m1 = wrap(last, carry_ref[1, :, cols])
        m2 = wrap(last2, carry_ref[0, :, cols])
        carry_ref[0, :, cols] = last2
        carry_ref[1, :, cols] = last
        up1 = jnp.concatenate([m1, up[:tm - SUBLANES]], axis=0)
        up2 = jnp.concatenate([m2, m1, up[:tm - 2 * SUBLANES]], axis=0)
        return (bconv_ref[:, cols] + wconv_ref[2:3, cols] * up
                + wconv_ref[1:2, cols] * up1 + wconv_ref[0:1, cols] * up2)

    def up_chunk(t, c):
        gcols = slice(c * F_CHUNK, (c + 1) * F_CHUNK)
        gate = conv(t, gcols)
        val = conv(t, slice(D_FF + c * F_CHUNK, D_FF + (c + 1) * F_CHUNK))
        hg = 0.5 * gate
        act_ref[slot(t), :, gcols] = ((hg + hg * jnp.tanh(hg)) * val).astype(BF16)

    def unpermute_store(t, lt, vals):
        for j in range(n_grp):
            skew_ref[slot(t), lt, pl.ds(j, SUBLANES, stride=pitch), :] = vals[SUBLANES * j:SUBLANES * (j + 1), :]

    def down_block(t, j):
        cs = slice(j * MXU_COLS, (j + 1) * MXU_COLS)
        lts = MXU_COLS // LANES
        y = x_cols(t, j * lts, (j + 1) * lts) + _dot(act_ref[slot(t)], wdown_ref[:, cs])
        for ll in range(lts):
            unpermute_store(t, j * lts + ll, y[:, ll * LANES:(ll + 1) * LANES])
        return jnp.sum(y * y, axis=-1, keepdims=True)

    def epilogue(t, ssqs):
        rf = lax.rsqrt(sum(ssqs) * (1.0 / D_MODEL) + EPS)
        unpermute_store(t, n_lane_tiles, jnp.broadcast_to(rf, (tm, LANES)))
        for k in range(SUBLANES):
            blk = slice(pitch * k, pitch * k + n_grp)
            rf_nat = skew_ref[slot(t), n_lane_tiles, blk, :]
            for lt in range(n_lane_tiles):
                cols = slice(lt * LANES, (lt + 1) * LANES)
                o_ref[t * tm + n_grp * k:t * tm + n_grp * (k + 1), cols] = (
                    skew_ref[slot(t), lt, blk, :] * rf_nat * gfin_ref[:, cols])

    down_at = {(k + 1) * n_chunks // (n_down + 1): k for k in range(n_down)}
    assert len(down_at) == n_down
    prologue(0)
    for t in range(n_tiles):
        if t + 1 < n_tiles:
            prologue(t + 1)
        ys = {}
        for c in range(n_chunks):
            up_chunk(t, c)
            if t >= 1 and c in down_at:
                ys[down_at[c]] = down_block(t - 1, down_at[c])
        if t >= 1:
            epilogue(t - 1, [ys[j] for j in range(n_down)])
    last = n_tiles - 1
    epilogue(last, [down_block(last, j) for j in range(n_down)])


def _conv_ffn(x2d, batch, seq, g_ffn, w_up, w_conv, b_conv, w_down, g_final):
    rows = FFN_TILES * TILE_ROWS
    steps_per_seq = seq // rows
    row_spec = pl.BlockSpec((rows, D_MODEL), lambda i: (i, 0))
    skew_rows = TILE_ROWS + SUBLANES * SUBLANES
    return pl.pallas_call(
        functools.partial(_ffn_kernel, steps_per_seq=steps_per_seq),
        grid=(batch * steps_per_seq,),
        in_specs=[
            pl.BlockSpec((FFN_TILES * (D_MODEL // LANES), TILE_ROWS, LANES), lambda i: (i, 0, 0)),
            _resident((1, D_MODEL)),
            _resident((D_MODEL, 2 * D_FF)),
            _resident((CONV_WIDTH, 2 * D_FF)),
            _resident((1, 2 * D_FF)),
            _resident((D_FF, D_MODEL)),
            _resident((1, D_MODEL)),
        ],
        out_specs=row_spec,
        out_shape=jax.ShapeDtypeStruct((x2d.shape[0] // (D_MODEL // LANES) * TILE_ROWS, D_MODEL), F32),
        scratch_shapes=[
            pltpu.VMEM((FFN_SLOTS, D_MODEL // LANES + 1, skew_rows, LANES), F32),
            pltpu.VMEM((FFN_SLOTS, TILE_ROWS, D_MODEL), BF16),
            pltpu.VMEM((CONV_WIDTH - 1, SUBLANES, 2 * D_FF), F32),
            pltpu.VMEM((FFN_SLOTS, TILE_ROWS, D_FF), BF16),
        ],
        compiler_params=pltpu.CompilerParams(
            dimension_semantics=("arbitrary",),
            vmem_limit_bytes=VMEM_LIMIT),
        name="conv_ffn",
    )(x2d, g_ffn, w_up, w_conv, b_conv, w_down, g_final)


def kernel(x, g_mix, w_in, g_sgu, w_s, b_s, sinks, rel_bias, w_pa, w_pb, w_out,
           g_ffn, w_up, w_conv, b_conv, w_down, g_final):
    batch, seq, d = x.shape
    assert w_in.shape[0] == 1 and d == D_MODEL and seq % (max(FFN_TILES, MIX_TILES) * TILE_ROWS) == 0 and w_in.shape[2] == IN_DIM
    bias_tbl = _bias_table(rel_bias)
    x2d = x.reshape(batch * seq, d)
    b_full = jnp.repeat(jnp.transpose(b_s[0]), A_WIDTH // A_GROUPS, axis=1)
    x2d, w_up_bf16, w_down_bf16 = _token_mixer(
        x2d, batch, seq, sinks[0], g_mix[0][None, :], w_in[0].astype(BF16), g_sgu[0][None, :],
        w_s[0], b_full, bias_tbl, w_pa[0].astype(BF16), w_pb[0].astype(BF16), w_out[0].astype(BF16),
        w_up[0], w_down[0])
    x2d = _conv_ffn(
        x2d, batch, seq, g_ffn[0][None, :], w_up_bf16, w_conv[0], b_conv[0][None, :],
        w_down_bf16, g_final[None, :])
    return x2d.reshape(batch, seq, d)
```

```python
import functools
import math

import jax
import jax.numpy as jnp
import numpy as np
from jax import lax
from jax.experimental import pallas as pl
from jax.experimental.pallas import tpu as pltpu

D_MODEL = 1024
CHUNK = 128
A_GROUPS = 4
A_WIDTH = 512
N_HEADS = 8
N_KV_HEADS = 2
HEAD_DIM = 64
Q_DIM = N_HEADS * HEAD_DIM
KV_DIM = N_KV_HEADS * HEAD_DIM
WINDOW = 128
BLOCK = 128
N_BUCKETS = 32
MAX_DISTANCE = 128
D_FF = 2816
CONV_WIDTH = 3
EPS = 1e-6
NEG_INF = -1e30

OFF_PU = 0
OFF_PV = OFF_PU + A_WIDTH
OFF_Q = OFF_PV + A_WIDTH
OFF_K = OFF_Q + Q_DIM
OFF_V = OFF_K + KV_DIM
OFF_GA = OFF_V + KV_DIM
OFF_GB = OFF_GA + D_MODEL
IN_DIM = OFF_GB + D_MODEL

LANES = 128
SUBLANES = 8
MXU_COLS = 256
TILE_ROWS = 512
F_CHUNK = 256
FFN_TILES = 2
FFN_SLOTS = 2
MIX_TILES = 2
VMEM_LIMIT = 56 * 1024 * 1024

F32 = jnp.float32
BF16 = jnp.bfloat16


def _gelu_tanh(x):
    c = math.sqrt(2.0 / math.pi)
    hx = 0.5 * x
    return hx + hx * jnp.tanh(x * (c + (c * 0.044715) * (x * x)))


def _sigmoid(x):
    return 0.5 * (1.0 + jnp.tanh(0.5 * x))


def _dot(a, b):
    return jnp.dot(a, b, preferred_element_type=F32)


def _band_buckets():
    i = np.arange(BLOCK)[:, None]
    j = np.arange(2 * BLOCK)[None, :]
    d = np.clip(i + BLOCK - j, 0, None)
    max_exact = N_BUCKETS // 2
    large = max_exact + (np.log(np.maximum(d, 1) / max_exact) / np.log(MAX_DISTANCE / max_exact)
                         * (N_BUCKETS - max_exact)).astype(np.int32)
    large = np.minimum(large, N_BUCKETS - 1)
    return np.where(d < max_exact, d, large).astype(np.int32)


def _prep_kernel(rb_ref, bucket_ref, *refs):
    n_w = (len(refs) - 1) // 2
    w_refs, o_ref, wb_refs = refs[:n_w], refs[n_w], refs[n_w + 1:]
    h = pl.program_id(0)
    bucket = bucket_ref[...]
    acc = jnp.zeros(bucket.shape, F32)
    for b in range(N_BUCKETS):
        acc = jnp.where(bucket == b, rb_ref[b, h], acc)
    o_ref[...] = acc
    for w_ref, wb_ref in zip(w_refs, wb_refs):
        wb_ref[...] = w_ref[...].astype(BF16)


def _prep(rel_bias, weights):
    buckets = jnp.asarray(_band_buckets())
    slab_specs = []
    for w in weights:
        assert w.shape[0] % (N_HEADS * 2 * SUBLANES) == 0
        slab_specs.append(pl.BlockSpec((w.shape[0] // N_HEADS, w.shape[1]), lambda h: (h, 0)))
    outs = pl.pallas_call(
        _prep_kernel,
        grid=(N_HEADS,),
        in_specs=[
            pl.BlockSpec(memory_space=pltpu.SMEM),
            pl.BlockSpec((BLOCK, 2 * BLOCK), lambda h: (0, 0)),
        ] + slab_specs,
        out_specs=[pl.BlockSpec((None, BLOCK, 2 * BLOCK), lambda h: (h, 0, 0))] + slab_specs,
        out_shape=[jax.ShapeDtypeStruct((N_HEADS, BLOCK, 2 * BLOCK), F32)]
        + [jax.ShapeDtypeStruct(w.shape, BF16) for w in weights],
        name="rel_bias_table",
    )(rel_bias, buckets, *weights)
    return outs[0], outs[1:]


def _mixer_kernel(sinks_ref, x_ref, gmix_ref, win_ref, gsgu_ref, ws_ref, bfull_ref, bias_ref,
                  wpa_ref, wpb_ref, wout_ref, wup_f32_ref, wdown_f32_ref, o_ref, wup_bf16_ref, wdown_bf16_ref,
                  h_ref, u_ref, vn_ref, qlo_ref, qhi_ref, kd_ref, vd_ref, ya_ref, yb_ref, m_ref):
    wup_bf16_ref[...] = wup_f32_ref[...].astype(BF16)
    wdown_bf16_ref[...] = wdown_f32_ref[...].astype(BF16)

    tm = TILE_ROWS
    n_tiles = x_ref.shape[0] // tm
    blocks_per_tile = tm // BLOCK
    s_idx = pl.program_id(1)
    heads_per_kv = N_HEADS // N_KV_HEADS
    n_out = D_MODEL // MXU_COLS

    @pl.when(s_idx == 0)
    def _():
        kd_ref[:, 0:BLOCK, :] = jnp.zeros((N_KV_HEADS, BLOCK, LANES), BF16)
        vd_ref[:, 0:BLOCK, :] = jnp.zeros((N_KV_HEADS, BLOCK, LANES), BF16)

    row_c = lax.broadcasted_iota(jnp.int32, (CHUNK, CHUNK), 0)
    col_c = lax.broadcasted_iota(jnp.int32, (CHUNK, CHUNK), 1)
    tril = col_c <= row_c
    wmask = [jnp.where(tril, ws_ref[g], 0.0).astype(BF16) for g in range(A_GROUPS)]
    row_i = lax.broadcasted_iota(jnp.int32, (BLOCK, 2 * BLOCK), 0)
    col_j = lax.broadcasted_iota(jnp.int32, (BLOCK, 2 * BLOCK), 1)
    band = (col_j > row_i) & (col_j <= row_i + BLOCK)
    band_first = band & ((col_j >= BLOCK) | (s_idx > 0))
    b_is_lo = lax.broadcasted_iota(jnp.int32, (BLOCK, LANES), 1) < HEAD_DIM
    q_is_lo = (lax.broadcasted_iota(jnp.int32, (tm, Q_DIM), 1) % LANES) < HEAD_DIM
    t_is_lo = lax.broadcasted_iota(jnp.int32, (tm, LANES), 1) < HEAD_DIM

    def norm(t):
        x = x_ref[t * tm:(t + 1) * tm, :]
        r = lax.rsqrt(jnp.mean(x * x, axis=-1, keepdims=True) + EPS)
        h_ref[t] = (x * r * gmix_ref[...]).astype(BF16)

    def proj_u(t):
        u_ref[t] = _gelu_tanh(_dot(h_ref[t], win_ref[:, OFF_PU:OFF_PU + A_WIDTH]))

    def proj_v(t):
        pv = _gelu_tanh(_dot(h_ref[t], win_ref[:, OFF_PV:OFF_PV + A_WIDTH]))
        rv = lax.rsqrt(jnp.mean(pv * pv, axis=-1, keepdims=True) + EPS)
        vn_ref[t] = (pv * rv * gsgu_ref[...]).astype(BF16)

    def proj_q(t):
        q = _dot(h_ref[t], win_ref[:, OFF_Q:OFF_Q + Q_DIM])
        qlo_ref[t] = jnp.where(q_is_lo, q, 0.0).astype(BF16)
        qhi_ref[t] = jnp.where(q_is_lo, 0.0, q).astype(BF16)

    def proj_kv(t):
        kvp = _dot(h_ref[t], win_ref[:, OFF_K:OFF_K + 2 * KV_DIM])
        k = kvp[:, :KV_DIM] * (HEAD_DIM ** -0.5)
        v = kvp[:, KV_DIM:]
        k_sw = pltpu.roll(k, HEAD_DIM, axis=1)
        v_sw = pltpu.roll(v, HEAD_DIM, axis=1)
        rows = slice(BLOCK + t * tm, BLOCK + (t + 1) * tm)
        kd_ref[0, rows, :] = jnp.where(t_is_lo, k, k_sw).astype(BF16)
        kd_ref[1, rows, :] = jnp.where(t_is_lo, k_sw, k).astype(BF16)
        vd_ref[0, rows, :] = jnp.where(t_is_lo, v, v_sw).astype(BF16)
        vd_ref[1, rows, :] = jnp.where(t_is_lo, v_sw, v).astype(BF16)

    def sgu(t, g):
        cols = slice(g * CHUNK, (g + 1) * CHUNK)
        v_chunks = [vn_ref[t, b * CHUNK:(b + 1) * CHUNK, cols] for b in range(tm // CHUNK)]
        s_all = _dot(wmask[g], jnp.concatenate(v_chunks, axis=1))
        for b in range(tm // CHUNK):
            rows = slice(b * CHUNK, (b + 1) * CHUNK)
            s = s_all[:, b * CHUNK:(b + 1) * CHUNK] + bfull_ref[:, cols]
            ya_ref[t, rows, cols] = (u_ref[t, rows, cols] * s).astype(BF16)

    probs = {}

    def attn_scores(t, b, kv):
        rows = slice(b * BLOCK, (b + 1) * BLOCK)
        bb = t * blocks_per_tile + b
        mask = band_first if bb == 0 else band
        q_parts = []
        for g in range(heads_per_kv):
            hh = kv * heads_per_kv + g
            src = qlo_ref if hh % 2 == 0 else qhi_ref
            c = hh // 2
            q_parts.append(src[t, rows, c * LANES:(c + 1) * LANES])
        q_stack = jnp.concatenate(q_parts, axis=0)
        keys = kd_ref[kv, bb * BLOCK:(bb + 2) * BLOCK, :]
        sc = lax.dot_general(q_stack, keys, (((1,), (1,)), ((), ())),
                             preferred_element_type=F32)
        p_parts = []
        inv_l = []
        for g in range(heads_per_kv):
            hh = kv * heads_per_kv + g
            sg = sc[g * BLOCK:(g + 1) * BLOCK] + bias_ref[hh]
            sg = jnp.where(mask, sg, NEG_INF)
            sink = sinks_ref[hh]
            m = jnp.maximum(jnp.max(sg, axis=-1, keepdims=True), sink)
            p = jnp.exp(sg - m)
            l = jnp.sum(p, axis=-1, keepdims=True) + jnp.exp(sink - m)
            p_parts.append(p.astype(BF16))
            inv_l.append(1.0 / l)
        probs[t, b, kv] = (jnp.concatenate(p_parts, axis=0), inv_l)

    def attn_values(t, b, kv):
        rows = slice(b * BLOCK, (b + 1) * BLOCK)
        bb = t * blocks_per_tile + b
        p_stack, inv_l = probs.pop((t, b, kv))
        vals = vd_ref[kv, bb * BLOCK:(bb + 2) * BLOCK, :]
        o = _dot(p_stack, vals)
        for cc in range(heads_per_kv // 2):
            o_even = o[(2 * cc) * BLOCK:(2 * cc + 1) * BLOCK] * inv_l[2 * cc]
            o_odd = o[(2 * cc + 1) * BLOCK:(2 * cc + 2) * BLOCK] * inv_l[2 * cc + 1]
            col = kv * (heads_per_kv // 2) + cc
            yb_ref[t, rows, col * LANES:(col + 1) * LANES] = jnp.where(b_is_lo, o_even, o_odd).astype(BF16)

    def merge(t, c):
        cs = slice(c * MXU_COLS, (c + 1) * MXU_COLS)
        ga = _dot(h_ref[t], win_ref[:, OFF_GA + c * MXU_COLS:OFF_GA + (c + 1) * MXU_COLS])
        gb = _dot(h_ref[t], win_ref[:, OFF_GB + c * MXU_COLS:OFF_GB + (c + 1) * MXU_COLS])
        pa = _dot(ya_ref[t], wpa_ref[:, cs])
        pb = _dot(yb_ref[t], wpb_ref[:, cs])
        m_ref[t, :, cs] = (_sigmoid(ga) * pa + _sigmoid(gb) * pb).astype(BF16)

    def out_proj(t, c):
        cs = slice(c * MXU_COLS, (c + 1) * MXU_COLS)
        val = x_ref[t * tm:(t + 1) * tm, cs] + _dot(m_ref[t], wout_ref[:, cs])
        n_grp = tm // SUBLANES
        n_lt = D_MODEL // LANES
        for ll in range(MXU_COLS // LANES):
            lt = c * (MXU_COLS // LANES) + ll
            for g in range(n_grp):
                k, j0 = divmod(SUBLANES * g, n_grp)
                o_ref[t * n_lt + lt, pl.ds(SUBLANES * j0 + k, SUBLANES, stride=SUBLANES), :] = (
                    val[SUBLANES * g:SUBLANES * (g + 1), ll * LANES:(ll + 1) * LANES])

    P = functools.partial

    def stage1(t):
        return [P(proj_u, t), P(proj_v, t), P(proj_q, t), P(proj_kv, t)]

    def stage2(t):
        sgu_items = [P(sgu, t, g) for g in range(A_GROUPS)]
        def attn(b, kv):
            attn_scores(t, b, kv)
            attn_values(t, b, kv)

        attn_items = [P(attn, b, kv) for b in range(blocks_per_tile) for kv in range(N_KV_HEADS)]
        items = []
        for i, item in enumerate(attn_items):
            items += sgu_items[i * A_GROUPS // len(attn_items):(i + 1) * A_GROUPS // len(attn_items)] + [item]
        return items

    def stage3(t):
        return [P(merge, t, c) for c in range(n_out)]

    def stage4(t):
        return [P(out_proj, t, c) for c in range(n_out)]

    def interleave(heavy, light):
        n, m = len(heavy), len(light)
        done = 0
        for i, item in enumerate(heavy):
            item()
            upto = (i + 1) * m // n
            for light_item in light[done:upto]:
                light_item()
            done = upto

    def run(items):
        for item in items:
            item()

    norm(0)
    run(stage1(0))
    for t in range(n_tiles):
        heavy = []
        if t >= 2:
            heavy += stage4(t - 2)
        if t >= 1:
            heavy += stage3(t - 1)
        if t + 1 < n_tiles:
            norm(t + 1)
            heavy += stage1(t + 1)
        interleave(heavy, stage2(t)) if heavy else run(stage2(t))
    if n_tiles >= 2:
        run(stage4(n_tiles - 2))
    run(stage3(n_tiles - 1))
    run(stage4(n_tiles - 1))

    rows_all = n_tiles * tm
    kd_ref[:, 0:BLOCK, :] = kd_ref[:, rows_all:rows_all + BLOCK, :]
    vd_ref[:, 0:BLOCK, :] = vd_ref[:, rows_all:rows_all + BLOCK, :]


def _resident(shape):
    nd = len(shape)
    return pl.BlockSpec(shape, lambda *_: (0,) * nd, pipeline_mode=pl.Buffered(1))


def _token_mixer(x2d, batch, seq, sinks, g_mix, w_in, g_sgu, w_s, b_full, bias_tbl, w_pa, w_pb, w_out,
                 w_up_f32, w_down_f32):
    tm = TILE_ROWS
    rows = MIX_TILES * tm
    ns = seq // rows
    n_steps = batch * ns
    n_lt = D_MODEL // LANES
    row_spec = pl.BlockSpec((rows, D_MODEL), lambda b, s: (b * ns + s, 0))
    up_slab = D_MODEL // n_steps
    down_slab = 16 * 11
    n_down_slabs = D_FF // down_slab
    assert D_MODEL % n_steps == 0 and up_slab % 16 == 0 and D_FF % down_slab == 0 and n_down_slabs <= n_steps
    up_spec = pl.BlockSpec((up_slab, 2 * D_FF), lambda b, s: (b * ns + s, 0))
    down_spec = pl.BlockSpec((down_slab, D_MODEL), lambda b, s: (jnp.minimum(b * ns + s, n_down_slabs - 1), 0))
    return pl.pallas_call(
        _mixer_kernel,
        grid=(batch, ns),
        in_specs=[
            pl.BlockSpec(memory_space=pltpu.SMEM),
            row_spec,
            _resident((1, D_MODEL)),
            _resident((D_MODEL, IN_DIM)),
            _resident((1, A_WIDTH)),
            _resident((A_GROUPS, CHUNK, CHUNK)),
            _resident((CHUNK, A_WIDTH)),
            _resident((N_HEADS, BLOCK, 2 * BLOCK)),
            _resident((A_WIDTH, D_MODEL)),
            _resident((Q_DIM, D_MODEL)),
            _resident((D_MODEL, D_MODEL)),
            up_spec,
            down_spec,
        ],
        out_specs=[pl.BlockSpec((MIX_TILES * n_lt, tm, LANES), lambda b, s: (b * ns + s, 0, 0)), up_spec, down_spec],
        out_shape=[jax.ShapeDtypeStruct((x2d.shape[0] // tm * n_lt, tm, LANES), F32),
                   jax.ShapeDtypeStruct(w_up_f32.shape, BF16),
                   jax.ShapeDtypeStruct(w_down_f32.shape, BF16)],
        scratch_shapes=[
            pltpu.VMEM((MIX_TILES, tm, D_MODEL), BF16),
            pltpu.VMEM((MIX_TILES, tm, A_WIDTH), F32),
            pltpu.VMEM((MIX_TILES, tm, A_WIDTH), BF16),
            pltpu.VMEM((MIX_TILES, tm, Q_DIM), BF16),
            pltpu.VMEM((MIX_TILES, tm, Q_DIM), BF16),
            pltpu.VMEM((N_KV_HEADS, BLOCK + rows, LANES), BF16),
            pltpu.VMEM((N_KV_HEADS, BLOCK + rows, LANES), BF16),
            pltpu.VMEM((MIX_TILES, tm, A_WIDTH), BF16),
            pltpu.VMEM((MIX_TILES, tm, Q_DIM), BF16),
            pltpu.VMEM((MIX_TILES, tm, D_MODEL), BF16),
        ],
        compiler_params=pltpu.CompilerParams(
            dimension_semantics=("arbitrary", "arbitrary"),
            vmem_limit_bytes=VMEM_LIMIT),
        name="token_mixer",
    )(sinks, x2d, g_mix, w_in, g_sgu, w_s, b_full, bias_tbl, w_pa, w_pb, w_out, w_up_f32, w_down_f32)


def _ffn_kernel(x_ref, gffn_ref, wup_ref, wconv_ref, bconv_ref, wdown_ref, gfin_ref, o_ref,
                skew_ref, h_ref, carry_ref, act_ref, *, steps_per_seq):
    tm = TILE_ROWS
    n_tiles = o_ref.shape[0] // tm
    n_grp = tm // SUBLANES
    pitch = n_grp + SUBLANES
    n_lane_tiles = D_MODEL // LANES
    n_chunks = D_FF // F_CHUNK
    n_down = D_MODEL // MXU_COLS

    @pl.when(pl.program_id(0) % steps_per_seq == 0)
    def _():
        carry_ref[...] = jnp.zeros(carry_ref.shape, F32)

    def slot(t):
        return t % FFN_SLOTS

    def x_cols(t, lt0, lt1):
        return jnp.concatenate([x_ref[t * n_lane_tiles + lt] for lt in range(lt0, lt1)], axis=1)

    def prologue(t):
        x = x_cols(t, 0, n_lane_tiles)
        r = lax.rsqrt(jnp.mean(x * x, axis=-1, keepdims=True) + EPS)
        h_ref[slot(t)] = (x * r * gffn_ref[...]).astype(BF16)

    first_sublane = lax.broadcasted_iota(jnp.int32, (SUBLANES, F_CHUNK), 0) == 0

    def conv(t, cols):
        up = _dot(h_ref[slot(t)], wup_ref[:, cols])
        last = up[tm - SUBLANES:tm]
        last2 = up[tm - 2 * SUBLANES:tm - SUBLANES]

        def wrap(cur, prev):
            return jnp.where(first_sublane, pltpu.roll(prev, 1, axis=0), pltpu.roll(cur, 1, axis=0))

        m1 = wrap(last, carry_ref[1, :, cols])
        m2 = wrap(last2, carry_ref[0, :, cols])
        carry_ref[0, :, cols] = last2
        carry_ref[1, :, cols] = last
        up1 = jnp.concatenate([m1, up[:tm - SUBLANES]], axis=0)
        up2 = jnp.concatenate([m2, m1, up[:tm - 2 * SUBLANES]], axis=0)
        return (bconv_ref[:, cols] + wconv_ref[2:3, cols] * up
                + wconv_ref[1:2, cols] * up1 + wconv_ref[0:1, cols] * up2)

    def up_chunk(t, c):
        gcols = slice(c * F_CHUNK, (c + 1) * F_CHUNK)
        gate = conv(t, gcols)
        val = conv(t, slice(D_FF + c * F_CHUNK, D_FF + (c + 1) * F_CHUNK))
        hg = 0.5 * gate
        act_ref[slot(t), :, gcols] = ((hg + hg * jnp.tanh(hg)) * val).astype(BF16)

    def unpermute_store(t, lt, vals):
        for j in range(n_grp):
            skew_ref[slot(t), lt, pl.ds(j, SUBLANES, stride=pitch), :] = vals[SUBLANES * j:SUBLANES * (j + 1), :]

    def down_block(t, j):
        cs = slice(j * MXU_COLS, (j + 1) * MXU_COLS)
        lts = MXU_COLS // LANES
        y = x_cols(t, j * lts, (j + 1) * lts) + _dot(act_ref[slot(t)], wdown_ref[:, cs])
        for ll in range(lts):
            unpermute_store(t, j * lts + ll, y[:, ll * LANES:(ll + 1) * LANES])
        return jnp.sum(y * y, axis=-1, keepdims=True)

    def epilogue(t, ssqs):
        rf = lax.rsqrt(sum(ssqs) * (1.0 / D_MODEL) + EPS)
        unpermute_store(t, n_lane_tiles, jnp.broadcast_to(rf, (tm, LANES)))
        for k in range(SUBLANES):
            blk = slice(pitch * k, pitch * k + n_grp)
            rf_nat = skew_ref[slot(t), n_lane_tiles, blk, :]
            for lt in range(n_lane_tiles):
                cols = slice(lt * LANES, (lt + 1) * LANES)
                o_ref[t * tm + n_grp * k:t * tm + n_grp * (k + 1), cols] = (
                    skew_ref[slot(t), lt, blk, :] * rf_nat * gfin_ref[:, cols])

    down_at = {(k + 1) * n_chunks // (n_down + 1): k for k in range(n_down)}
    assert len(down_at) == n_down
    prologue(0)
    for t in range(n_tiles):
        if t + 1 < n_tiles:
            prologue(t + 1)
        ys = {}
        for c in range(n_chunks):
            up_chunk(t, c)
            if t >= 1 and c in down_at:
                ys[down_at[c]] = down_block(t - 1, down_at[c])
        if t >= 1:
            epilogue(t - 1, [ys[j] for j in range(n_down)])
    last = n_tiles - 1
    epilogue(last, [down_block(last, j) for j in range(n_down)])


def _conv_ffn(x2d, batch, seq, g_ffn, w_up, w_conv, b_conv, w_down, g_final):
    rows = FFN_TILES * TILE_ROWS
    steps_per_seq = seq // rows
    row_spec = pl.BlockSpec((rows, D_MODEL), lambda i: (i, 0))
    skew_rows = TILE_ROWS + SUBLANES * SUBLANES
    return pl.pallas_call(
        functools.partial(_ffn_kernel, steps_per_seq=steps_per_seq),
        grid=(batch * steps_per_seq,),
        in_specs=[
            pl.BlockSpec((FFN_TILES * (D_MODEL // LANES), TILE_ROWS, LANES), lambda i: (i, 0, 0)),
            _resident((1, D_MODEL)),
            _resident((D_MODEL, 2 * D_FF)),
            _resident((CONV_WIDTH, 2 * D_FF)),
            _resident((1, 2 * D_FF)),
            _resident((D_FF, D_MODEL)),
            _resident((1, D_MODEL)),
        ],
        out_specs=row_spec,
        out_shape=jax.ShapeDtypeStruct((x2d.shape[0] // (D_MODEL // LANES) * TILE_ROWS, D_MODEL), F32),
        scratch_shapes=[
            pltpu.VMEM((FFN_SLOTS, D_MODEL // LANES + 1, skew_rows, LANES), F32),
            pltpu.VMEM((FFN_SLOTS, TILE_ROWS, D_MODEL), BF16),
            pltpu.VMEM((CONV_WIDTH - 1, SUBLANES, 2 * D_FF), F32),
            pltpu.VMEM((FFN_SLOTS, TILE_ROWS, D_FF), BF16),
        ],
        compiler_params=pltpu.CompilerParams(
            dimension_semantics=("arbitrary",),
            vmem_limit_bytes=VMEM_LIMIT),
        name="conv_ffn",
    )(x2d, g_ffn, w_up, w_conv, b_conv, w_down, g_final)


def kernel(x, g_mix, w_in, g_sgu, w_s, b_s, sinks, rel_bias, w_pa, w_pb, w_out,
           g_ffn, w_up, w_conv, b_conv, w_down, g_final):
    batch, seq, d = x.shape
    assert w_in.shape[0] == 1 and d == D_MODEL and seq % (max(FFN_TILES, MIX_TILES) * TILE_ROWS) == 0 and w_in.shape[2] == IN_DIM
    bias_tbl, (w_in_bf16, w_pa_bf16, w_pb_bf16, w_out_bf16) = _prep(
        rel_bias, (w_in[0], w_pa[0], w_pb[0], w_out[0]))
    x2d = x.reshape(batch * seq, d)
    b_full = jnp.repeat(jnp.transpose(b_s[0]), A_WIDTH // A_GROUPS, axis=1)
    x2d, w_up_bf16, w_down_bf16 = _token_mixer(
        x2d, batch, seq, sinks[0], g_mix[0][None, :], w_in_bf16, g_sgu[0][None, :],
        w_s[0], b_full, bias_tbl, w_pa_bf16, w_pb_bf16, w_out_bf16,
        w_up[0], w_down[0])
    x2d = _conv_ffn(
        x2d, batch, seq, g_ffn[0][None, :], w_up_bf16, w_conv[0], b_conv[0][None, :],
        w_down_bf16, g_final[None, :])
    return x2d.reshape(batch, seq, d)
```

```python
import functools
import math

import jax
import jax.numpy as jnp
import numpy as np
from jax import lax
from jax.experimental import pallas as pl
from jax.experimental.pallas import tpu as pltpu

D_MODEL = 1024
CHUNK = 128
A_GROUPS = 4
A_WIDTH = 512
N_HEADS = 8
N_KV_HEADS = 2
HEAD_DIM = 64
Q_DIM = N_HEADS * HEAD_DIM
KV_DIM = N_KV_HEADS * HEAD_DIM
WINDOW = 128
BLOCK = 128
N_BUCKETS = 32
MAX_DISTANCE = 128
D_FF = 2816
CONV_WIDTH = 3
EPS = 1e-6
NEG_INF = -1e30

OFF_PU = 0
OFF_PV = OFF_PU + A_WIDTH
OFF_Q = OFF_PV + A_WIDTH
OFF_K = OFF_Q + Q_DIM
OFF_V = OFF_K + KV_DIM
OFF_GA = OFF_V + KV_DIM
OFF_GB = OFF_GA + D_MODEL
IN_DIM = OFF_GB + D_MODEL

LANES = 128
SUBLANES = 8
MXU_COLS = 256
TILE_ROWS = 512
F_CHUNK = 256
FFN_TILES = 2
FFN_SLOTS = 2
MIX_TILES = 2
VMEM_BYTES = 64 * 1024 * 1024
VMEM_LIMIT = VMEM_BYTES - 8 * 1024 * 1024

F32 = jnp.float32
BF16 = jnp.bfloat16


def _gelu_tanh(x):
    c = math.sqrt(2.0 / math.pi)
    hx = 0.5 * x
    return hx + hx * jnp.tanh(x * (c + (c * 0.044715) * (x * x)))


def _sigmoid(x):
    return 0.5 * (1.0 + jnp.tanh(0.5 * x))


def _dot(a, b):
    return jnp.dot(a, b, preferred_element_type=F32)


def _band_buckets():
    i = np.arange(BLOCK)[:, None]
    j = np.arange(2 * BLOCK)[None, :]
    d = np.clip(i + BLOCK - j, 0, None)
    max_exact = N_BUCKETS // 2
    large = max_exact + (np.log(np.maximum(d, 1) / max_exact) / np.log(MAX_DISTANCE / max_exact)
                         * (N_BUCKETS - max_exact)).astype(np.int32)
    large = np.minimum(large, N_BUCKETS - 1)
    return np.where(d < max_exact, d, large).astype(np.int32)


def _prep_kernel(rb_ref, bucket_ref, *refs):
    n_w = (len(refs) - 1) // 2
    w_refs, o_ref, wb_refs = refs[:n_w], refs[n_w], refs[n_w + 1:]
    h = pl.program_id(0)
    bucket = bucket_ref[...]
    acc = jnp.zeros(bucket.shape, F32)
    for b in range(N_BUCKETS):
        acc = jnp.where(bucket == b, rb_ref[b, h], acc)
    o_ref[...] = acc
    for w_ref, wb_ref in zip(w_refs, wb_refs):
        wb_ref[...] = w_ref[...].astype(BF16)


def _prep(rel_bias, weights):
    buckets = jnp.asarray(_band_buckets())
    slab_specs = []
    for w in weights:
        assert w.shape[0] % (N_HEADS * 2 * SUBLANES) == 0
        slab_specs.append(pl.BlockSpec((w.shape[0] // N_HEADS, w.shape[1]), lambda h: (h, 0)))
    outs = pl.pallas_call(
        _prep_kernel,
        grid=(N_HEADS,),
        in_specs=[
            pl.BlockSpec(memory_space=pltpu.SMEM),
            pl.BlockSpec((BLOCK, 2 * BLOCK), lambda h: (0, 0)),
        ] + slab_specs,
        out_specs=[pl.BlockSpec((None, BLOCK, 2 * BLOCK), lambda h: (h, 0, 0))] + slab_specs,
        out_shape=[jax.ShapeDtypeStruct((N_HEADS, BLOCK, 2 * BLOCK), F32)]
        + [jax.ShapeDtypeStruct(w.shape, BF16) for w in weights],
        name="rel_bias_table",
    )(rel_bias, buckets, *weights)
    return outs[0], outs[1:]


def _mixer_kernel(sinks_ref, x_ref, gmix_ref, win_ref, gsgu_ref, ws_ref, bfull_ref, bias_ref,
                  wpa_ref, wpb_ref, wout_ref, wup_f32_ref, wdown_f32_ref, o_ref, wup_bf16_ref, wdown_bf16_ref,
                  h_ref, u_ref, vn_ref, qlo_ref, qhi_ref, kd_ref, vd_ref, ya_ref, yb_ref, m_ref):
    wup_bf16_ref[...] = wup_f32_ref[...].astype(BF16)
    wdown_bf16_ref[...] = wdown_f32_ref[...].astype(BF16)

    tm = TILE_ROWS
    n_tiles = x_ref.shape[0] // tm
    blocks_per_tile = tm // BLOCK
    s_idx = pl.program_id(1)
    heads_per_kv = N_HEADS // N_KV_HEADS
    n_out = D_MODEL // MXU_COLS

    @pl.when(s_idx == 0)
    def _():
        kd_ref[:, 0:BLOCK, :] = jnp.zeros((N_KV_HEADS, BLOCK, LANES), BF16)
        vd_ref[:, 0:BLOCK, :] = jnp.zeros((N_KV_HEADS, BLOCK, LANES), BF16)

    row_c = lax.broadcasted_iota(jnp.int32, (CHUNK, CHUNK), 0)
    col_c = lax.broadcasted_iota(jnp.int32, (CHUNK, CHUNK), 1)
    tril = col_c <= row_c
    wmask = [jnp.where(tril, ws_ref[g], 0.0).astype(BF16) for g in range(A_GROUPS)]
    row_i = lax.broadcasted_iota(jnp.int32, (BLOCK, 2 * BLOCK), 0)
    col_j = lax.broadcasted_iota(jnp.int32, (BLOCK, 2 * BLOCK), 1)
    band = (col_j > row_i) & (col_j <= row_i + BLOCK)
    band_first = band & ((col_j >= BLOCK) | (s_idx > 0))
    b_is_lo = lax.broadcasted_iota(jnp.int32, (BLOCK, LANES), 1) < HEAD_DIM
    q_is_lo = (lax.broadcasted_iota(jnp.int32, (tm, Q_DIM), 1) % LANES) < HEAD_DIM
    t_is_lo = lax.broadcasted_iota(jnp.int32, (tm, LANES), 1) < HEAD_DIM

    def norm(t):
        x = x_ref[t * tm:(t + 1) * tm, :]
        r = lax.rsqrt(jnp.mean(x * x, axis=-1, keepdims=True) + EPS)
        h_ref[t] = (x * r * gmix_ref[...]).astype(BF16)

    def proj_u(t):
        u_ref[t] = _gelu_tanh(_dot(h_ref[t], win_ref[:, OFF_PU:OFF_PU + A_WIDTH]))

    def proj_v(t):
        pv = _gelu_tanh(_dot(h_ref[t], win_ref[:, OFF_PV:OFF_PV + A_WIDTH]))
        rv = lax.rsqrt(jnp.mean(pv * pv, axis=-1, keepdims=True) + EPS)
        vn_ref[t] = (pv * rv * gsgu_ref[...]).astype(BF16)

    def proj_q(t):
        q = _dot(h_ref[t], win_ref[:, OFF_Q:OFF_Q + Q_DIM])
        qlo_ref[t] = jnp.where(q_is_lo, q, 0.0).astype(BF16)
        qhi_ref[t] = jnp.where(q_is_lo, 0.0, q).astype(BF16)

    def proj_kv(t):
        kvp = _dot(h_ref[t], win_ref[:, OFF_K:OFF_K + 2 * KV_DIM])
        k = kvp[:, :KV_DIM] * (HEAD_DIM ** -0.5)
        v = kvp[:, KV_DIM:]
        k_sw = pltpu.roll(k, HEAD_DIM, axis=1)
        v_sw = pltpu.roll(v, HEAD_DIM, axis=1)
        rows = slice(BLOCK + t * tm, BLOCK + (t + 1) * tm)
        kd_ref[0, rows, :] = jnp.where(t_is_lo, k, k_sw).astype(BF16)
        kd_ref[1, rows, :] = jnp.where(t_is_lo, k_sw, k).astype(BF16)
        vd_ref[0, rows, :] = jnp.where(t_is_lo, v, v_sw).astype(BF16)
        vd_ref[1, rows, :] = jnp.where(t_is_lo, v_sw, v).astype(BF16)

    def sgu(t, g):
        cols = slice(g * CHUNK, (g + 1) * CHUNK)
        v_chunks = [vn_ref[t, b * CHUNK:(b + 1) * CHUNK, cols] for b in range(tm // CHUNK)]
        s_all = _dot(wmask[g], jnp.concatenate(v_chunks, axis=1))
        for b in range(tm // CHUNK):
            rows = slice(b * CHUNK, (b + 1) * CHUNK)
            s = s_all[:, b * CHUNK:(b + 1) * CHUNK] + bfull_ref[:, cols]
            ya_ref[t, rows, cols] = (u_ref[t, rows, cols] * s).astype(BF16)

    probs = {}

    def attn_scores(t, b, kv):
        rows = slice(b * BLOCK, (b + 1) * BLOCK)
        bb = t * blocks_per_tile + b
        mask = band_first if bb == 0 else band
        q_parts = []
        for g in range(heads_per_kv):
            hh = kv * heads_per_kv + g
            src = qlo_ref if hh % 2 == 0 else qhi_ref
            c = hh // 2
            q_parts.append(src[t, rows, c * LANES:(c + 1) * LANES])
        q_stack = jnp.concatenate(q_parts, axis=0)
        keys = kd_ref[kv, bb * BLOCK:(bb + 2) * BLOCK, :]
        sc = lax.dot_general(q_stack, keys, (((1,), (1,)), ((), ())),
                             preferred_element_type=F32)
        p_parts = []
        inv_l = []
        for g in range(heads_per_kv):
            hh = kv * heads_per_kv + g
            sg = sc[g * BLOCK:(g + 1) * BLOCK] + bias_ref[hh]
            sg = jnp.where(mask, sg, NEG_INF)
            sink = sinks_ref[hh]
            m = jnp.maximum(jnp.max(sg, axis=-1, keepdims=True), sink)
            p = jnp.exp(sg - m)
            l = jnp.sum(p, axis=-1, keepdims=True) + jnp.exp(sink - m)
            p_parts.append(p.astype(BF16))
            inv_l.append(1.0 / l)
        probs[t, b, kv] = (jnp.concatenate(p_parts, axis=0), inv_l)

    def attn_values(t, b, kv):
        rows = slice(b * BLOCK, (b + 1) * BLOCK)
        bb = t * blocks_per_tile + b
        p_stack, inv_l = probs.pop((t, b, kv))
        vals = vd_ref[kv, bb * BLOCK:(bb + 2) * BLOCK, :]
        o = _dot(p_stack, vals)
        for cc in range(heads_per_kv // 2):
            o_even = o[(2 * cc) * BLOCK:(2 * cc + 1) * BLOCK] * inv_l[2 * cc]
            o_odd = o[(2 * cc + 1) * BLOCK:(2 * cc + 2) * BLOCK] * inv_l[2 * cc + 1]
            col = kv * (heads_per_kv // 2) + cc
            yb_ref[t, rows, col * LANES:(col + 1) * LANES] = jnp.where(b_is_lo, o_even, o_odd).astype(BF16)

    def merge(t, c):
        cs = slice(c * MXU_COLS, (c + 1) * MXU_COLS)
        ga = _dot(h_ref[t], win_ref[:, OFF_GA + c * MXU_COLS:OFF_GA + (c + 1) * MXU_COLS])
        gb = _dot(h_ref[t], win_ref[:, OFF_GB + c * MXU_COLS:OFF_GB + (c + 1) * MXU_COLS])
        pa = _dot(ya_ref[t], wpa_ref[:, cs])
        pb = _dot(yb_ref[t], wpb_ref[:, cs])
        m_ref[t, :, cs] = (_sigmoid(ga) * pa + _sigmoid(gb) * pb).astype(BF16)

    def out_proj(t, c):
        cs = slice(c * MXU_COLS, (c + 1) * MXU_COLS)
        val = x_ref[t * tm:(t + 1) * tm, cs] + _dot(m_ref[t], wout_ref[:, cs])
        n_grp = tm // SUBLANES
        n_lt = D_MODEL // LANES
        for ll in range(MXU_COLS // LANES):
            lt = c * (MXU_COLS // LANES) + ll
            for g in range(n_grp):
                k, j0 = divmod(SUBLANES * g, n_grp)
                o_ref[t * n_lt + lt, pl.ds(SUBLANES * j0 + k, SUBLANES, stride=SUBLANES), :] = (
                    val[SUBLANES * g:SUBLANES * (g + 1), ll * LANES:(ll + 1) * LANES])

    P = functools.partial

    def stage1(t):
        return [P(proj_u, t), P(proj_v, t), P(proj_q, t), P(proj_kv, t)]

    def stage2(t):
        sgu_items = [P(sgu, t, g) for g in range(A_GROUPS)]
        def attn(b, kv):
            attn_scores(t, b, kv)
            attn_values(t, b, kv)

        attn_items = [P(attn, b, kv) for b in range(blocks_per_tile) for kv in range(N_KV_HEADS)]
        items = []
        for i, item in enumerate(attn_items):
            items += sgu_items[i * A_GROUPS // len(attn_items):(i + 1) * A_GROUPS // len(attn_items)] + [item]
        return items

    def stage3(t):
        return [P(merge, t, c) for c in range(n_out)]

    def stage4(t):
        return [P(out_proj, t, c) for c in range(n_out)]

    def interleave(heavy, light):
        n, m = len(heavy), len(light)
        done = 0
        for i, item in enumerate(heavy):
            item()
            upto = (i + 1) * m // n
            for light_item in light[done:upto]:
                light_item()
            done = upto

    def run(items):
        for item in items:
            item()

    norm(0)
    run(stage1(0))
    for t in range(n_tiles):
        heavy = []
        if t >= 2:
            heavy += stage4(t - 2)
        if t >= 1:
            heavy += stage3(t - 1)
        if t + 1 < n_tiles:
            norm(t + 1)
            heavy += stage1(t + 1)
        interleave(heavy, stage2(t)) if heavy else run(stage2(t))
    if n_tiles >= 2:
        run(stage4(n_tiles - 2))
    run(stage3(n_tiles - 1))
    run(stage4(n_tiles - 1))

    rows_all = n_tiles * tm
    kd_ref[:, 0:BLOCK, :] = kd_ref[:, rows_all:rows_all + BLOCK, :]
    vd_ref[:, 0:BLOCK, :] = vd_ref[:, rows_all:rows_all + BLOCK, :]


def _resident(shape):
    nd = len(shape)
    return pl.BlockSpec(shape, lambda *_: (0,) * nd, pipeline_mode=pl.Buffered(1))


def _token_mixer(x2d, batch, seq, sinks, g_mix, w_in, g_sgu, w_s, b_full, bias_tbl, w_pa, w_pb, w_out,
                 w_up_f32, w_down_f32):
    tm = TILE_ROWS
    rows = MIX_TILES * tm
    ns = seq // rows
    n_steps = batch * ns
    n_lt = D_MODEL // LANES
    row_spec = pl.BlockSpec((rows, D_MODEL), lambda b, s: (b * ns + s, 0))
    up_slab = D_MODEL // n_steps
    down_slab = 16 * 11
    n_down_slabs = D_FF // down_slab
    assert D_MODEL % n_steps == 0 and up_slab % 16 == 0 and D_FF % down_slab == 0 and n_down_slabs <= n_steps
    up_spec = pl.BlockSpec((up_slab, 2 * D_FF), lambda b, s: (b * ns + s, 0))
    down_spec = pl.BlockSpec((down_slab, D_MODEL), lambda b, s: (jnp.minimum(b * ns + s, n_down_slabs - 1), 0))
    return pl.pallas_call(
        _mixer_kernel,
        grid=(batch, ns),
        in_specs=[
            pl.BlockSpec(memory_space=pltpu.SMEM),
            row_spec,
            _resident((1, D_MODEL)),
            _resident((D_MODEL, IN_DIM)),
            _resident((1, A_WIDTH)),
            _resident((A_GROUPS, CHUNK, CHUNK)),
            _resident((CHUNK, A_WIDTH)),
            _resident((N_HEADS, BLOCK, 2 * BLOCK)),
            _resident((A_WIDTH, D_MODEL)),
            _resident((Q_DIM, D_MODEL)),
            _resident((D_MODEL, D_MODEL)),
            up_spec,
            down_spec,
        ],
        out_specs=[pl.BlockSpec((MIX_TILES * n_lt, tm, LANES), lambda b, s: (b * ns + s, 0, 0)), up_spec, down_spec],
        out_shape=[jax.ShapeDtypeStruct((x2d.shape[0] // tm * n_lt, tm, LANES), F32),
                   jax.ShapeDtypeStruct(w_up_f32.shape, BF16),
                   jax.ShapeDtypeStruct(w_down_f32.shape, BF16)],
        scratch_shapes=[
            pltpu.VMEM((MIX_TILES, tm, D_MODEL), BF16),
            pltpu.VMEM((MIX_TILES, tm, A_WIDTH), F32),
            pltpu.VMEM((MIX_TILES, tm, A_WIDTH), BF16),
            pltpu.VMEM((MIX_TILES, tm, Q_DIM), BF16),
            pltpu.VMEM((MIX_TILES, tm, Q_DIM), BF16),
            pltpu.VMEM((N_KV_HEADS, BLOCK + rows, LANES), BF16),
            pltpu.VMEM((N_KV_HEADS, BLOCK + rows, LANES), BF16),
            pltpu.VMEM((MIX_TILES, tm, A_WIDTH), BF16),
            pltpu.VMEM((MIX_TILES, tm, Q_DIM), BF16),
            pltpu.VMEM((MIX_TILES, tm, D_MODEL), BF16),
        ],
        compiler_params=pltpu.CompilerParams(
            dimension_semantics=("arbitrary", "arbitrary"),
            vmem_limit_bytes=VMEM_LIMIT),
        name="token_mixer",
    )(sinks, x2d, g_mix, w_in, g_sgu, w_s, b_full, bias_tbl, w_pa, w_pb, w_out, w_up_f32, w_down_f32)


def _ffn_kernel(x_ref, gffn_ref, wup_ref, wconv_ref, bconv_ref, wdown_ref, gfin_ref, o_ref,
                skew_ref, h_ref, carry_ref, act_ref, *, steps_per_seq):
    tm = TILE_ROWS
    n_tiles = o_ref.shape[0] // tm
    n_grp = tm // SUBLANES
    pitch = n_grp + SUBLANES
    n_lane_tiles = D_MODEL // LANES
    n_chunks = D_FF // F_CHUNK
    n_down = D_MODEL // MXU_COLS

    @pl.when(pl.program_id(0) % steps_per_seq == 0)
    def _():
        carry_ref[...] = jnp.zeros(carry_ref.shape, F32)

    def slot(t):
        return t % FFN_SLOTS

    def x_cols(t, lt0, lt1):
        return jnp.concatenate([x_ref[t * n_lane_tiles + lt] for lt in range(lt0, lt1)], axis=1)

    def prologue(t):
        x = x_cols(t, 0, n_lane_tiles)
        r = lax.rsqrt(jnp.mean(x * x, axis=-1, keepdims=True) + EPS)
        h_ref[slot(t)] = (x * r * gffn_ref[...]).astype(BF16)

    first_sublane = lax.broadcasted_iota(jnp.int32, (SUBLANES, F_CHUNK), 0) == 0

    def conv(t, cols):
        up = _dot(h_ref[slot(t)], wup_ref[:, cols])
        last = up[tm - SUBLANES:tm]
        last2 = up[tm - 2 * SUBLANES:tm - SUBLANES]

        def wrap(cur, prev):
            return jnp.where(first_sublane, pltpu.roll(prev, 1, axis=0), pltpu.roll(cur, 1, axis=0))

        m1 = wrap(last, carry_ref[1, :, cols])
        m2 = wrap(last2, carry_ref[0, :, cols])
        carry_ref[0, :, cols] = last2
        carry_ref[1, :, cols] = last
        up1 = jnp.concatenate([m1, up[:tm - SUBLANES]], axis=0)
        up2 = jnp.concatenate([m2, m1, up[:tm - 2 * SUBLANES]], axis=0)
        return (bconv_ref[:, cols] + wconv_ref[2:3, cols] * up
                + wconv_ref[1:2, cols] * up1 + wconv_ref[0:1, cols] * up2)

    def up_chunk(t, c):
        gcols = slice(c * F_CHUNK, (c + 1) * F_CHUNK)
        gate = conv(t, gcols)
        val = conv(t, slice(D_FF + c * F_CHUNK, D_FF + (c + 1) * F_CHUNK))
        hg = 0.5 * gate
        act_ref[slot(t), :, gcols] = ((hg + hg * jnp.tanh(hg)) * val).astype(BF16)

    def unpermute_store(t, lt, vals):
        for j in range(n_grp):
            skew_ref[slot(t), lt, pl.ds(j, SUBLANES, stride=pitch), :] = vals[SUBLANES * j:SUBLANES * (j + 1), :]

    def down_block(t, j):
        cs = slice(j * MXU_COLS, (j + 1) * MXU_COLS)
        lts = MXU_COLS // LANES
        y = x_cols(t, j * lts, (j + 1) * lts) + _dot(act_ref[slot(t)], wdown_ref[:, cs])
        for ll in range(lts):
            unpermute_store(t, j * lts + ll, y[:, ll * LANES:(ll + 1) * LANES])
        return jnp.sum(y * y, axis=-1, keepdims=True)

    def epilogue(t, ssqs):
        rf = lax.rsqrt(sum(ssqs) * (1.0 / D_MODEL) + EPS)
        unpermute_store(t, n_lane_tiles, jnp.broadcast_to(rf, (tm, LANES)))
        for k in range(SUBLANES):
            blk = slice(pitch * k, pitch * k + n_grp)
            rf_nat = skew_ref[slot(t), n_lane_tiles, blk, :]
            for lt in range(n_lane_tiles):
                cols = slice(lt * LANES, (lt + 1) * LANES)
                o_ref[t * tm + n_grp * k:t * tm + n_grp * (k + 1), cols] = (
                    skew_ref[slot(t), lt, blk, :] * rf_nat * gfin_ref[:, cols])

    down_before = {k * n_chunks // n_down: k for k in range(n_down)}
    assert len(down_before) == n_down
    prologue(0)
    for t in range(n_tiles):
        if t + 1 < n_tiles:
            prologue(t + 1)
        ys = {}
        for c in range(n_chunks):
            if t >= 1 and c in down_before:
                ys[down_before[c]] = down_block(t - 1, down_before[c])
            up_chunk(t, c)
        if t >= 1:
            epilogue(t - 1, [ys[j] for j in range(n_down)])
    last = n_tiles - 1
    epilogue(last, [down_block(last, j) for j in range(n_down)])


def _conv_ffn(x2d, batch, seq, g_ffn, w_up, w_conv, b_conv, w_down, g_final):
    rows = FFN_TILES * TILE_ROWS
    steps_per_seq = seq // rows
    row_spec = pl.BlockSpec((rows, D_MODEL), lambda i: (i, 0))
    skew_rows = TILE_ROWS + SUBLANES * SUBLANES
    return pl.pallas_call(
        functools.partial(_ffn_kernel, steps_per_seq=steps_per_seq),
        grid=(batch * steps_per_seq,),
        in_specs=[
            pl.BlockSpec((FFN_TILES * (D_MODEL // LANES), TILE_ROWS, LANES), lambda i: (i, 0, 0)),
            _resident((1, D_MODEL)),
            _resident((D_MODEL, 2 * D_FF)),
            _resident((CONV_WIDTH, 2 * D_FF)),
            _resident((1, 2 * D_FF)),
            _resident((D_FF, D_MODEL)),
            _resident((1, D_MODEL)),
        ],
        out_specs=row_spec,
        out_shape=jax.ShapeDtypeStruct((x2d.shape[0] // (D_MODEL // LANES) * TILE_ROWS, D_MODEL), F32),
        scratch_shapes=[
            pltpu.VMEM((FFN_SLOTS, D_MODEL // LANES + 1, skew_rows, LANES), F32),
            pltpu.VMEM((FFN_SLOTS, TILE_ROWS, D_MODEL), BF16),
            pltpu.VMEM((CONV_WIDTH - 1, SUBLANES, 2 * D_FF), F32),
            pltpu.VMEM((FFN_SLOTS, TILE_ROWS, D_FF), BF16),
        ],
        compiler_params=pltpu.CompilerParams(
            dimension_semantics=("arbitrary",),
            vmem_limit_bytes=VMEM_LIMIT),
        name="conv_ffn",
    )(x2d, g_ffn, w_up, w_conv, b_conv, w_down, g_final)


def kernel(x, g_mix, w_in, g_sgu, w_s, b_s, sinks, rel_bias, w_pa, w_pb, w_out,
           g_ffn, w_up, w_conv, b_conv, w_down, g_final):
    batch, seq, d = x.shape
    assert w_in.shape[0] == 1 and d == D_MODEL and seq % (max(FFN_TILES, MIX_TILES) * TILE_ROWS) == 0 and w_in.shape[2] == IN_DIM
    bias_tbl, (w_in_bf16, w_pa_bf16, w_pb_bf16, w_out_bf16) = _prep(
        rel_bias, (w_in[0], w_pa[0], w_pb[0], w_out[0]))
    x2d = x.reshape(batch * seq, d)
    b_full = jnp.repeat(jnp.transpose(b_s[0]), A_WIDTH // A_GROUPS, axis=1)
    x2d, w_up_bf16, w_down_bf16 = _token_mixer(
        x2d, batch, seq, sinks[0], g_mix[0][None, :], w_in_bf16, g_sgu[0][None, :],
        w_s[0], b_full, bias_tbl, w_pa_bf16, w_pb_bf16, w_out_bf16,
        w_up[0], w_down[0])
    x2d = _conv_ffn(
        x2d, batch, seq, g_ffn[0][None, :], w_up_bf16, w_conv[0], b_conv[0][None, :],
        w_down_bf16, g_final[None, :])
    return x2d.reshape(batch, seq, d)
```

```python
import functools
import math

import jax
import jax.numpy as jnp
import numpy as np
from jax import lax
from jax.experimental import pallas as pl
from jax.experimental.pallas import tpu as pltpu

D_MODEL = 1024
CHUNK = 128
A_GROUPS = 4
A_WIDTH = 512
N_HEADS = 8
N_KV_HEADS = 2
HEAD_DIM = 64
Q_DIM = N_HEADS * HEAD_DIM
KV_DIM = N_KV_HEADS * HEAD_DIM
WINDOW = 128
BLOCK = 128
N_BUCKETS = 32
MAX_DISTANCE = 128
D_FF = 2816
CONV_WIDTH = 3
EPS = 1e-6
NEG_INF = -1e30
LOG2E = math.log2(math.e)

OFF_PU = 0
OFF_PV = OFF_PU + A_WIDTH
OFF_Q = OFF_PV + A_WIDTH
OFF_K = OFF_Q + Q_DIM
OFF_V = OFF_K + KV_DIM
OFF_GA = OFF_V + KV_DIM
OFF_GB = OFF_GA + D_MODEL
IN_DIM = OFF_GB + D_MODEL

LANES = 128
SUBLANES = 8
MXU_COLS = 256
TILE_ROWS = 512
F_CHUNK = 256
FFN_TILES = 2
FFN_SLOTS = 2
MIX_TILES = 2
VMEM_BYTES = 64 * 1024 * 1024
VMEM_LIMIT = VMEM_BYTES - 8 * 1024 * 1024

F32 = jnp.float32
BF16 = jnp.bfloat16


def _gelu_tanh(x):
    c = math.sqrt(2.0 / math.pi)
    hx = 0.5 * x
    return hx + hx * jnp.tanh(x * (c + (c * 0.044715) * (x * x)))


def _sigmoid(x):
    return 1.0 / (1.0 + jnp.exp(-x))


def _dot(a, b):
    return jnp.dot(a, b, preferred_element_type=F32)


def _band_buckets():
    i = np.arange(BLOCK)[:, None]
    j = np.arange(2 * BLOCK)[None, :]
    d = np.clip(i + BLOCK - j, 0, None)
    max_exact = N_BUCKETS // 2
    large = max_exact + (np.log(np.maximum(d, 1) / max_exact) / np.log(MAX_DISTANCE / max_exact)
                         * (N_BUCKETS - max_exact)).astype(np.int32)
    large = np.minimum(large, N_BUCKETS - 1)
    return np.where(d < max_exact, d, large).astype(np.int32)


def _prep_kernel(rb_ref, bucket_ref, *refs):
    n_w = (len(refs) - 1) // 2
    w_refs, o_ref, wb_refs = refs[:n_w], refs[n_w], refs[n_w + 1:]
    h = pl.program_id(0)
    bucket = bucket_ref[...]
    acc = jnp.zeros(bucket.shape, F32)
    for b in range(N_BUCKETS):
        acc = jnp.where(bucket == b, rb_ref[b, h], acc)
    o_ref[...] = acc * LOG2E
    for w_ref, wb_ref in zip(w_refs, wb_refs):
        wb_ref[...] = w_ref[...].astype(BF16)


def _prep(rel_bias, weights):
    buckets = jnp.asarray(_band_buckets())
    slab_specs = []
    for w in weights:
        assert w.shape[0] % (N_HEADS * 2 * SUBLANES) == 0
        slab_specs.append(pl.BlockSpec((w.shape[0] // N_HEADS, w.shape[1]), lambda h: (h, 0)))
    outs = pl.pallas_call(
        _prep_kernel,
        grid=(N_HEADS,),
        in_specs=[
            pl.BlockSpec(memory_space=pltpu.SMEM),
            pl.BlockSpec((BLOCK, 2 * BLOCK), lambda h: (0, 0)),
        ] + slab_specs,
        out_specs=[pl.BlockSpec((None, BLOCK, 2 * BLOCK), lambda h: (h, 0, 0))] + slab_specs,
        out_shape=[jax.ShapeDtypeStruct((N_HEADS, BLOCK, 2 * BLOCK), F32)]
        + [jax.ShapeDtypeStruct(w.shape, BF16) for w in weights],
        name="rel_bias_table",
    )(rel_bias, buckets, *weights)
    return outs[0], outs[1:]


def _mixer_kernel(sinks_ref, x_ref, gmix_ref, win_ref, gsgu_ref, ws_ref, bfull_ref, bias_ref,
                  wpa_ref, wpb_ref, wout_ref, wup_f32_ref, wdown_f32_ref, o_ref, wup_bf16_ref, wdown_bf16_ref,
                  h_ref, u_ref, vn_ref, qlo_ref, qhi_ref, kd_ref, vd_ref, ya_ref, yb_ref, m_ref):
    wup_bf16_ref[...] = wup_f32_ref[...].astype(BF16)
    wdown_bf16_ref[...] = wdown_f32_ref[...].astype(BF16)

    tm = TILE_ROWS
    n_tiles = x_ref.shape[0] // tm
    blocks_per_tile = tm // BLOCK
    s_idx = pl.program_id(1)
    heads_per_kv = N_HEADS // N_KV_HEADS
    n_out = D_MODEL // MXU_COLS

    @pl.when(s_idx == 0)
    def _():
        kd_ref[:, 0:BLOCK, :] = jnp.zeros((N_KV_HEADS, BLOCK, LANES), BF16)
        vd_ref[:, 0:BLOCK, :] = jnp.zeros((N_KV_HEADS, BLOCK, LANES), BF16)

    row_c = lax.broadcasted_iota(jnp.int32, (CHUNK, CHUNK), 0)
    col_c = lax.broadcasted_iota(jnp.int32, (CHUNK, CHUNK), 1)
    tril = col_c <= row_c
    wmask = [jnp.where(tril, ws_ref[g], 0.0).astype(BF16) for g in range(A_GROUPS)]
    row_i = lax.broadcasted_iota(jnp.int32, (BLOCK, 2 * BLOCK), 0)
    col_j = lax.broadcasted_iota(jnp.int32, (BLOCK, 2 * BLOCK), 1)
    band = (col_j > row_i) & (col_j <= row_i + BLOCK)
    band_first = band & ((col_j >= BLOCK) | (s_idx > 0))
    b_is_lo = lax.broadcasted_iota(jnp.int32, (BLOCK, LANES), 1) < HEAD_DIM
    q_is_lo = (lax.broadcasted_iota(jnp.int32, (tm, Q_DIM), 1) % LANES) < HEAD_DIM
    t_is_lo = lax.broadcasted_iota(jnp.int32, (tm, LANES), 1) < HEAD_DIM

    def norm(t):
        x = x_ref[t * tm:(t + 1) * tm, :]
        r = lax.rsqrt(jnp.mean(x * x, axis=-1, keepdims=True) + EPS)
        h_ref[t] = (x * r * gmix_ref[...]).astype(BF16)

    def proj_u(t):
        u_ref[t] = _gelu_tanh(_dot(h_ref[t], win_ref[:, OFF_PU:OFF_PU + A_WIDTH]))

    def proj_v(t):
        pv = _gelu_tanh(_dot(h_ref[t], win_ref[:, OFF_PV:OFF_PV + A_WIDTH]))
        rv = lax.rsqrt(jnp.mean(pv * pv, axis=-1, keepdims=True) + EPS)
        vn_ref[t] = (pv * rv * gsgu_ref[...]).astype(BF16)

    def proj_q(t):
        q = _dot(h_ref[t], win_ref[:, OFF_Q:OFF_Q + Q_DIM])
        qlo_ref[t] = jnp.where(q_is_lo, q, 0.0).astype(BF16)
        qhi_ref[t] = jnp.where(q_is_lo, 0.0, q).astype(BF16)

    def proj_kv(t):
        kvp = _dot(h_ref[t], win_ref[:, OFF_K:OFF_K + 2 * KV_DIM])
        k = kvp[:, :KV_DIM] * (HEAD_DIM ** -0.5 * LOG2E)
        v = kvp[:, KV_DIM:]
        k_sw = pltpu.roll(k, HEAD_DIM, axis=1)
        v_sw = pltpu.roll(v, HEAD_DIM, axis=1)
        rows = slice(BLOCK + t * tm, BLOCK + (t + 1) * tm)
        kd_ref[0, rows, :] = jnp.where(t_is_lo, k, k_sw).astype(BF16)
        kd_ref[1, rows, :] = jnp.where(t_is_lo, k_sw, k).astype(BF16)
        vd_ref[0, rows, :] = jnp.where(t_is_lo, v, v_sw).astype(BF16)
        vd_ref[1, rows, :] = jnp.where(t_is_lo, v_sw, v).astype(BF16)

    def sgu(t, g):
        cols = slice(g * CHUNK, (g + 1) * CHUNK)
        v_chunks = [vn_ref[t, b * CHUNK:(b + 1) * CHUNK, cols] for b in range(tm // CHUNK)]
        s_all = _dot(wmask[g], jnp.concatenate(v_chunks, axis=1))
        for b in range(tm // CHUNK):
            rows = slice(b * CHUNK, (b + 1) * CHUNK)
            s = s_all[:, b * CHUNK:(b + 1) * CHUNK] + bfull_ref[:, cols]
            ya_ref[t, rows, cols] = (u_ref[t, rows, cols] * s).astype(BF16)

    probs = {}

    def attn_scores(t, b, kv):
        rows = slice(b * BLOCK, (b + 1) * BLOCK)
        bb = t * blocks_per_tile + b
        mask = band_first if bb == 0 else band
        q_parts = []
        for g in range(heads_per_kv):
            hh = kv * heads_per_kv + g
            src = qlo_ref if hh % 2 == 0 else qhi_ref
            c = hh // 2
            q_parts.append(src[t, rows, c * LANES:(c + 1) * LANES])
        q_stack = jnp.concatenate(q_parts, axis=0)
        keys = kd_ref[kv, bb * BLOCK:(bb + 2) * BLOCK, :]
        sc = lax.dot_general(q_stack, keys, (((1,), (1,)), ((), ())),
                             preferred_element_type=F32)
        p_parts = []
        inv_l = []
        for g in range(heads_per_kv):
            hh = kv * heads_per_kv + g
            sg = sc[g * BLOCK:(g + 1) * BLOCK] + bias_ref[hh]
            sg = jnp.where(mask, sg, NEG_INF)
            sink = sinks_ref[hh] * LOG2E
            m = jnp.maximum(jnp.max(sg, axis=-1, keepdims=True), sink)
            p = jnp.exp2(sg - m)
            l = jnp.sum(p, axis=-1, keepdims=True) + jnp.exp2(sink - m)
            p_parts.append(p.astype(BF16))
            inv_l.append(1.0 / l)
        probs[t, b, kv] = (jnp.concatenate(p_parts, axis=0), inv_l)

    def attn_values(t, b, kv):
        rows = slice(b * BLOCK, (b + 1) * BLOCK)
        bb = t * blocks_per_tile + b
        p_stack, inv_l = probs.pop((t, b, kv))
        vals = vd_ref[kv, bb * BLOCK:(bb + 2) * BLOCK, :]
        o = _dot(p_stack, vals)
        for cc in range(heads_per_kv // 2):
            o_even = o[(2 * cc) * BLOCK:(2 * cc + 1) * BLOCK] * inv_l[2 * cc]
            o_odd = o[(2 * cc + 1) * BLOCK:(2 * cc + 2) * BLOCK] * inv_l[2 * cc + 1]
            col = kv * (heads_per_kv // 2) + cc
            yb_ref[t, rows, col * LANES:(col + 1) * LANES] = jnp.where(b_is_lo, o_even, o_odd).astype(BF16)

    def merge(t, c):
        cs = slice(c * MXU_COLS, (c + 1) * MXU_COLS)
        ga = _dot(h_ref[t], win_ref[:, OFF_GA + c * MXU_COLS:OFF_GA + (c + 1) * MXU_COLS])
        gb = _dot(h_ref[t], win_ref[:, OFF_GB + c * MXU_COLS:OFF_GB + (c + 1) * MXU_COLS])
        pa = _dot(ya_ref[t], wpa_ref[:, cs])
        pb = _dot(yb_ref[t], wpb_ref[:, cs])
        m_ref[t, :, cs] = (_sigmoid(ga) * pa + _sigmoid(gb) * pb).astype(BF16)

    def out_proj(t, c):
        cs = slice(c * MXU_COLS, (c + 1) * MXU_COLS)
        val = x_ref[t * tm:(t + 1) * tm, cs] + _dot(m_ref[t], wout_ref[:, cs])
        n_grp = tm // SUBLANES
        n_lt = D_MODEL // LANES
        for ll in range(MXU_COLS // LANES):
            lt = c * (MXU_COLS // LANES) + ll
            for g in range(n_grp):
                k, j0 = divmod(SUBLANES * g, n_grp)
                o_ref[t * n_lt + lt, pl.ds(SUBLANES * j0 + k, SUBLANES, stride=SUBLANES), :] = (
                    val[SUBLANES * g:SUBLANES * (g + 1), ll * LANES:(ll + 1) * LANES])

    P = functools.partial

    def stage1(t):
        return [P(proj_u, t), P(proj_v, t), P(proj_q, t), P(proj_kv, t)]

    def stage2(t):
        sgu_items = [P(sgu, t, g) for g in range(A_GROUPS)]
        def attn(b, kv):
            attn_scores(t, b, kv)
            attn_values(t, b, kv)

        attn_items = [P(attn, b, kv) for b in range(blocks_per_tile) for kv in range(N_KV_HEADS)]
        items = []
        for i, item in enumerate(attn_items):
            items += sgu_items[i * A_GROUPS // len(attn_items):(i + 1) * A_GROUPS // len(attn_items)] + [item]
        return items

    def stage3(t):
        return [P(merge, t, c) for c in range(n_out)]

    def stage4(t):
        return [P(out_proj, t, c) for c in range(n_out)]

    def interleave(heavy, light):
        n, m = len(heavy), len(light)
        done = 0
        for i, item in enumerate(heavy):
            item()
            upto = (i + 1) * m // n
            for light_item in light[done:upto]:
                light_item()
            done = upto

    def run(items):
        for item in items:
            item()

    norm(0)
    run(stage1(0))
    for t in range(n_tiles):
        heavy = []
        if t >= 2:
            heavy += stage4(t - 2)
        if t >= 1:
            heavy += stage3(t - 1)
        if t + 1 < n_tiles:
            norm(t + 1)
            heavy += stage1(t + 1)
        interleave(heavy, stage2(t)) if heavy else run(stage2(t))
    if n_tiles >= 2:
        run(stage4(n_tiles - 2))
    run(stage3(n_tiles - 1))
    run(stage4(n_tiles - 1))

    rows_all = n_tiles * tm
    kd_ref[:, 0:BLOCK, :] = kd_ref[:, rows_all:rows_all + BLOCK, :]
    vd_ref[:, 0:BLOCK, :] = vd_ref[:, rows_all:rows_all + BLOCK, :]


def _resident(shape):
    nd = len(shape)
    return pl.BlockSpec(shape, lambda *_: (0,) * nd, pipeline_mode=pl.Buffered(1))


def _token_mixer(x2d, batch, seq, sinks, g_mix, w_in, g_sgu, w_s, b_full, bias_tbl, w_pa, w_pb, w_out,
                 w_up_f32, w_down_f32):
    tm = TILE_ROWS
    rows = MIX_TILES * tm
    ns = seq // rows
    n_steps = batch * ns
    n_lt = D_MODEL // LANES
    row_spec = pl.BlockSpec((rows, D_MODEL), lambda b, s: (b * ns + s, 0))
    up_slab = D_MODEL // n_steps
    down_slab = 16 * 11
    n_down_slabs = D_FF // down_slab
    assert D_MODEL % n_steps == 0 and up_slab % 16 == 0 and D_FF % down_slab == 0 and n_down_slabs <= n_steps
    up_spec = pl.BlockSpec((up_slab, 2 * D_FF), lambda b, s: (b * ns + s, 0))
    down_spec = pl.BlockSpec((down_slab, D_MODEL), lambda b, s: (jnp.minimum(b * ns + s, n_down_slabs - 1), 0))
    return pl.pallas_call(
        _mixer_kernel,
        grid=(batch, ns),
        in_specs=[
            pl.BlockSpec(memory_space=pltpu.SMEM),
            row_spec,
            _resident((1, D_MODEL)),
            _resident((D_MODEL, IN_DIM)),
            _resident((1, A_WIDTH)),
            _resident((A_GROUPS, CHUNK, CHUNK)),
            _resident((CHUNK, A_WIDTH)),
            _resident((N_HEADS, BLOCK, 2 * BLOCK)),
            _resident((A_WIDTH, D_MODEL)),
            _resident((Q_DIM, D_MODEL)),
            _resident((D_MODEL, D_MODEL)),
            up_spec,
            down_spec,
        ],
        out_specs=[pl.BlockSpec((MIX_TILES * n_lt, tm, LANES), lambda b, s: (b * ns + s, 0, 0)), up_spec, down_spec],
        out_shape=[jax.ShapeDtypeStruct((x2d.shape[0] // tm * n_lt, tm, LANES), F32),
                   jax.ShapeDtypeStruct(w_up_f32.shape, BF16),
                   jax.ShapeDtypeStruct(w_down_f32.shape, BF16)],
        scratch_shapes=[
            pltpu.VMEM((MIX_TILES, tm, D_MODEL), BF16),
            pltpu.VMEM((MIX_TILES, tm, A_WIDTH), F32),
            pltpu.VMEM((MIX_TILES, tm, A_WIDTH), BF16),
            pltpu.VMEM((MIX_TILES, tm, Q_DIM), BF16),
            pltpu.VMEM((MIX_TILES, tm, Q_DIM), BF16),
            pltpu.VMEM((N_KV_HEADS, BLOCK + rows, LANES), BF16),
            pltpu.VMEM((N_KV_HEADS, BLOCK + rows, LANES), BF16),
            pltpu.VMEM((MIX_TILES, tm, A_WIDTH), BF16),
            pltpu.VMEM((MIX_TILES, tm, Q_DIM), BF16),
            pltpu.VMEM((MIX_TILES, tm, D_MODEL), BF16),
        ],
        compiler_params=pltpu.CompilerParams(
            dimension_semantics=("arbitrary", "arbitrary"),
            vmem_limit_bytes=VMEM_LIMIT),
        name="token_mixer",
    )(sinks, x2d, g_mix, w_in, g_sgu, w_s, b_full, bias_tbl, w_pa, w_pb, w_out, w_up_f32, w_down_f32)


def _ffn_kernel(x_ref, gffn_ref, wup_ref, wconv_ref, bconv_ref, wdown_ref, gfin_ref, o_ref,
                skew_ref, h_ref, carry_ref, act_ref, *, steps_per_seq):
    tm = TILE_ROWS
    n_tiles = o_ref.shape[0] // tm
    n_grp = tm // SUBLANES
    pitch = n_grp + SUBLANES
    n_lane_tiles = D_MODEL // LANES
    n_chunks = D_FF // F_CHUNK
    n_down = D_MODEL // MXU_COLS

    @pl.when(pl.program_id(0) % steps_per_seq == 0)
    def _():
        carry_ref[...] = jnp.zeros(carry_ref.shape, F32)

    def slot(t):
        return t % FFN_SLOTS

    def x_cols(t, lt0, lt1):
        return jnp.concatenate([x_ref[t * n_lane_tiles + lt] for lt in range(lt0, lt1)], axis=1)

    def prologue(t):
        x = x_cols(t, 0, n_lane_tiles)
        r = lax.rsqrt(jnp.mean(x * x, axis=-1, keepdims=True) + EPS)
        h_ref[slot(t)] = (x * r * gffn_ref[...]).astype(BF16)

    first_sublane = lax.broadcasted_iota(jnp.int32, (SUBLANES, F_CHUNK), 0) == 0

    def conv(t, cols):
        up = _dot(h_ref[slot(t)], wup_ref[:, cols])
        last = up[tm - SUBLANES:tm]
        last2 = up[tm - 2 * SUBLANES:tm - SUBLANES]

        def wrap(cur, prev):
            return jnp.where(first_sublane, pltpu.roll(prev, 1, axis=0), pltpu.roll(cur, 1, axis=0))

        m1 = wrap(last, carry_ref[1, :, cols])
        m2 = wrap(last2, carry_ref[0, :, cols])
        carry_ref[0, :, cols] = last2
        carry_ref[1, :, cols] = last
        up1 = jnp.concatenate([m1, up[:tm - SUBLANES]], axis=0)
        up2 = jnp.concatenate([m2, m1, up[:tm - 2 * SUBLANES]], axis=0)
        return (bconv_ref[:, cols] + wconv_ref[2:3, cols] * up
                + wconv_ref[1:2, cols] * up1 + wconv_ref[0:1, cols] * up2)

    def up_chunk(t, c):
        gcols = slice(c * F_CHUNK, (c + 1) * F_CHUNK)
        gate = conv(t, gcols)
        val = conv(t, slice(D_FF + c * F_CHUNK, D_FF + (c + 1) * F_CHUNK))
        hg = 0.5 * gate
        act_ref[slot(t), :, gcols] = ((hg + hg * jnp.tanh(hg)) * val).astype(BF16)

    def unpermute_store(t, lt, vals):
        for j in range(n_grp):
            skew_ref[slot(t), lt, pl.ds(j, SUBLANES, stride=pitch), :] = vals[SUBLANES * j:SUBLANES * (j + 1), :]

    def down_block(t, j):
        cs = slice(j * MXU_COLS, (j + 1) * MXU_COLS)
        lts = MXU_COLS // LANES
        y = x_cols(t, j * lts, (j + 1) * lts) + _dot(act_ref[slot(t)], wdown_ref[:, cs])
        for ll in range(lts):
            unpermute_store(t, j * lts + ll, y[:, ll * LANES:(ll + 1) * LANES])
        return jnp.sum(y * y, axis=-1, keepdims=True)

    def epilogue(t, ssqs):
        rf = lax.rsqrt(sum(ssqs) * (1.0 / D_MODEL) + EPS)
        unpermute_store(t, n_lane_tiles, jnp.broadcast_to(rf, (tm, LANES)))
        for k in range(SUBLANES):
            blk = slice(pitch * k, pitch * k + n_grp)
            rf_nat = skew_ref[slot(t), n_lane_tiles, blk, :]
            for lt in range(n_lane_tiles):
                cols = slice(lt * LANES, (lt + 1) * LANES)
                o_ref[t * tm + n_grp * k:t * tm + n_grp * (k + 1), cols] = (
                    skew_ref[slot(t), lt, blk, :] * rf_nat * gfin_ref[:, cols])

    down_before = {k * n_chunks // n_down: k for k in range(n_down)}
    assert len(down_before) == n_down
    prologue(0)
    for t in range(n_tiles):
        if t + 1 < n_tiles:
            prologue(t + 1)
        ys = {}
        for c in range(n_chunks):
            if t >= 1 and c in down_before:
                ys[down_before[c]] = down_block(t - 1, down_before[c])
            up_chunk(t, c)
        if t >= 1:
            epilogue(t - 1, [ys[j] for j in range(n_down)])
    last = n_tiles - 1
    epilogue(last, [down_block(last, j) for j in range(n_down)])


def _conv_ffn(x2d, batch, seq, g_ffn, w_up, w_conv, b_conv, w_down, g_final):
    rows = FFN_TILES * TILE_ROWS
    steps_per_seq = seq // rows
    row_spec = pl.BlockSpec((rows, D_MODEL), lambda i: (i, 0))
    skew_rows = TILE_ROWS + SUBLANES * SUBLANES
    return pl.pallas_call(
        functools.partial(_ffn_kernel, steps_per_seq=steps_per_seq),
        grid=(batch * steps_per_seq,),
        in_specs=[
            pl.BlockSpec((FFN_TILES * (D_MODEL // LANES), TILE_ROWS, LANES), lambda i: (i, 0, 0)),
            _resident((1, D_MODEL)),
            _resident((D_MODEL, 2 * D_FF)),
            _resident((CONV_WIDTH, 2 * D_FF)),
            _resident((1, 2 * D_FF)),
            _resident((D_FF, D_MODEL)),
            _resident((1, D_MODEL)),
        ],
        out_specs=row_spec,
        out_shape=jax.ShapeDtypeStruct((x2d.shape[0] // (D_MODEL // LANES) * TILE_ROWS, D_MODEL), F32),
        scratch_shapes=[
            pltpu.VMEM((FFN_SLOTS, D_MODEL // LANES + 1, skew_rows, LANES), F32),
            pltpu.VMEM((FFN_SLOTS, TILE_ROWS, D_MODEL), BF16),
            pltpu.VMEM((CONV_WIDTH - 1, SUBLANES, 2 * D_FF), F32),
            pltpu.VMEM((FFN_SLOTS, TILE_ROWS, D_FF), BF16),
        ],
        compiler_params=pltpu.CompilerParams(
            dimension_semantics=("arbitrary",),
            vmem_limit_bytes=VMEM_LIMIT),
        name="conv_ffn",
    )(x2d, g_ffn, w_up, w_conv, b_conv, w_down, g_final)


def kernel(x, g_mix, w_in, g_sgu, w_s, b_s, sinks, rel_bias, w_pa, w_pb, w_out,
           g_ffn, w_up, w_conv, b_conv, w_down, g_final):
    batch, seq, d = x.shape
    assert w_in.shape[0] == 1 and d == D_MODEL and seq % (max(FFN_TILES, MIX_TILES) * TILE_ROWS) == 0 and w_in.shape[2] == IN_DIM
    bias_tbl, (w_in_bf16, w_pa_bf16, w_pb_bf16, w_out_bf16) = _prep(
        rel_bias, (w_in[0], w_pa[0], w_pb[0], w_out[0]))
    x2d = x.reshape(batch * seq, d)
    b_full = jnp.repeat(jnp.transpose(b_s[0]), A_WIDTH // A_GROUPS, axis=1)
    x2d, w_up_bf16, w_down_bf16 = _token_mixer(
        x2d, batch, seq, sinks[0], g_mix[0][None, :], w_in_bf16, g_sgu[0][None, :],
        w_s[0], b_full, bias_tbl, w_pa_bf16, w_pb_bf16, w_out_bf16,
        w_up[0], w_down[0])
    x2d = _conv_ffn(
        x2d, batch, seq, g_ffn[0][None, :], w_up_bf16, w_conv[0], b_conv[0][None, :],
        w_down_bf16, g_final[None, :])
    return x2d.reshape(batch, seq, d)
```

```python
import functools
import math

import jax
import jax.numpy as jnp
import numpy as np
from jax import lax
from jax.experimental import pallas as pl
from jax.experimental.pallas import tpu as pltpu

D_MODEL = 1024
CHUNK = 128
A_GROUPS = 4
A_WIDTH = 512
N_HEADS = 8
N_KV_HEADS = 2
HEAD_DIM = 64
Q_DIM = N_HEADS * HEAD_DIM
KV_DIM = N_KV_HEADS * HEAD_DIM
WINDOW = 128
BLOCK = 128
N_BUCKETS = 32
MAX_DISTANCE = 128
D_FF = 2816
CONV_WIDTH = 3
EPS = 1e-6
NEG_INF = -1e30
LOG2E = math.log2(math.e)

OFF_PU = 0
OFF_PV = OFF_PU + A_WIDTH
OFF_Q = OFF_PV + A_WIDTH
OFF_K = OFF_Q + Q_DIM
OFF_V = OFF_K + KV_DIM
OFF_GA = OFF_V + KV_DIM
OFF_GB = OFF_GA + D_MODEL
IN_DIM = OFF_GB + D_MODEL

LANES = 128
SUBLANES = 8
MXU_COLS = 256
TILE_ROWS = 512
F_CHUNK = 256
FFN_TILES = 2
FFN_SLOTS = 2
MIX_TILES = 2
VMEM_BYTES = 64 * 1024 * 1024
VMEM_LIMIT = VMEM_BYTES - 8 * 1024 * 1024
BF16_ROWS = 2 * SUBLANES

F32 = jnp.float32
BF16 = jnp.bfloat16


def _gelu_tanh(x):
    c = math.sqrt(2.0 / math.pi)
    hx = 0.5 * x
    return hx + hx * jnp.tanh(x * (c + (c * 0.044715) * (x * x)))


def _sigmoid(x):
    return 1.0 / (1.0 + jnp.exp(-x))


def _dot(a, b):
    return jnp.dot(a, b, preferred_element_type=F32)


def _band_buckets():
    i = np.arange(BLOCK)[:, None]
    j = np.arange(2 * BLOCK)[None, :]
    d = np.clip(i + BLOCK - j, 0, None)
    max_exact = N_BUCKETS // 2
    large = max_exact + (np.log(np.maximum(d, 1) / max_exact) / np.log(MAX_DISTANCE / max_exact)
                         * (N_BUCKETS - max_exact)).astype(np.int32)
    large = np.minimum(large, N_BUCKETS - 1)
    return np.where(d < max_exact, d, large).astype(np.int32)


def _prep_kernel(rb_ref, bucket_ref, *refs):
    n_w = (len(refs) - 1) // 2
    w_refs, o_ref, wb_refs = refs[:n_w], refs[n_w], refs[n_w + 1:]
    h = pl.program_id(0)
    bucket = bucket_ref[...]
    acc = jnp.zeros(bucket.shape, F32)
    for b in range(N_BUCKETS):
        acc = jnp.where(bucket == b, rb_ref[b, h], acc)
    o_ref[...] = acc * LOG2E
    for w_ref, wb_ref in zip(w_refs, wb_refs):
        wb_ref[...] = w_ref[...].astype(BF16)


def _prep(rel_bias, weights):
    buckets = jnp.asarray(_band_buckets())
    slab_specs = []
    for w in weights:
        assert w.shape[0] % (N_HEADS * BF16_ROWS) == 0
        slab_specs.append(pl.BlockSpec((w.shape[0] // N_HEADS, w.shape[1]), lambda h: (h, 0)))
    outs = pl.pallas_call(
        _prep_kernel,
        grid=(N_HEADS,),
        in_specs=[
            pl.BlockSpec(memory_space=pltpu.SMEM),
            pl.BlockSpec((BLOCK, 2 * BLOCK), lambda h: (0, 0)),
        ] + slab_specs,
        out_specs=[pl.BlockSpec((None, BLOCK, 2 * BLOCK), lambda h: (h, 0, 0))] + slab_specs,
        out_shape=[jax.ShapeDtypeStruct((N_HEADS, BLOCK, 2 * BLOCK), F32)]
        + [jax.ShapeDtypeStruct(w.shape, BF16) for w in weights],
        name="rel_bias_table",
    )(rel_bias, buckets, *weights)
    return outs[0], outs[1:]


def _mixer_kernel(sinks_ref, x_ref, gmix_ref, win_ref, gsgu_ref, ws_ref, bfull_ref, bias_ref,
                  wpa_ref, wpb_ref, wout_ref, wup_f32_ref, wdown_f32_ref, o_ref, wup_bf16_ref, wdown_bf16_ref,
                  h_ref, u_ref, vn_ref, qlo_ref, qhi_ref, kd_ref, vd_ref, ya_ref, yb_ref, m_ref):
    wup_bf16_ref[...] = wup_f32_ref[...].astype(BF16)
    wdown_bf16_ref[...] = wdown_f32_ref[...].astype(BF16)

    tm = TILE_ROWS
    n_tiles = x_ref.shape[0] // tm
    blocks_per_tile = tm // BLOCK
    s_idx = pl.program_id(1)
    heads_per_kv = N_HEADS // N_KV_HEADS
    n_out = D_MODEL // MXU_COLS

    @pl.when(s_idx == 0)
    def _():
        kd_ref[:, 0:BLOCK, :] = jnp.zeros((N_KV_HEADS, BLOCK, LANES), BF16)
        vd_ref[:, 0:BLOCK, :] = jnp.zeros((N_KV_HEADS, BLOCK, LANES), BF16)

    row_c = lax.broadcasted_iota(jnp.int32, (CHUNK, CHUNK), 0)
    col_c = lax.broadcasted_iota(jnp.int32, (CHUNK, CHUNK), 1)
    tril = col_c <= row_c
    wmask = [jnp.where(tril, ws_ref[g], 0.0).astype(BF16) for g in range(A_GROUPS)]
    row_i = lax.broadcasted_iota(jnp.int32, (BLOCK, 2 * BLOCK), 0)
    col_j = lax.broadcasted_iota(jnp.int32, (BLOCK, 2 * BLOCK), 1)
    band = (col_j > row_i) & (col_j <= row_i + BLOCK)
    band_first = band & ((col_j >= BLOCK) | (s_idx > 0))
    b_is_lo = lax.broadcasted_iota(jnp.int32, (BLOCK, LANES), 1) < HEAD_DIM
    q_is_lo = (lax.broadcasted_iota(jnp.int32, (tm, Q_DIM), 1) % LANES) < HEAD_DIM
    t_is_lo = lax.broadcasted_iota(jnp.int32, (tm, LANES), 1) < HEAD_DIM

    def norm(t):
        x = x_ref[t * tm:(t + 1) * tm, :]
        r = lax.rsqrt(jnp.mean(x * x, axis=-1, keepdims=True) + EPS)
        h_ref[t] = (x * r * gmix_ref[...]).astype(BF16)

    def proj_u(t):
        u_ref[t] = _gelu_tanh(_dot(h_ref[t], win_ref[:, OFF_PU:OFF_PU + A_WIDTH]))

    def proj_v(t):
        pv = _gelu_tanh(_dot(h_ref[t], win_ref[:, OFF_PV:OFF_PV + A_WIDTH]))
        rv = lax.rsqrt(jnp.mean(pv * pv, axis=-1, keepdims=True) + EPS)
        vn_ref[t] = (pv * rv * gsgu_ref[...]).astype(BF16)

    def proj_q(t):
        q = _dot(h_ref[t], win_ref[:, OFF_Q:OFF_Q + Q_DIM])
        qlo_ref[t] = jnp.where(q_is_lo, q, 0.0).astype(BF16)
        qhi_ref[t] = jnp.where(q_is_lo, 0.0, q).astype(BF16)

    def proj_kv(t):
        kvp = _dot(h_ref[t], win_ref[:, OFF_K:OFF_K + 2 * KV_DIM])
        k = kvp[:, :KV_DIM] * (HEAD_DIM ** -0.5 * LOG2E)
        v = kvp[:, KV_DIM:]
        k_sw = pltpu.roll(k, HEAD_DIM, axis=1)
        v_sw = pltpu.roll(v, HEAD_DIM, axis=1)
        rows = slice(BLOCK + t * tm, BLOCK + (t + 1) * tm)
        kd_ref[0, rows, :] = jnp.where(t_is_lo, k, k_sw).astype(BF16)
        kd_ref[1, rows, :] = jnp.where(t_is_lo, k_sw, k).astype(BF16)
        vd_ref[0, rows, :] = jnp.where(t_is_lo, v, v_sw).astype(BF16)
        vd_ref[1, rows, :] = jnp.where(t_is_lo, v_sw, v).astype(BF16)

    def sgu(t, g):
        cols = slice(g * CHUNK, (g + 1) * CHUNK)
        v_chunks = [vn_ref[t, b * CHUNK:(b + 1) * CHUNK, cols] for b in range(tm // CHUNK)]
        s_all = _dot(wmask[g], jnp.concatenate(v_chunks, axis=1))
        for b in range(tm // CHUNK):
            rows = slice(b * CHUNK, (b + 1) * CHUNK)
            s = s_all[:, b * CHUNK:(b + 1) * CHUNK] + bfull_ref[:, cols]
            ya_ref[t, rows, cols] = (u_ref[t, rows, cols] * s).astype(BF16)

    probs = {}

    def attn_scores(t, b, kv):
        rows = slice(b * BLOCK, (b + 1) * BLOCK)
        bb = t * blocks_per_tile + b
        mask = band_first if bb == 0 else band
        q_parts = []
        for g in range(heads_per_kv):
            hh = kv * heads_per_kv + g
            src = qlo_ref if hh % 2 == 0 else qhi_ref
            c = hh // 2
            q_parts.append(src[t, rows, c * LANES:(c + 1) * LANES])
        q_stack = jnp.concatenate(q_parts, axis=0)
        keys = kd_ref[kv, bb * BLOCK:(bb + 2) * BLOCK, :]
        sc = lax.dot_general(q_stack, keys, (((1,), (1,)), ((), ())),
                             preferred_element_type=F32)
        p_parts = []
        inv_l = []
        for g in range(heads_per_kv):
            hh = kv * heads_per_kv + g
            sg = sc[g * BLOCK:(g + 1) * BLOCK] + bias_ref[hh]
            sg = jnp.where(mask, sg, NEG_INF)
            sink = sinks_ref[hh] * LOG2E
            m = jnp.maximum(jnp.max(sg, axis=-1, keepdims=True), sink)
            p = jnp.exp2(sg - m)
            l = jnp.sum(p, axis=-1, keepdims=True) + jnp.exp2(sink - m)
            p_parts.append(p.astype(BF16))
            inv_l.append(1.0 / l)
        probs[t, b, kv] = (jnp.concatenate(p_parts, axis=0), inv_l)

    def attn_values(t, b, kv):
        rows = slice(b * BLOCK, (b + 1) * BLOCK)
        bb = t * blocks_per_tile + b
        p_stack, inv_l = probs.pop((t, b, kv))
        vals = vd_ref[kv, bb * BLOCK:(bb + 2) * BLOCK, :]
        o = _dot(p_stack, vals)
        for cc in range(heads_per_kv // 2):
            o_even = o[(2 * cc) * BLOCK:(2 * cc + 1) * BLOCK] * inv_l[2 * cc]
            o_odd = o[(2 * cc + 1) * BLOCK:(2 * cc + 2) * BLOCK] * inv_l[2 * cc + 1]
            col = kv * (heads_per_kv // 2) + cc
            yb_ref[t, rows, col * LANES:(col + 1) * LANES] = jnp.where(b_is_lo, o_even, o_odd).astype(BF16)

    def merge(t, c):
        cs = slice(c * MXU_COLS, (c + 1) * MXU_COLS)
        ga = _dot(h_ref[t], win_ref[:, OFF_GA + c * MXU_COLS:OFF_GA + (c + 1) * MXU_COLS])
        gb = _dot(h_ref[t], win_ref[:, OFF_GB + c * MXU_COLS:OFF_GB + (c + 1) * MXU_COLS])
        pa = _dot(ya_ref[t], wpa_ref[:, cs])
        pb = _dot(yb_ref[t], wpb_ref[:, cs])
        m_ref[t, :, cs] = (_sigmoid(ga) * pa + _sigmoid(gb) * pb).astype(BF16)

    def out_proj(t, c):
        cs = slice(c * MXU_COLS, (c + 1) * MXU_COLS)
        val = x_ref[t * tm:(t + 1) * tm, cs] + _dot(m_ref[t], wout_ref[:, cs])
        n_grp = tm // SUBLANES
        n_lt = D_MODEL // LANES
        for ll in range(MXU_COLS // LANES):
            lt = c * (MXU_COLS // LANES) + ll
            for g in range(n_grp):
                k, j0 = divmod(SUBLANES * g, n_grp)
                o_ref[t * n_lt + lt, pl.ds(SUBLANES * j0 + k, SUBLANES, stride=SUBLANES), :] = (
                    val[SUBLANES * g:SUBLANES * (g + 1), ll * LANES:(ll + 1) * LANES])

    P = functools.partial

    def stage1(t):
        return [P(proj_u, t), P(proj_v, t), P(proj_q, t), P(proj_kv, t)]

    def stage2(t):
        sgu_items = [P(sgu, t, g) for g in range(A_GROUPS)]
        def attn(b, kv):
            attn_scores(t, b, kv)
            attn_values(t, b, kv)

        attn_items = [P(attn, b, kv) for b in range(blocks_per_tile) for kv in range(N_KV_HEADS)]
        items = []
        for i, item in enumerate(attn_items):
            items += sgu_items[i * A_GROUPS // len(attn_items):(i + 1) * A_GROUPS // len(attn_items)] + [item]
        return items

    def stage3(t):
        return [P(merge, t, c) for c in range(n_out)]

    def stage4(t):
        return [P(out_proj, t, c) for c in range(n_out)]

    def interleave(heavy, light):
        n, m = len(heavy), len(light)
        done = 0
        for i, item in enumerate(heavy):
            item()
            upto = (i + 1) * m // n
            for light_item in light[done:upto]:
                light_item()
            done = upto

    def run(items):
        for item in items:
            item()

    norm(0)
    run(stage1(0))
    for t in range(n_tiles):
        heavy = []
        if t >= 2:
            heavy += stage4(t - 2)
        if t >= 1:
            heavy += stage3(t - 1)
        if t + 1 < n_tiles:
            norm(t + 1)
            heavy += stage1(t + 1)
        interleave(heavy, stage2(t)) if heavy else run(stage2(t))
    if n_tiles >= 2:
        run(stage4(n_tiles - 2))
    run(stage3(n_tiles - 1))
    run(stage4(n_tiles - 1))

    rows_all = n_tiles * tm
    kd_ref[:, 0:BLOCK, :] = kd_ref[:, rows_all:rows_all + BLOCK, :]
    vd_ref[:, 0:BLOCK, :] = vd_ref[:, rows_all:rows_all + BLOCK, :]


def _resident(shape):
    nd = len(shape)
    return pl.BlockSpec(shape, lambda *_: (0,) * nd, pipeline_mode=pl.Buffered(1))


def _token_mixer(x2d, batch, seq, sinks, g_mix, w_in, g_sgu, w_s, b_full, bias_tbl, w_pa, w_pb, w_out,
                 w_up_f32, w_down_f32):
    tm = TILE_ROWS
    rows = MIX_TILES * tm
    ns = seq // rows
    n_steps = batch * ns
    n_lt = D_MODEL // LANES
    row_spec = pl.BlockSpec((rows, D_MODEL), lambda b, s: (b * ns + s, 0))
    up_slab = D_MODEL // n_steps
    down_slab = BF16_ROWS * (D_FF // F_CHUNK)
    n_down_slabs = D_FF // down_slab
    assert D_MODEL % n_steps == 0 and up_slab % BF16_ROWS == 0 and D_FF % down_slab == 0 and n_down_slabs <= n_steps
    up_spec = pl.BlockSpec((up_slab, 2 * D_FF), lambda b, s: (b * ns + s, 0))
    down_spec = pl.BlockSpec((down_slab, D_MODEL), lambda b, s: (jnp.minimum(b * ns + s, n_down_slabs - 1), 0))
    return pl.pallas_call(
        _mixer_kernel,
        grid=(batch, ns),
        in_specs=[
            pl.BlockSpec(memory_space=pltpu.SMEM),
            row_spec,
            _resident((1, D_MODEL)),
            _resident((D_MODEL, IN_DIM)),
            _resident((1, A_WIDTH)),
            _resident((A_GROUPS, CHUNK, CHUNK)),
            _resident((CHUNK, A_WIDTH)),
            _resident((N_HEADS, BLOCK, 2 * BLOCK)),
            _resident((A_WIDTH, D_MODEL)),
            _resident((Q_DIM, D_MODEL)),
            _resident((D_MODEL, D_MODEL)),
            up_spec,
            down_spec,
        ],
        out_specs=[pl.BlockSpec((MIX_TILES * n_lt, tm, LANES), lambda b, s: (b * ns + s, 0, 0)), up_spec, down_spec],
        out_shape=[jax.ShapeDtypeStruct((x2d.shape[0] // tm * n_lt, tm, LANES), F32),
                   jax.ShapeDtypeStruct(w_up_f32.shape, BF16),
                   jax.ShapeDtypeStruct(w_down_f32.shape, BF16)],
        scratch_shapes=[
            pltpu.VMEM((MIX_TILES, tm, D_MODEL), BF16),
            pltpu.VMEM((MIX_TILES, tm, A_WIDTH), F32),
            pltpu.VMEM((MIX_TILES, tm, A_WIDTH), BF16),
            pltpu.VMEM((MIX_TILES, tm, Q_DIM), BF16),
            pltpu.VMEM((MIX_TILES, tm, Q_DIM), BF16),
            pltpu.VMEM((N_KV_HEADS, BLOCK + rows, LANES), BF16),
            pltpu.VMEM((N_KV_HEADS, BLOCK + rows, LANES), BF16),
            pltpu.VMEM((MIX_TILES, tm, A_WIDTH), BF16),
            pltpu.VMEM((MIX_TILES, tm, Q_DIM), BF16),
            pltpu.VMEM((MIX_TILES, tm, D_MODEL), BF16),
        ],
        compiler_params=pltpu.CompilerParams(
            dimension_semantics=("arbitrary", "arbitrary"),
            vmem_limit_bytes=VMEM_LIMIT),
        name="token_mixer",
    )(sinks, x2d, g_mix, w_in, g_sgu, w_s, b_full, bias_tbl, w_pa, w_pb, w_out, w_up_f32, w_down_f32)


def _ffn_kernel(x_ref, gffn_ref, wup_ref, wconv_ref, bconv_ref, wdown_ref, gfin_ref, o_ref,
                skew_ref, h_ref, carry_ref, act_ref, *, steps_per_seq):
    tm = TILE_ROWS
    n_tiles = o_ref.shape[0] // tm
    n_grp = tm // SUBLANES
    pitch = n_grp + SUBLANES
    n_lane_tiles = D_MODEL // LANES
    n_chunks = D_FF // F_CHUNK
    n_down = D_MODEL // MXU_COLS

    @pl.when(pl.program_id(0) % steps_per_seq == 0)
    def _():
        carry_ref[...] = jnp.zeros(carry_ref.shape, F32)

    def slot(t):
        return t % FFN_SLOTS

    def x_cols(t, lt0, lt1):
        return jnp.concatenate([x_ref[t * n_lane_tiles + lt] for lt in range(lt0, lt1)], axis=1)

    def prologue(t):
        x = x_cols(t, 0, n_lane_tiles)
        r = lax.rsqrt(jnp.mean(x * x, axis=-1, keepdims=True) + EPS)
        h_ref[slot(t)] = (x * r * gffn_ref[...]).astype(BF16)

    first_sublane = lax.broadcasted_iota(jnp.int32, (SUBLANES, F_CHUNK), 0) == 0

    def conv(t, cols):
        up = _dot(h_ref[slot(t)], wup_ref[:, cols])
        last = up[tm - SUBLANES:tm]
        last2 = up[tm - 2 * SUBLANES:tm - SUBLANES]

        def wrap(cur, prev):
            return jnp.where(first_sublane, pltpu.roll(prev, 1, axis=0), pltpu.roll(cur, 1, axis=0))

        m1 = wrap(last, carry_ref[1, :, cols])
        m2 = wrap(last2, carry_ref[0, :, cols])
        carry_ref[0, :, cols] = last2
        carry_ref[1, :, cols] = last
        up1 = jnp.concatenate([m1, up[:tm - SUBLANES]], axis=0)
        up2 = jnp.concatenate([m2, m1, up[:tm - 2 * SUBLANES]], axis=0)
        return (bconv_ref[:, cols] + wconv_ref[2:3, cols] * up
                + wconv_ref[1:2, cols] * up1 + wconv_ref[0:1, cols] * up2)

    def up_chunk(t, c):
        gcols = slice(c * F_CHUNK, (c + 1) * F_CHUNK)
        gate = conv(t, gcols)
        val = conv(t, slice(D_FF + c * F_CHUNK, D_FF + (c + 1) * F_CHUNK))
        hg = 0.5 * gate
        act_ref[slot(t), :, gcols] = ((hg + hg * jnp.tanh(hg)) * val).astype(BF16)

    def unpermute_store(t, lt, vals):
        for j in range(n_grp):
            skew_ref[slot(t), lt, pl.ds(j, SUBLANES, stride=pitch), :] = vals[SUBLANES * j:SUBLANES * (j + 1), :]

    def down_block(t, j):
        cs = slice(j * MXU_COLS, (j + 1) * MXU_COLS)
        lts = MXU_COLS // LANES
        y = x_cols(t, j * lts, (j + 1) * lts) + _dot(act_ref[slot(t)], wdown_ref[:, cs])
        for ll in range(lts):
            unpermute_store(t, j * lts + ll, y[:, ll * LANES:(ll + 1) * LANES])
        return jnp.sum(y * y, axis=-1, keepdims=True)

    def epilogue(t, ssqs):
        rf = lax.rsqrt(sum(ssqs) * (1.0 / D_MODEL) + EPS)
        unpermute_store(t, n_lane_tiles, jnp.broadcast_to(rf, (tm, LANES)))
        for k in range(SUBLANES):
            blk = slice(pitch * k, pitch * k + n_grp)
            rf_nat = skew_ref[slot(t), n_lane_tiles, blk, :]
            for lt in range(n_lane_tiles):
                cols = slice(lt * LANES, (lt + 1) * LANES)
                o_ref[t * tm + n_grp * k:t * tm + n_grp * (k + 1), cols] = (
                    skew_ref[slot(t), lt, blk, :] * rf_nat * gfin_ref[:, cols])

    down_before = {k * n_chunks // n_down: k for k in range(n_down)}
    assert len(down_before) == n_down
    prologue(0)
    for t in range(n_tiles):
        if t + 1 < n_tiles:
            prologue(t + 1)
        ys = {}
        for c in range(n_chunks):
            if t >= 1 and c in down_before:
                ys[down_before[c]] = down_block(t - 1, down_before[c])
            up_chunk(t, c)
        if t >= 1:
            epilogue(t - 1, [ys[j] for j in range(n_down)])
    last = n_tiles - 1
    epilogue(last, [down_block(last, j) for j in range(n_down)])


def _conv_ffn(x2d, batch, seq, g_ffn, w_up, w_conv, b_conv, w_down, g_final):
    rows = FFN_TILES * TILE_ROWS
    steps_per_seq = seq // rows
    row_spec = pl.BlockSpec((rows, D_MODEL), lambda i: (i, 0))
    skew_rows = TILE_ROWS + SUBLANES * SUBLANES
    return pl.pallas_call(
        functools.partial(_ffn_kernel, steps_per_seq=steps_per_seq),
        grid=(batch * steps_per_seq,),
        in_specs=[
            pl.BlockSpec((FFN_TILES * (D_MODEL // LANES), TILE_ROWS, LANES), lambda i: (i, 0, 0)),
            _resident((1, D_MODEL)),
            _resident((D_MODEL, 2 * D_FF)),
            _resident((CONV_WIDTH, 2 * D_FF)),
            _resident((1, 2 * D_FF)),
            _resident((D_FF, D_MODEL)),
            _resident((1, D_MODEL)),
        ],
        out_specs=row_spec,
        out_shape=jax.ShapeDtypeStruct((x2d.shape[0] // (D_MODEL // LANES) * TILE_ROWS, D_MODEL), F32),
        scratch_shapes=[
            pltpu.VMEM((FFN_SLOTS, D_MODEL // LANES + 1, skew_rows, LANES), F32),
            pltpu.VMEM((FFN_SLOTS, TILE_ROWS, D_MODEL), BF16),
            pltpu.VMEM((CONV_WIDTH - 1, SUBLANES, 2 * D_FF), F32),
            pltpu.VMEM((FFN_SLOTS, TILE_ROWS, D_FF), BF16),
        ],
        compiler_params=pltpu.CompilerParams(
            dimension_semantics=("arbitrary",),
            vmem_limit_bytes=VMEM_LIMIT),
        name="conv_ffn",
    )(x2d, g_ffn, w_up, w_conv, b_conv, w_down, g_final)


def kernel(x, g_mix, w_in, g_sgu, w_s, b_s, sinks, rel_bias, w_pa, w_pb, w_out,
           g_ffn, w_up, w_conv, b_conv, w_down, g_final):
    batch, seq, d = x.shape
    assert w_in.shape[0] == 1 and d == D_MODEL and seq % (max(FFN_TILES, MIX_TILES) * TILE_ROWS) == 0 and w_in.shape[2] == IN_DIM
    bias_tbl, (w_in_bf16, w_pa_bf16, w_pb_bf16, w_out_bf16) = _prep(
        rel_bias, (w_in[0], w_pa[0], w_pb[0], w_out[0]))
    x2d = x.reshape(batch * seq, d)
    b_full = jnp.repeat(jnp.transpose(b_s[0]), A_WIDTH // A_GROUPS, axis=1)
    x2d, w_up_bf16, w_down_bf16 = _token_mixer(
        x2d, batch, seq, sinks[0], g_mix[0][None, :], w_in_bf16, g_sgu[0][None, :],
        w_s[0], b_full, bias_tbl, w_pa_bf16, w_pb_bf16, w_out_bf16,
        w_up[0], w_down[0])
    x2d = _conv_ffn(
        x2d, batch, seq, g_ffn[0][None, :], w_up_bf16, w_conv[0], b_conv[0][None, :],
        w_down_bf16, g_final[None, :])
    return x2d.reshape(batch, seq, d)
```

```python
import functools
import math

import jax
import jax.numpy as jnp
import numpy as np
from jax import lax
from jax.experimental import pallas as pl
from jax.experimental.pallas import tpu as pltpu

D_MODEL = 1024
CHUNK = 128
A_GROUPS = 4
A_WIDTH = 512
N_HEADS = 8
N_KV_HEADS = 2
HEAD_DIM = 64
Q_DIM = N_HEADS * HEAD_DIM
KV_DIM = N_KV_HEADS * HEAD_DIM
WINDOW = 128
BLOCK = 128
N_BUCKETS = 32
MAX_DISTANCE = 128
D_FF = 2816
CONV_WIDTH = 3
EPS = 1e-6
NEG_INF = -1e30
LOG2E = math.log2(math.e)

OFF_PU = 0
OFF_PV = OFF_PU + A_WIDTH
OFF_Q = OFF_PV + A_WIDTH
OFF_K = OFF_Q + Q_DIM
OFF_V = OFF_K + KV_DIM
OFF_GA = OFF_V + KV_DIM
OFF_GB = OFF_GA + D_MODEL
IN_DIM = OFF_GB + D_MODEL

LANES = 128
SUBLANES = 8
MXU_COLS = 256
TILE_ROWS = 512
F_CHUNK = 256
FFN_TILES = 2
FFN_SLOTS = 2
MIX_TILES = 2
VMEM_BYTES = 64 * 1024 * 1024
VMEM_LIMIT = VMEM_BYTES - 8 * 1024 * 1024
BF16_ROWS = 2 * SUBLANES

F32 = jnp.float32
BF16 = jnp.bfloat16


def _gelu_tanh(x):
    c = math.sqrt(2.0 / math.pi)
    hx = 0.5 * x
    return hx + hx * jnp.tanh(x * (c + (c * 0.044715) * (x * x)))


def _sigmoid(x):
    return 1.0 / (1.0 + jnp.exp(-x))


def _dot(a, b):
    return jnp.dot(a, b, preferred_element_type=F32)


def _band_buckets():
    i = np.arange(BLOCK)[:, None]
    j = np.arange(2 * BLOCK)[None, :]
    d = np.clip(i + BLOCK - j, 0, None)
    max_exact = N_BUCKETS // 2
    large = max_exact + (np.log(np.maximum(d, 1) / max_exact) / np.log(MAX_DISTANCE / max_exact)
                         * (N_BUCKETS - max_exact)).astype(np.int32)
    large = np.minimum(large, N_BUCKETS - 1)
    in_band = (j > i) & (j <= i + BLOCK)
    return np.where(in_band, np.where(d < max_exact, d, large), -1).astype(np.int32)


def _prep_kernel(rb_ref, bucket_ref, *refs):
    n_w = (len(refs) - 1) // 2
    w_refs, o_ref, wb_refs = refs[:n_w], refs[n_w], refs[n_w + 1:]
    h = pl.program_id(0)
    bucket = bucket_ref[...]
    acc = jnp.zeros(bucket.shape, F32)
    for b in range(N_BUCKETS):
        acc = jnp.where(bucket == b, rb_ref[b, h], acc)
    o_ref[...] = jnp.where(bucket >= 0, acc * LOG2E, NEG_INF)
    for w_ref, wb_ref in zip(w_refs, wb_refs):
        wb_ref[...] = w_ref[...].astype(BF16)


def _prep(rel_bias, weights):
    buckets = jnp.asarray(_band_buckets())
    slab_specs = []
    for w in weights:
        assert w.shape[0] % (N_HEADS * BF16_ROWS) == 0
        slab_specs.append(pl.BlockSpec((w.shape[0] // N_HEADS, w.shape[1]), lambda h: (h, 0)))
    outs = pl.pallas_call(
        _prep_kernel,
        grid=(N_HEADS,),
        in_specs=[
            pl.BlockSpec(memory_space=pltpu.SMEM),
            pl.BlockSpec((BLOCK, 2 * BLOCK), lambda h: (0, 0)),
        ] + slab_specs,
        out_specs=[pl.BlockSpec((None, BLOCK, 2 * BLOCK), lambda h: (h, 0, 0))] + slab_specs,
        out_shape=[jax.ShapeDtypeStruct((N_HEADS, BLOCK, 2 * BLOCK), F32)]
        + [jax.ShapeDtypeStruct(w.shape, BF16) for w in weights],
        name="rel_bias_table",
    )(rel_bias, buckets, *weights)
    return outs[0], outs[1:]


def _mixer_kernel(sinks_ref, x_ref, gmix_ref, win_ref, gsgu_ref, ws_ref, bfull_ref, bias_ref,
                  wpa_ref, wpb_ref, wout_ref, wup_f32_ref, wdown_f32_ref, o_ref, wup_bf16_ref, wdown_bf16_ref,
                  h_ref, u_ref, vn_ref, qlo_ref, qhi_ref, kd_ref, vd_ref, ya_ref, yb_ref, m_ref):
    wup_bf16_ref[...] = wup_f32_ref[...].astype(BF16)
    wdown_bf16_ref[...] = wdown_f32_ref[...].astype(BF16)

    tm = TILE_ROWS
    n_tiles = x_ref.shape[0] // tm
    blocks_per_tile = tm // BLOCK
    s_idx = pl.program_id(1)
    heads_per_kv = N_HEADS // N_KV_HEADS
    n_out = D_MODEL // MXU_COLS

    @pl.when(s_idx == 0)
    def _():
        kd_ref[:, 0:BLOCK, :] = jnp.zeros((N_KV_HEADS, BLOCK, LANES), BF16)
        vd_ref[:, 0:BLOCK, :] = jnp.zeros((N_KV_HEADS, BLOCK, LANES), BF16)

    row_c = lax.broadcasted_iota(jnp.int32, (CHUNK, CHUNK), 0)
    col_c = lax.broadcasted_iota(jnp.int32, (CHUNK, CHUNK), 1)
    tril = col_c <= row_c
    wmask = [jnp.where(tril, ws_ref[g], 0.0).astype(BF16) for g in range(A_GROUPS)]
    col_j = lax.broadcasted_iota(jnp.int32, (BLOCK, 2 * BLOCK), 1)
    has_prev = (col_j >= BLOCK) | (s_idx > 0)
    b_is_lo = lax.broadcasted_iota(jnp.int32, (BLOCK, LANES), 1) < HEAD_DIM
    q_is_lo = (lax.broadcasted_iota(jnp.int32, (tm, Q_DIM), 1) % LANES) < HEAD_DIM
    t_is_lo = lax.broadcasted_iota(jnp.int32, (tm, LANES), 1) < HEAD_DIM

    def norm(t):
        x = x_ref[t * tm:(t + 1) * tm, :]
        r = lax.rsqrt(jnp.mean(x * x, axis=-1, keepdims=True) + EPS)
        h_ref[t] = (x * r * gmix_ref[...]).astype(BF16)

    def proj_u(t):
        u_ref[t] = _gelu_tanh(_dot(h_ref[t], win_ref[:, OFF_PU:OFF_PU + A_WIDTH]))

    def proj_v(t):
        pv = _gelu_tanh(_dot(h_ref[t], win_ref[:, OFF_PV:OFF_PV + A_WIDTH]))
        rv = lax.rsqrt(jnp.mean(pv * pv, axis=-1, keepdims=True) + EPS)
        vn_ref[t] = (pv * rv * gsgu_ref[...]).astype(BF16)

    def proj_q(t):
        q = _dot(h_ref[t], win_ref[:, OFF_Q:OFF_Q + Q_DIM])
        qlo_ref[t] = jnp.where(q_is_lo, q, 0.0).astype(BF16)
        qhi_ref[t] = jnp.where(q_is_lo, 0.0, q).astype(BF16)

    def proj_kv(t):
        kvp = _dot(h_ref[t], win_ref[:, OFF_K:OFF_K + 2 * KV_DIM])
        k = kvp[:, :KV_DIM] * (HEAD_DIM ** -0.5 * LOG2E)
        v = kvp[:, KV_DIM:]
        k_sw = pltpu.roll(k, HEAD_DIM, axis=1)
        v_sw = pltpu.roll(v, HEAD_DIM, axis=1)
        rows = slice(BLOCK + t * tm, BLOCK + (t + 1) * tm)
        kd_ref[0, rows, :] = jnp.where(t_is_lo, k, k_sw).astype(BF16)
        kd_ref[1, rows, :] = jnp.where(t_is_lo, k_sw, k).astype(BF16)
        vd_ref[0, rows, :] = jnp.where(t_is_lo, v, v_sw).astype(BF16)
        vd_ref[1, rows, :] = jnp.where(t_is_lo, v_sw, v).astype(BF16)

    def sgu(t, g):
        cols = slice(g * CHUNK, (g + 1) * CHUNK)
        v_chunks = [vn_ref[t, b * CHUNK:(b + 1) * CHUNK, cols] for b in range(tm // CHUNK)]
        s_all = _dot(wmask[g], jnp.concatenate(v_chunks, axis=1))
        for b in range(tm // CHUNK):
            rows = slice(b * CHUNK, (b + 1) * CHUNK)
            s = s_all[:, b * CHUNK:(b + 1) * CHUNK] + bfull_ref[:, cols]
            ya_ref[t, rows, cols] = (u_ref[t, rows, cols] * s).astype(BF16)

    probs = {}

    def attn_scores(t, b, kv):
        rows = slice(b * BLOCK, (b + 1) * BLOCK)
        bb = t * blocks_per_tile + b
        q_parts = []
        for g in range(heads_per_kv):
            hh = kv * heads_per_kv + g
            src = qlo_ref if hh % 2 == 0 else qhi_ref
            c = hh // 2
            q_parts.append(src[t, rows, c * LANES:(c + 1) * LANES])
        q_stack = jnp.concatenate(q_parts, axis=0)
        keys = kd_ref[kv, bb * BLOCK:(bb + 2) * BLOCK, :]
        sc = lax.dot_general(q_stack, keys, (((1,), (1,)), ((), ())),
                             preferred_element_type=F32)
        p_parts = []
        inv_l = []
        for g in range(heads_per_kv):
            hh = kv * heads_per_kv + g
            sg = sc[g * BLOCK:(g + 1) * BLOCK] + bias_ref[hh]
            if bb == 0:
                sg = jnp.where(has_prev, sg, NEG_INF)
            sink = sinks_ref[hh] * LOG2E
            m = jnp.maximum(jnp.max(sg, axis=-1, keepdims=True), sink)
            p = jnp.exp2(sg - m)
            l = jnp.sum(p, axis=-1, keepdims=True) + jnp.exp2(sink - m)
            p_parts.append(p.astype(BF16))
            inv_l.append(1.0 / l)
        probs[t, b, kv] = (jnp.concatenate(p_parts, axis=0), inv_l)

    def attn_values(t, b, kv):
        rows = slice(b * BLOCK, (b + 1) * BLOCK)
        bb = t * blocks_per_tile + b
        p_stack, inv_l = probs.pop((t, b, kv))
        vals = vd_ref[kv, bb * BLOCK:(bb + 2) * BLOCK, :]
        o = _dot(p_stack, vals)
        for cc in range(heads_per_kv // 2):
            o_even = o[(2 * cc) * BLOCK:(2 * cc + 1) * BLOCK] * inv_l[2 * cc]
            o_odd = o[(2 * cc + 1) * BLOCK:(2 * cc + 2) * BLOCK] * inv_l[2 * cc + 1]
            col = kv * (heads_per_kv // 2) + cc
            yb_ref[t, rows, col * LANES:(col + 1) * LANES] = jnp.where(b_is_lo, o_even, o_odd).astype(BF16)

    def merge(t, c):
        cs = slice(c * MXU_COLS, (c + 1) * MXU_COLS)
        ga = _dot(h_ref[t], win_ref[:, OFF_GA + c * MXU_COLS:OFF_GA + (c + 1) * MXU_COLS])
        gb = _dot(h_ref[t], win_ref[:, OFF_GB + c * MXU_COLS:OFF_GB + (c + 1) * MXU_COLS])
        pa = _dot(ya_ref[t], wpa_ref[:, cs])
        pb = _dot(yb_ref[t], wpb_ref[:, cs])
        m_ref[t, :, cs] = (_sigmoid(ga) * pa + _sigmoid(gb) * pb).astype(BF16)

    def out_proj(t, c):
        cs = slice(c * MXU_COLS, (c + 1) * MXU_COLS)
        val = x_ref[t * tm:(t + 1) * tm, cs] + _dot(m_ref[t], wout_ref[:, cs])
        n_grp = tm // SUBLANES
        n_lt = D_MODEL // LANES
        for ll in range(MXU_COLS // LANES):
            lt = c * (MXU_COLS // LANES) + ll
            for g in range(n_grp):
                k, j0 = divmod(SUBLANES * g, n_grp)
                o_ref[t * n_lt + lt, pl.ds(SUBLANES * j0 + k, SUBLANES, stride=SUBLANES), :] = (
                    val[SUBLANES * g:SUBLANES * (g + 1), ll * LANES:(ll + 1) * LANES])

    P = functools.partial

    def stage1(t):
        return [P(proj_u, t), P(proj_v, t), P(proj_q, t), P(proj_kv, t)]

    def stage2(t):
        sgu_items = [P(sgu, t, g) for g in range(A_GROUPS)]
        def attn(b, kv):
            attn_scores(t, b, kv)
            attn_values(t, b, kv)

        attn_items = [P(attn, b, kv) for b in range(blocks_per_tile) for kv in range(N_KV_HEADS)]
        items = []
        for i, item in enumerate(attn_items):
            items += sgu_items[i * A_GROUPS // len(attn_items):(i + 1) * A_GROUPS // len(attn_items)] + [item]
        return items

    def stage3(t):
        return [P(merge, t, c) for c in range(n_out)]

    def stage4(t):
        return [P(out_proj, t, c) for c in range(n_out)]

    def interleave(heavy, light):
        n, m = len(heavy), len(light)
        done = 0
        for i, item in enumerate(heavy):
            item()
            upto = (i + 1) * m // n
            for light_item in light[done:upto]:
                light_item()
            done = upto

    def run(items):
        for item in items:
            item()

    norm(0)
    run(stage1(0))
    for t in range(n_tiles):
        heavy = []
        if t >= 2:
            heavy += stage4(t - 2)
        if t >= 1:
            heavy += stage3(t - 1)
        if t + 1 < n_tiles:
            norm(t + 1)
            heavy += stage1(t + 1)
        interleave(heavy, stage2(t)) if heavy else run(stage2(t))
    if n_tiles >= 2:
        run(stage4(n_tiles - 2))
    run(stage3(n_tiles - 1))
    run(stage4(n_tiles - 1))

    rows_all = n_tiles * tm
    kd_ref[:, 0:BLOCK, :] = kd_ref[:, rows_all:rows_all + BLOCK, :]
    vd_ref[:, 0:BLOCK, :] = vd_ref[:, rows_all:rows_all + BLOCK, :]


def _resident(shape):
    nd = len(shape)
    return pl.BlockSpec(shape, lambda *_: (0,) * nd, pipeline_mode=pl.Buffered(1))


def _token_mixer(x2d, batch, seq, sinks, g_mix, w_in, g_sgu, w_s, b_full, bias_tbl, w_pa, w_pb, w_out,
                 w_up_f32, w_down_f32):
    tm = TILE_ROWS
    rows = MIX_TILES * tm
    ns = seq // rows
    n_steps = batch * ns
    n_lt = D_MODEL // LANES
    row_spec = pl.BlockSpec((rows, D_MODEL), lambda b, s: (b * ns + s, 0))
    up_slab = D_MODEL // n_steps
    down_slab = BF16_ROWS * (D_FF // F_CHUNK)
    n_down_slabs = D_FF // down_slab
    assert D_MODEL % n_steps == 0 and up_slab % BF16_ROWS == 0 and D_FF % down_slab == 0 and n_down_slabs <= n_steps
    up_spec = pl.BlockSpec((up_slab, 2 * D_FF), lambda b, s: (b * ns + s, 0))
    down_spec = pl.BlockSpec((down_slab, D_MODEL), lambda b, s: (jnp.minimum(b * ns + s, n_down_slabs - 1), 0))
    return pl.pallas_call(
        _mixer_kernel,
        grid=(batch, ns),
        in_specs=[
            pl.BlockSpec(memory_space=pltpu.SMEM),
            row_spec,
            _resident((1, D_MODEL)),
            _resident((D_MODEL, IN_DIM)),
            _resident((1, A_WIDTH)),
            _resident((A_GROUPS, CHUNK, CHUNK)),
            _resident((CHUNK, A_WIDTH)),
            _resident((N_HEADS, BLOCK, 2 * BLOCK)),
            _resident((A_WIDTH, D_MODEL)),
            _resident((Q_DIM, D_MODEL)),
            _resident((D_MODEL, D_MODEL)),
            up_spec,
            down_spec,
        ],
        out_specs=[pl.BlockSpec((MIX_TILES * n_lt, tm, LANES), lambda b, s: (b * ns + s, 0, 0)), up_spec, down_spec],
        out_shape=[jax.ShapeDtypeStruct((x2d.shape[0] // tm * n_lt, tm, LANES), F32),
                   jax.ShapeDtypeStruct(w_up_f32.shape, BF16),
                   jax.ShapeDtypeStruct(w_down_f32.shape, BF16)],
        scratch_shapes=[
            pltpu.VMEM((MIX_TILES, tm, D_MODEL), BF16),
            pltpu.VMEM((MIX_TILES, tm, A_WIDTH), F32),
            pltpu.VMEM((MIX_TILES, tm, A_WIDTH), BF16),
            pltpu.VMEM((MIX_TILES, tm, Q_DIM), BF16),
            pltpu.VMEM((MIX_TILES, tm, Q_DIM), BF16),
            pltpu.VMEM((N_KV_HEADS, BLOCK + rows, LANES), BF16),
            pltpu.VMEM((N_KV_HEADS, BLOCK + rows, LANES), BF16),
            pltpu.VMEM((MIX_TILES, tm, A_WIDTH), BF16),
            pltpu.VMEM((MIX_TILES, tm, Q_DIM), BF16),
            pltpu.VMEM((MIX_TILES, tm, D_MODEL), BF16),
        ],
        compiler_params=pltpu.CompilerParams(
            dimension_semantics=("arbitrary", "arbitrary"),
            vmem_limit_bytes=VMEM_LIMIT),
        name="token_mixer",
    )(sinks, x2d, g_mix, w_in, g_sgu, w_s, b_full, bias_tbl, w_pa, w_pb, w_out, w_up_f32, w_down_f32)


def _ffn_kernel(x_ref, gffn_ref, wup_ref, wconv_ref, bconv_ref, wdown_ref, gfin_ref, o_ref,
                skew_ref, h_ref, carry_ref, act_ref, *, steps_per_seq):
    tm = TILE_ROWS
    n_tiles = o_ref.shape[0] // tm
    n_grp = tm // SUBLANES
    pitch = n_grp + SUBLANES
    n_lane_tiles = D_MODEL // LANES
    n_chunks = D_FF // F_CHUNK
    n_down = D_MODEL // MXU_COLS

    @pl.when(pl.program_id(0) % steps_per_seq == 0)
    def _():
        carry_ref[...] = jnp.zeros(carry_ref.shape, F32)

    def slot(t):
        return t % FFN_SLOTS

    def x_cols(t, lt0, lt1):
        return jnp.concatenate([x_ref[t * n_lane_tiles + lt] for lt in range(lt0, lt1)], axis=1)

    def prologue(t):
        x = x_cols(t, 0, n_lane_tiles)
        r = lax.rsqrt(jnp.mean(x * x, axis=-1, keepdims=True) + EPS)
        h_ref[slot(t)] = (x * r * gffn_ref[...]).astype(BF16)

    first_sublane = lax.broadcasted_iota(jnp.int32, (SUBLANES, F_CHUNK), 0) == 0

    def conv(t, cols):
        up = _dot(h_ref[slot(t)], wup_ref[:, cols])
        last = up[tm - SUBLANES:tm]
        last2 = up[tm - 2 * SUBLANES:tm - SUBLANES]

        def wrap(cur, prev):
            return jnp.where(first_sublane, pltpu.roll(prev, 1, axis=0), pltpu.roll(cur, 1, axis=0))

        m1 = wrap(last, carry_ref[1, :, cols])
        m2 = wrap(last2, carry_ref[0, :, cols])
        carry_ref[0, :, cols] = last2
        carry_ref[1, :, cols] = last
        up1 = jnp.concatenate([m1, up[:tm - SUBLANES]], axis=0)
        up2 = jnp.concatenate([m2, m1, up[:tm - 2 * SUBLANES]], axis=0)
        return (bconv_ref[:, cols] + wconv_ref[2:3, cols] * up
                + wconv_ref[1:2, cols] * up1 + wconv_ref[0:1, cols] * up2)

    def up_chunk(t, c):
        gcols = slice(c * F_CHUNK, (c + 1) * F_CHUNK)
        gate = conv(t, gcols)
        val = conv(t, slice(D_FF + c * F_CHUNK, D_FF + (c + 1) * F_CHUNK))
        hg = 0.5 * gate
        act_ref[slot(t), :, gcols] = ((hg + hg * jnp.tanh(hg)) * val).astype(BF16)

    def unpermute_store(t, lt, vals):
        for j in range(n_grp):
            skew_ref[slot(t), lt, pl.ds(j, SUBLANES, stride=pitch), :] = vals[SUBLANES * j:SUBLANES * (j + 1), :]

    def down_block(t, j):
        cs = slice(j * MXU_COLS, (j + 1) * MXU_COLS)
        lts = MXU_COLS // LANES
        y = x_cols(t, j * lts, (j + 1) * lts) + _dot(act_ref[slot(t)], wdown_ref[:, cs])
        for ll in range(lts):
            unpermute_store(t, j * lts + ll, y[:, ll * LANES:(ll + 1) * LANES])
        return jnp.sum(y * y, axis=-1, keepdims=True)

    def epilogue(t, ssqs):
        rf = lax.rsqrt(sum(ssqs) * (1.0 / D_MODEL) + EPS)
        unpermute_store(t, n_lane_tiles, jnp.broadcast_to(rf, (tm, LANES)))
        for k in range(SUBLANES):
            blk = slice(pitch * k, pitch * k + n_grp)
            rf_nat = skew_ref[slot(t), n_lane_tiles, blk, :]
            for lt in range(n_lane_tiles):
                cols = slice(lt * LANES, (lt + 1) * LANES)
                o_ref[t * tm + n_grp * k:t * tm + n_grp * (k + 1), cols] = (
                    skew_ref[slot(t), lt, blk, :] * rf_nat * gfin_ref[:, cols])

    down_before = {k * n_chunks // n_down: k for k in range(n_down)}
    assert len(down_before) == n_down
    prologue(0)
    for t in range(n_tiles):
        if t + 1 < n_tiles:
            prologue(t + 1)
        ys = {}
        for c in range(n_chunks):
            if t >= 1 and c in down_before:
                ys[down_before[c]] = down_block(t - 1, down_before[c])
            up_chunk(t, c)
        if t >= 1:
            epilogue(t - 1, [ys[j] for j in range(n_down)])
    last = n_tiles - 1
    epilogue(last, [down_block(last, j) for j in range(n_down)])


def _conv_ffn(x2d, batch, seq, g_ffn, w_up, w_conv, b_conv, w_down, g_final):
    rows = FFN_TILES * TILE_ROWS
    steps_per_seq = seq // rows
    row_spec = pl.BlockSpec((rows, D_MODEL), lambda i: (i, 0))
    skew_rows = TILE_ROWS + SUBLANES * SUBLANES
    return pl.pallas_call(
        functools.partial(_ffn_kernel, steps_per_seq=steps_per_seq),
        grid=(batch * steps_per_seq,),
        in_specs=[
            pl.BlockSpec((FFN_TILES * (D_MODEL // LANES), TILE_ROWS, LANES), lambda i: (i, 0, 0)),
            _resident((1, D_MODEL)),
            _resident((D_MODEL, 2 * D_FF)),
            _resident((CONV_WIDTH, 2 * D_FF)),
            _resident((1, 2 * D_FF)),
            _resident((D_FF, D_MODEL)),
            _resident((1, D_MODEL)),
        ],
        out_specs=row_spec,
        out_shape=jax.ShapeDtypeStruct((x2d.shape[0] // (D_MODEL // LANES) * TILE_ROWS, D_MODEL), F32),
        scratch_shapes=[
            pltpu.VMEM((FFN_SLOTS, D_MODEL // LANES + 1, skew_rows, LANES), F32),
            pltpu.VMEM((FFN_SLOTS, TILE_ROWS, D_MODEL), BF16),
            pltpu.VMEM((CONV_WIDTH - 1, SUBLANES, 2 * D_FF), F32),
            pltpu.VMEM((FFN_SLOTS, TILE_ROWS, D_FF), BF16),
        ],
        compiler_params=pltpu.CompilerParams(
            dimension_semantics=("arbitrary",),
            vmem_limit_bytes=VMEM_LIMIT),
        name="conv_ffn",
    )(x2d, g_ffn, w_up, w_conv, b_conv, w_down, g_final)


def kernel(x, g_mix, w_in, g_sgu, w_s, b_s, sinks, rel_bias, w_pa, w_pb, w_out,
           g_ffn, w_up, w_conv, b_conv, w_down, g_final):
    batch, seq, d = x.shape
    assert w_in.shape[0] == 1 and d == D_MODEL and seq % (max(FFN_TILES, MIX_TILES) * TILE_ROWS) == 0 and w_in.shape[2] == IN_DIM
    bias_tbl, (w_in_bf16, w_pa_bf16, w_pb_bf16, w_out_bf16) = _prep(
        rel_bias, (w_in[0], w_pa[0], w_pb[0], w_out[0]))
    x2d = x.reshape(batch * seq, d)
    b_full = jnp.repeat(jnp.transpose(b_s[0]), A_WIDTH // A_GROUPS, axis=1)
    x2d, w_up_bf16, w_down_bf16 = _token_mixer(
        x2d, batch, seq, sinks[0], g_mix[0][None, :], w_in_bf16, g_sgu[0][None, :],
        w_s[0], b_full, bias_tbl, w_pa_bf16, w_pb_bf16, w_out_bf16,
        w_up[0], w_down[0])
    x2d = _conv_ffn(
        x2d, batch, seq, g_ffn[0][None, :], w_up_bf16, w_conv[0], b_conv[0][None, :],
        w_down_bf16, g_final[None, :])
    return x2d.reshape(batch, seq, d)
```

```python
import functools
import math

import jax
import jax.numpy as jnp
import numpy as np
from jax import lax
from jax.experimental import pallas as pl
from jax.experimental.pallas import tpu as pltpu

D_MODEL = 1024
CHUNK = 128
A_GROUPS = 4
A_WIDTH = 512
N_HEADS = 8
N_KV_HEADS = 2
HEAD_DIM = 64
Q_DIM = N_HEADS * HEAD_DIM
KV_DIM = N_KV_HEADS * HEAD_DIM
WINDOW = 128
BLOCK = 128
N_BUCKETS = 32
MAX_DISTANCE = 128
D_FF = 2816
CONV_WIDTH = 3
EPS = 1e-6
NEG_INF = -1e30
LOG2E = math.log2(math.e)

OFF_PU = 0
OFF_PV = OFF_PU + A_WIDTH
OFF_Q = OFF_PV + A_WIDTH
OFF_K = OFF_Q + Q_DIM
OFF_V = OFF_K + KV_DIM
OFF_GA = OFF_V + KV_DIM
OFF_GB = OFF_GA + D_MODEL
IN_DIM = OFF_GB + D_MODEL

LANES = 128
SUBLANES = 8
MXU_COLS = 256
TILE_ROWS = 512
F_CHUNK = 256
FFN_TILES = 2
FFN_SLOTS = 2
MIX_TILES = 2
VMEM_BYTES = 64 * 1024 * 1024
VMEM_LIMIT = VMEM_BYTES - 8 * 1024 * 1024
BF16_ROWS = 2 * SUBLANES

F32 = jnp.float32
BF16 = jnp.bfloat16


def _gelu_tanh(x):
    c = math.sqrt(2.0 / math.pi)
    hx = 0.5 * x
    return hx + hx * jnp.tanh(x * (c + (c * 0.044715) * (x * x)))


def _dot(a, b):
    return jnp.dot(a, b, preferred_element_type=F32)


def _band_buckets():
    i = np.arange(BLOCK)[:, None]
    j = np.arange(2 * BLOCK)[None, :]
    d = np.clip(i + BLOCK - j, 0, None)
    max_exact = N_BUCKETS // 2
    large = max_exact + (np.log(np.maximum(d, 1) / max_exact) / np.log(MAX_DISTANCE / max_exact)
                         * (N_BUCKETS - max_exact)).astype(np.int32)
    large = np.minimum(large, N_BUCKETS - 1)
    in_band = (j > i) & (j <= i + BLOCK)
    return np.where(in_band, np.where(d < max_exact, d, large), -1).astype(np.int32)


def _prep_kernel(rb_ref, bucket_ref, *refs, halved_cols):
    n_w = (len(refs) - 1) // 2
    w_refs, o_ref, wb_refs = refs[:n_w], refs[n_w], refs[n_w + 1:]
    h = pl.program_id(0)
    bucket = bucket_ref[...]
    acc = jnp.zeros(bucket.shape, F32)
    for b in range(N_BUCKETS):
        acc = jnp.where(bucket == b, rb_ref[b, h], acc)
    o_ref[...] = jnp.where(bucket >= 0, acc * LOG2E, NEG_INF)
    for w_ref, wb_ref, cols in zip(w_refs, wb_refs, halved_cols):
        lo, hi = cols if cols is not None else (0, 0)
        n_cols = w_ref.shape[1]
        if lo > 0:
            wb_ref[:, :lo] = w_ref[:, :lo].astype(BF16)
        if hi > lo:
            wb_ref[:, lo:hi] = (0.5 * w_ref[:, lo:hi]).astype(BF16)
        if hi < n_cols:
            wb_ref[:, hi:] = w_ref[:, hi:].astype(BF16)


def _prep(rel_bias, weights, halved_cols):
    buckets = jnp.asarray(_band_buckets())
    slab_specs = []
    for w in weights:
        assert w.shape[0] % (N_HEADS * BF16_ROWS) == 0
        slab_specs.append(pl.BlockSpec((w.shape[0] // N_HEADS, w.shape[1]), lambda h: (h, 0)))
    outs = pl.pallas_call(
        functools.partial(_prep_kernel, halved_cols=halved_cols),
        grid=(N_HEADS,),
        in_specs=[
            pl.BlockSpec(memory_space=pltpu.SMEM),
            pl.BlockSpec((BLOCK, 2 * BLOCK), lambda h: (0, 0)),
        ] + slab_specs,
        out_specs=[pl.BlockSpec((None, BLOCK, 2 * BLOCK), lambda h: (h, 0, 0))] + slab_specs,
        out_shape=[jax.ShapeDtypeStruct((N_HEADS, BLOCK, 2 * BLOCK), F32)]
        + [jax.ShapeDtypeStruct(w.shape, BF16) for w in weights],
        name="rel_bias_table",
    )(rel_bias, buckets, *weights)
    return outs[0], outs[1:]


def _mixer_kernel(sinks_ref, x_ref, gmix_ref, win_ref, gsgu_ref, ws_ref, bfull_ref, bias_ref,
                  wpa_ref, wpb_ref, wout_ref, wup_f32_ref, wdown_f32_ref, o_ref, wup_bf16_ref, wdown_bf16_ref,
                  h_ref, u_ref, vn_ref, qlo_ref, qhi_ref, kd_ref, vd_ref, ya_ref, yb_ref, m_ref):
    wup_bf16_ref[...] = wup_f32_ref[...].astype(BF16)
    wdown_bf16_ref[...] = wdown_f32_ref[...].astype(BF16)

    tm = TILE_ROWS
    n_tiles = x_ref.shape[0] // tm
    blocks_per_tile = tm // BLOCK
    s_idx = pl.program_id(1)
    heads_per_kv = N_HEADS // N_KV_HEADS
    n_out = D_MODEL // MXU_COLS

    @pl.when(s_idx == 0)
    def _():
        kd_ref[:, 0:BLOCK, :] = jnp.zeros((N_KV_HEADS, BLOCK, LANES), BF16)
        vd_ref[:, 0:BLOCK, :] = jnp.zeros((N_KV_HEADS, BLOCK, LANES), BF16)

    row_c = lax.broadcasted_iota(jnp.int32, (CHUNK, CHUNK), 0)
    col_c = lax.broadcasted_iota(jnp.int32, (CHUNK, CHUNK), 1)
    tril = col_c <= row_c
    wmask = [jnp.where(tril, ws_ref[g], 0.0).astype(BF16) for g in range(A_GROUPS)]
    col_j = lax.broadcasted_iota(jnp.int32, (BLOCK, 2 * BLOCK), 1)
    has_prev = (col_j >= BLOCK) | (s_idx > 0)
    b_is_lo = lax.broadcasted_iota(jnp.int32, (BLOCK, LANES), 1) < HEAD_DIM
    q_is_lo = (lax.broadcasted_iota(jnp.int32, (tm, Q_DIM), 1) % LANES) < HEAD_DIM
    t_is_lo = lax.broadcasted_iota(jnp.int32, (tm, LANES), 1) < HEAD_DIM

    def norm(t):
        x = x_ref[t * tm:(t + 1) * tm, :]
        r = lax.rsqrt(jnp.mean(x * x, axis=-1, keepdims=True) + EPS)
        h_ref[t] = (x * r * gmix_ref[...]).astype(BF16)

    def proj_u(t):
        u_ref[t] = _gelu_tanh(_dot(h_ref[t], win_ref[:, OFF_PU:OFF_PU + A_WIDTH]))

    def proj_v(t):
        pv = _gelu_tanh(_dot(h_ref[t], win_ref[:, OFF_PV:OFF_PV + A_WIDTH]))
        rv = lax.rsqrt(jnp.mean(pv * pv, axis=-1, keepdims=True) + EPS)
        vn_ref[t] = (pv * rv * gsgu_ref[...]).astype(BF16)

    def proj_q(t):
        q = _dot(h_ref[t], win_ref[:, OFF_Q:OFF_Q + Q_DIM])
        qlo_ref[t] = jnp.where(q_is_lo, q, 0.0).astype(BF16)
        qhi_ref[t] = jnp.where(q_is_lo, 0.0, q).astype(BF16)

    def proj_kv(t):
        kvp = _dot(h_ref[t], win_ref[:, OFF_K:OFF_K + 2 * KV_DIM])
        k = kvp[:, :KV_DIM] * (HEAD_DIM ** -0.5 * LOG2E)
        v = kvp[:, KV_DIM:]
        k_sw = pltpu.roll(k, HEAD_DIM, axis=1)
        v_sw = pltpu.roll(v, HEAD_DIM, axis=1)
        rows = slice(BLOCK + t * tm, BLOCK + (t + 1) * tm)
        kd_ref[0, rows, :] = jnp.where(t_is_lo, k, k_sw).astype(BF16)
        kd_ref[1, rows, :] = jnp.where(t_is_lo, k_sw, k).astype(BF16)
        vd_ref[0, rows, :] = jnp.where(t_is_lo, v, v_sw).astype(BF16)
        vd_ref[1, rows, :] = jnp.where(t_is_lo, v_sw, v).astype(BF16)

    def sgu(t, g):
        cols = slice(g * CHUNK, (g + 1) * CHUNK)
        v_chunks = [vn_ref[t, b * CHUNK:(b + 1) * CHUNK, cols] for b in range(tm // CHUNK)]
        s_all = _dot(wmask[g], jnp.concatenate(v_chunks, axis=1))
        for b in range(tm // CHUNK):
            rows = slice(b * CHUNK, (b + 1) * CHUNK)
            s = s_all[:, b * CHUNK:(b + 1) * CHUNK] + bfull_ref[:, cols]
            ya_ref[t, rows, cols] = (u_ref[t, rows, cols] * s).astype(BF16)

    probs = {}

    def attn_scores(t, b, kv):
        rows = slice(b * BLOCK, (b + 1) * BLOCK)
        bb = t * blocks_per_tile + b
        q_parts = []
        for g in range(heads_per_kv):
            hh = kv * heads_per_kv + g
            src = qlo_ref if hh % 2 == 0 else qhi_ref
            c = hh // 2
            q_parts.append(src[t, rows, c * LANES:(c + 1) * LANES])
        q_stack = jnp.concatenate(q_parts, axis=0)
        keys = kd_ref[kv, bb * BLOCK:(bb + 2) * BLOCK, :]
        sc = lax.dot_general(q_stack, keys, (((1,), (1,)), ((), ())),
                             preferred_element_type=F32)
        p_parts = []
        inv_l = []
        for g in range(heads_per_kv):
            hh = kv * heads_per_kv + g
            sg = sc[g * BLOCK:(g + 1) * BLOCK] + bias_ref[hh]
            if bb == 0:
                sg = jnp.where(has_prev, sg, NEG_INF)
            sink = sinks_ref[hh] * LOG2E
            m = jnp.maximum(jnp.max(sg, axis=-1, keepdims=True), sink)
            p = jnp.exp2(sg - m)
            l = jnp.sum(p, axis=-1, keepdims=True) + jnp.exp2(sink - m)
            p_parts.append(p.astype(BF16))
            inv_l.append(1.0 / l)
        probs[t, b, kv] = (jnp.concatenate(p_parts, axis=0), inv_l)

    def attn_values(t, b, kv):
        rows = slice(b * BLOCK, (b + 1) * BLOCK)
        bb = t * blocks_per_tile + b
        p_stack, inv_l = probs.pop((t, b, kv))
        vals = vd_ref[kv, bb * BLOCK:(bb + 2) * BLOCK, :]
        o = _dot(p_stack, vals)
        for cc in range(heads_per_kv // 2):
            o_even = o[(2 * cc) * BLOCK:(2 * cc + 1) * BLOCK] * inv_l[2 * cc]
            o_odd = o[(2 * cc + 1) * BLOCK:(2 * cc + 2) * BLOCK] * inv_l[2 * cc + 1]
            col = kv * (heads_per_kv // 2) + cc
            yb_ref[t, rows, col * LANES:(col + 1) * LANES] = jnp.where(b_is_lo, o_even, o_odd).astype(BF16)

    def merge(t, c):
        cs = slice(c * MXU_COLS, (c + 1) * MXU_COLS)
        ta = jnp.tanh(_dot(h_ref[t], win_ref[:, OFF_GA + c * MXU_COLS:OFF_GA + (c + 1) * MXU_COLS]))
        tb = jnp.tanh(_dot(h_ref[t], win_ref[:, OFF_GB + c * MXU_COLS:OFF_GB + (c + 1) * MXU_COLS]))
        hpa = _dot(ya_ref[t], wpa_ref[:, cs])
        hpb = _dot(yb_ref[t], wpb_ref[:, cs])
        m_ref[t, :, cs] = ((hpa + hpa * ta) + (hpb + hpb * tb)).astype(BF16)

    def out_proj(t, c):
        cs = slice(c * MXU_COLS, (c + 1) * MXU_COLS)
        val = x_ref[t * tm:(t + 1) * tm, cs] + _dot(m_ref[t], wout_ref[:, cs])
        n_grp = tm // SUBLANES
        n_lt = D_MODEL // LANES
        for ll in range(MXU_COLS // LANES):
            lt = c * (MXU_COLS // LANES) + ll
            for g in range(n_grp):
                k, j0 = divmod(SUBLANES * g, n_grp)
                o_ref[t * n_lt + lt, pl.ds(SUBLANES * j0 + k, SUBLANES, stride=SUBLANES), :] = (
                    val[SUBLANES * g:SUBLANES * (g + 1), ll * LANES:(ll + 1) * LANES])

    P = functools.partial

    def stage1(t):
        return [P(proj_u, t), P(proj_v, t), P(proj_q, t), P(proj_kv, t)]

    def stage2(t):
        sgu_items = [P(sgu, t, g) for g in range(A_GROUPS)]
        def attn(b, kv):
            attn_scores(t, b, kv)
            attn_values(t, b, kv)

        attn_items = [P(attn, b, kv) for b in range(blocks_per_tile) for kv in range(N_KV_HEADS)]
        items = []
        for i, item in enumerate(attn_items):
            items += sgu_items[i * A_GROUPS // len(attn_items):(i + 1) * A_GROUPS // len(attn_items)] + [item]
        return items

    def stage3(t):
        return [P(merge, t, c) for c in range(n_out)]

    def stage4(t):
        return [P(out_proj, t, c) for c in range(n_out)]

    def interleave(heavy, light):
        n, m = len(heavy), len(light)
        done = 0
        for i, item in enumerate(heavy):
            item()
            upto = (i + 1) * m // n
            for light_item in light[done:upto]:
                light_item()
            done = upto

    def run(items):
        for item in items:
            item()

    norm(0)
    run(stage1(0))
    for t in range(n_tiles):
        heavy = []
        if t >= 2:
            heavy += stage4(t - 2)
        if t >= 1:
            heavy += stage3(t - 1)
        if t + 1 < n_tiles:
            norm(t + 1)
            heavy += stage1(t + 1)
        interleave(heavy, stage2(t)) if heavy else run(stage2(t))
    if n_tiles >= 2:
        run(stage4(n_tiles - 2))
    run(stage3(n_tiles - 1))
    run(stage4(n_tiles - 1))

    rows_all = n_tiles * tm
    kd_ref[:, 0:BLOCK, :] = kd_ref[:, rows_all:rows_all + BLOCK, :]
    vd_ref[:, 0:BLOCK, :] = vd_ref[:, rows_all:rows_all + BLOCK, :]


def _resident(shape):
    nd = len(shape)
    return pl.BlockSpec(shape, lambda *_: (0,) * nd, pipeline_mode=pl.Buffered(1))


def _token_mixer(x2d, batch, seq, sinks, g_mix, w_in, g_sgu, w_s, b_full, bias_tbl, w_pa, w_pb, w_out,
                 w_up_f32, w_down_f32):
    tm = TILE_ROWS
    rows = MIX_TILES * tm
    ns = seq // rows
    n_steps = batch * ns
    n_lt = D_MODEL // LANES
    row_spec = pl.BlockSpec((rows, D_MODEL), lambda b, s: (b * ns + s, 0))
    up_slab = D_MODEL // n_steps
    down_slab = BF16_ROWS * (D_FF // F_CHUNK)
    n_down_slabs = D_FF // down_slab
    assert D_MODEL % n_steps == 0 and up_slab % BF16_ROWS == 0 and D_FF % down_slab == 0 and n_down_slabs <= n_steps
    up_spec = pl.BlockSpec((up_slab, 2 * D_FF), lambda b, s: (b * ns + s, 0))
    down_spec = pl.BlockSpec((down_slab, D_MODEL), lambda b, s: (jnp.minimum(b * ns + s, n_down_slabs - 1), 0))
    return pl.pallas_call(
        _mixer_kernel,
        grid=(batch, ns),
        in_specs=[
            pl.BlockSpec(memory_space=pltpu.SMEM),
            row_spec,
            _resident((1, D_MODEL)),
            _resident((D_MODEL, IN_DIM)),
            _resident((1, A_WIDTH)),
            _resident((A_GROUPS, CHUNK, CHUNK)),
            _resident((CHUNK, A_WIDTH)),
            _resident((N_HEADS, BLOCK, 2 * BLOCK)),
            _resident((A_WIDTH, D_MODEL)),
            _resident((Q_DIM, D_MODEL)),
            _resident((D_MODEL, D_MODEL)),
            up_spec,
            down_spec,
        ],
        out_specs=[pl.BlockSpec((MIX_TILES * n_lt, tm, LANES), lambda b, s: (b * ns + s, 0, 0)), up_spec, down_spec],
        out_shape=[jax.ShapeDtypeStruct((x2d.shape[0] // tm * n_lt, tm, LANES), F32),
                   jax.ShapeDtypeStruct(w_up_f32.shape, BF16),
                   jax.ShapeDtypeStruct(w_down_f32.shape, BF16)],
        scratch_shapes=[
            pltpu.VMEM((MIX_TILES, tm, D_MODEL), BF16),
            pltpu.VMEM((MIX_TILES, tm, A_WIDTH), F32),
            pltpu.VMEM((MIX_TILES, tm, A_WIDTH), BF16),
            pltpu.VMEM((MIX_TILES, tm, Q_DIM), BF16),
            pltpu.VMEM((MIX_TILES, tm, Q_DIM), BF16),
            pltpu.VMEM((N_KV_HEADS, BLOCK + rows, LANES), BF16),
            pltpu.VMEM((N_KV_HEADS, BLOCK + rows, LANES), BF16),
            pltpu.VMEM((MIX_TILES, tm, A_WIDTH), BF16),
            pltpu.VMEM((MIX_TILES, tm, Q_DIM), BF16),
            pltpu.VMEM((MIX_TILES, tm, D_MODEL), BF16),
        ],
        compiler_params=pltpu.CompilerParams(
            dimension_semantics=("arbitrary", "arbitrary"),
            vmem_limit_bytes=VMEM_LIMIT),
        name="token_mixer",
    )(sinks, x2d, g_mix, w_in, g_sgu, w_s, b_full, bias_tbl, w_pa, w_pb, w_out, w_up_f32, w_down_f32)


def _ffn_kernel(x_ref, gffn_ref, wup_ref, wconv_ref, bconv_ref, wdown_ref, gfin_ref, o_ref,
                skew_ref, h_ref, carry_ref, act_ref, *, steps_per_seq):
    tm = TILE_ROWS
    n_tiles = o_ref.shape[0] // tm
    n_grp = tm // SUBLANES
    pitch = n_grp + SUBLANES
    n_lane_tiles = D_MODEL // LANES
    n_chunks = D_FF // F_CHUNK
    n_down = D_MODEL // MXU_COLS

    @pl.when(pl.program_id(0) % steps_per_seq == 0)
    def _():
        carry_ref[...] = jnp.zeros(carry_ref.shape, F32)

    def slot(t):
        return t % FFN_SLOTS

    def x_cols(t, lt0, lt1):
        return jnp.concatenate([x_ref[t * n_lane_tiles + lt] for lt in range(lt0, lt1)], axis=1)

    def prologue(t):
        x = x_cols(t, 0, n_lane_tiles)
        r = lax.rsqrt(jnp.mean(x * x, axis=-1, keepdims=True) + EPS)
        h_ref[slot(t)] = (x * r * gffn_ref[...]).astype(BF16)

    first_sublane = lax.broadcasted_iota(jnp.int32, (SUBLANES, F_CHUNK), 0) == 0

    def conv(t, cols):
        up = _dot(h_ref[slot(t)], wup_ref[:, cols])
        last = up[tm - SUBLANES:tm]
        last2 = up[tm - 2 * SUBLANES:tm - SUBLANES]

        def wrap(cur, prev):
            return jnp.where(first_sublane, pltpu.roll(prev, 1, axis=0), pltpu.roll(cur, 1, axis=0))

        m1 = wrap(last, carry_ref[1, :, cols])
        m2 = wrap(last2, carry_ref[0, :, cols])
        carry_ref[0, :, cols] = last2
        carry_ref[1, :, cols] = last
        up1 = jnp.concatenate([m1, up[:tm - SUBLANES]], axis=0)
        up2 = jnp.concatenate([m2, m1, up[:tm - 2 * SUBLANES]], axis=0)
        return (bconv_ref[:, cols] + wconv_ref[2:3, cols] * up
                + wconv_ref[1:2, cols] * up1 + wconv_ref[0:1, cols] * up2)

    def up_chunk(t, c):
        gcols = slice(c * F_CHUNK, (c + 1) * F_CHUNK)
        gate = conv(t, gcols)
        val = conv(t, slice(D_FF + c * F_CHUNK, D_FF + (c + 1) * F_CHUNK))
        hg = 0.5 * gate
        act_ref[slot(t), :, gcols] = ((hg + hg * jnp.tanh(hg)) * val).astype(BF16)

    def unpermute_store(t, lt, vals):
        for j in range(n_grp):
            skew_ref[slot(t), lt, pl.ds(j, SUBLANES, stride=pitch), :] = vals[SUBLANES * j:SUBLANES * (j + 1), :]

    def down_block(t, j):
        cs = slice(j * MXU_COLS, (j + 1) * MXU_COLS)
        lts = MXU_COLS // LANES
        y = x_cols(t, j * lts, (j + 1) * lts) + _dot(act_ref[slot(t)], wdown_ref[:, cs])
        for ll in range(lts):
            unpermute_store(t, j * lts + ll, y[:, ll * LANES:(ll + 1) * LANES])
        return jnp.sum(y * y, axis=-1, keepdims=True)

    def epilogue(t, ssqs):
        rf = lax.rsqrt(sum(ssqs) * (1.0 / D_MODEL) + EPS)
        unpermute_store(t, n_lane_tiles, jnp.broadcast_to(rf, (tm, LANES)))
        for k in range(SUBLANES):
            blk = slice(pitch * k, pitch * k + n_grp)
            rf_nat = skew_ref[slot(t), n_lane_tiles, blk, :]
            for lt in range(n_lane_tiles):
                cols = slice(lt * LANES, (lt + 1) * LANES)
                o_ref[t * tm + n_grp * k:t * tm + n_grp * (k + 1), cols] = (
                    skew_ref[slot(t), lt, blk, :] * rf_nat * gfin_ref[:, cols])

    down_before = {k * n_chunks // n_down: k for k in range(n_down)}
    assert len(down_before) == n_down
    prologue(0)
    for t in range(n_tiles):
        if t + 1 < n_tiles:
            prologue(t + 1)
        ys = {}
        for c in range(n_chunks):
            if t >= 1 and c in down_before:
                ys[down_before[c]] = down_block(t - 1, down_before[c])
            up_chunk(t, c)
        if t >= 1:
            epilogue(t - 1, [ys[j] for j in range(n_down)])
    last = n_tiles - 1
    epilogue(last, [down_block(last, j) for j in range(n_down)])


def _conv_ffn(x2d, batch, seq, g_ffn, w_up, w_conv, b_conv, w_down, g_final):
    rows = FFN_TILES * TILE_ROWS
    steps_per_seq = seq // rows
    row_spec = pl.BlockSpec((rows, D_MODEL), lambda i: (i, 0))
    skew_rows = TILE_ROWS + SUBLANES * SUBLANES
    return pl.pallas_call(
        functools.partial(_ffn_kernel, steps_per_seq=steps_per_seq),
        grid=(batch * steps_per_seq,),
        in_specs=[
            pl.BlockSpec((FFN_TILES * (D_MODEL // LANES), TILE_ROWS, LANES), lambda i: (i, 0, 0)),
            _resident((1, D_MODEL)),
            _resident((D_MODEL, 2 * D_FF)),
            _resident((CONV_WIDTH, 2 * D_FF)),
            _resident((1, 2 * D_FF)),
            _resident((D_FF, D_MODEL)),
            _resident((1, D_MODEL)),
        ],
        out_specs=row_spec,
        out_shape=jax.ShapeDtypeStruct((x2d.shape[0] // (D_MODEL // LANES) * TILE_ROWS, D_MODEL), F32),
        scratch_shapes=[
            pltpu.VMEM((FFN_SLOTS, D_MODEL // LANES + 1, skew_rows, LANES), F32),
            pltpu.VMEM((FFN_SLOTS, TILE_ROWS, D_MODEL), BF16),
            pltpu.VMEM((CONV_WIDTH - 1, SUBLANES, 2 * D_FF), F32),
            pltpu.VMEM((FFN_SLOTS, TILE_ROWS, D_FF), BF16),
        ],
        compiler_params=pltpu.CompilerParams(
            dimension_semantics=("arbitrary",),
            vmem_limit_bytes=VMEM_LIMIT),
        name="conv_ffn",
    )(x2d, g_ffn, w_up, w_conv, b_conv, w_down, g_final)


def kernel(x, g_mix, w_in, g_sgu, w_s, b_s, sinks, rel_bias, w_pa, w_pb, w_out,
           g_ffn, w_up, w_conv, b_conv, w_down, g_final):
    batch, seq, d = x.shape
    assert w_in.shape[0] == 1 and d == D_MODEL and seq % (max(FFN_TILES, MIX_TILES) * TILE_ROWS) == 0 and w_in.shape[2] == IN_DIM
    bias_tbl, (w_in_bf16, w_pa_bf16, w_pb_bf16, w_out_bf16) = _prep(
        rel_bias, (w_in[0], w_pa[0], w_pb[0], w_out[0]),
        halved_cols=((OFF_GA, IN_DIM), (0, D_MODEL), (0, D_MODEL), None))
    x2d = x.reshape(batch * seq, d)
    b_full = jnp.repeat(jnp.transpose(b_s[0]), A_WIDTH // A_GROUPS, axis=1)
    x2d, w_up_bf16, w_down_bf16 = _token_mixer(
        x2d, batch, seq, sinks[0], g_mix[0][None, :], w_in_bf16, g_sgu[0][None, :],
        w_s[0], b_full, bias_tbl, w_pa_bf16, w_pb_bf16, w_out_bf16,
        w_up[0], w_down[0])
    x2d = _conv_ffn(
        x2d, batch, seq, g_ffn[0][None, :], w_up_bf16, w_conv[0], b_conv[0][None, :],
        w_down_bf16, g_final[None, :])
    return x2d.reshape(batch, seq, d)
```

```python
import functools
import math

import jax
import jax.numpy as jnp
import numpy as np
from jax import lax
from jax.experimental import pallas as pl
from jax.experimental.pallas import tpu as pltpu

D_MODEL = 1024
CHUNK = 128
A_GROUPS = 4
A_WIDTH = 512
N_HEADS = 8
N_KV_HEADS = 2
HEAD_DIM = 64
Q_DIM = N_HEADS * HEAD_DIM
KV_DIM = N_KV_HEADS * HEAD_DIM
WINDOW = 128
BLOCK = 128
N_BUCKETS = 32
MAX_DISTANCE = 128
D_FF = 2816
CONV_WIDTH = 3
EPS = 1e-6
NEG_INF = -1e30
LOG2E = math.log2(math.e)

OFF_PU = 0
OFF_PV = OFF_PU + A_WIDTH
OFF_Q = OFF_PV + A_WIDTH
OFF_K = OFF_Q + Q_DIM
OFF_V = OFF_K + KV_DIM
OFF_GA = OFF_V + KV_DIM
OFF_GB = OFF_GA + D_MODEL
IN_DIM = OFF_GB + D_MODEL

LANES = 128
SUBLANES = 8
MXU_COLS = 256
TILE_ROWS = 512
F_CHUNK = 256
FFN_TILES = 2
FFN_SLOTS = 2
MIX_TILES = 2
VMEM_BYTES = 64 * 1024 * 1024
VMEM_LIMIT = VMEM_BYTES - 8 * 1024 * 1024
BF16_ROWS = 2 * SUBLANES

F32 = jnp.float32
BF16 = jnp.bfloat16


def _gelu_tanh(x):
    c = math.sqrt(2.0 / math.pi)
    hx = 0.5 * x
    return hx + hx * jnp.tanh(x * (c + (c * 0.044715) * (x * x)))


def _sigmoid(x):
    return 1.0 / (1.0 + jnp.exp(-x))


def _dot(a, b):
    return jnp.dot(a, b, preferred_element_type=F32)


def _band_buckets():
    assert WINDOW == BLOCK
    i = np.arange(BLOCK)[:, None]
    j = np.arange(BLOCK)[None, :]
    d = np.where(j > i, i + BLOCK - j, i - j)
    max_exact = N_BUCKETS // 2
    large = max_exact + (np.log(np.maximum(d, 1) / max_exact) / np.log(MAX_DISTANCE / max_exact)
                         * (N_BUCKETS - max_exact)).astype(np.int32)
    large = np.minimum(large, N_BUCKETS - 1)
    return np.where(d < max_exact, d, large).astype(np.int32)


def _prep_kernel(rb_ref, bucket_ref, *refs):
    n_w = (len(refs) - 1) // 2
    w_refs, o_ref, wb_refs = refs[:n_w], refs[n_w], refs[n_w + 1:]
    h = pl.program_id(0)
    bucket = bucket_ref[...]
    acc = jnp.zeros(bucket.shape, F32)
    for b in range(N_BUCKETS):
        acc = jnp.where(bucket == b, rb_ref[b, h], acc)
    o_ref[...] = acc * LOG2E
    for w_ref, wb_ref in zip(w_refs, wb_refs):
        wb_ref[...] = w_ref[...].astype(BF16)


def _prep(rel_bias, weights):
    buckets = jnp.asarray(_band_buckets())
    slab_specs = []
    for w in weights:
        assert w.shape[0] % (N_HEADS * BF16_ROWS) == 0
        slab_specs.append(pl.BlockSpec((w.shape[0] // N_HEADS, w.shape[1]), lambda h: (h, 0)))
    outs = pl.pallas_call(
        _prep_kernel,
        grid=(N_HEADS,),
        in_specs=[
            pl.BlockSpec(memory_space=pltpu.SMEM),
            pl.BlockSpec((BLOCK, BLOCK), lambda h: (0, 0)),
        ] + slab_specs,
        out_specs=[pl.BlockSpec((None, BLOCK, BLOCK), lambda h: (h, 0, 0))] + slab_specs,
        out_shape=[jax.ShapeDtypeStruct((N_HEADS, BLOCK, BLOCK), F32)]
        + [jax.ShapeDtypeStruct(w.shape, BF16) for w in weights],
        name="rel_bias_table",
    )(rel_bias, buckets, *weights)
    return outs[0], outs[1:]


def _mixer_kernel(sinks_ref, x_ref, gmix_ref, win_ref, gsgu_ref, ws_ref, bfull_ref, bias_ref,
                  wpa_ref, wpb_ref, wout_ref, wup_f32_ref, wdown_f32_ref, o_ref, wup_bf16_ref, wdown_bf16_ref,
                  h_ref, u_ref, vn_ref, qlo_ref, qhi_ref, kd_ref, vd_ref, ya_ref, yb_ref, m_ref):
    wup_bf16_ref[...] = wup_f32_ref[...].astype(BF16)
    wdown_bf16_ref[...] = wdown_f32_ref[...].astype(BF16)

    tm = TILE_ROWS
    n_tiles = x_ref.shape[0] // tm
    blocks_per_tile = tm // BLOCK
    s_idx = pl.program_id(1)
    heads_per_kv = N_HEADS // N_KV_HEADS
    n_out = D_MODEL // MXU_COLS

    @pl.when(s_idx == 0)
    def _():
        kd_ref[:, 0:BLOCK, :] = jnp.zeros((N_KV_HEADS, BLOCK, LANES), BF16)
        vd_ref[:, 0:BLOCK, :] = jnp.zeros((N_KV_HEADS, BLOCK, LANES), BF16)

    row_c = lax.broadcasted_iota(jnp.int32, (CHUNK, CHUNK), 0)
    col_c = lax.broadcasted_iota(jnp.int32, (CHUNK, CHUNK), 1)
    tril = col_c <= row_c
    wmask = [jnp.where(tril, ws_ref[g], 0.0).astype(BF16) for g in range(A_GROUPS)]
    from_prev = (lax.broadcasted_iota(jnp.int32, (BLOCK, BLOCK), 1)
                 > lax.broadcasted_iota(jnp.int32, (BLOCK, BLOCK), 0))
    no_prev = from_prev & (s_idx == 0)
    b_is_lo = lax.broadcasted_iota(jnp.int32, (BLOCK, LANES), 1) < HEAD_DIM
    q_is_lo = (lax.broadcasted_iota(jnp.int32, (tm, Q_DIM), 1) % LANES) < HEAD_DIM
    t_is_lo = lax.broadcasted_iota(jnp.int32, (tm, LANES), 1) < HEAD_DIM

    def norm(t):
        x = x_ref[t * tm:(t + 1) * tm, :]
        r = lax.rsqrt(jnp.mean(x * x, axis=-1, keepdims=True) + EPS)
        h_ref[t] = (x * r * gmix_ref[...]).astype(BF16)

    def proj_u(t):
        u_ref[t] = _gelu_tanh(_dot(h_ref[t], win_ref[:, OFF_PU:OFF_PU + A_WIDTH]))

    def proj_v(t):
        pv = _gelu_tanh(_dot(h_ref[t], win_ref[:, OFF_PV:OFF_PV + A_WIDTH]))
        rv = lax.rsqrt(jnp.mean(pv * pv, axis=-1, keepdims=True) + EPS)
        vn_ref[t] = (pv * rv * gsgu_ref[...]).astype(BF16)

    def proj_q(t):
        q = _dot(h_ref[t], win_ref[:, OFF_Q:OFF_Q + Q_DIM])
        qlo_ref[t] = jnp.where(q_is_lo, q, 0.0).astype(BF16)
        qhi_ref[t] = jnp.where(q_is_lo, 0.0, q).astype(BF16)

    def proj_kv(t):
        kvp = _dot(h_ref[t], win_ref[:, OFF_K:OFF_K + 2 * KV_DIM])
        k = kvp[:, :KV_DIM] * (HEAD_DIM ** -0.5 * LOG2E)
        v = kvp[:, KV_DIM:]
        k_sw = pltpu.roll(k, HEAD_DIM, axis=1)
        v_sw = pltpu.roll(v, HEAD_DIM, axis=1)
        rows = slice(BLOCK + t * tm, BLOCK + (t + 1) * tm)
        kd_ref[0, rows, :] = jnp.where(t_is_lo, k, k_sw).astype(BF16)
        kd_ref[1, rows, :] = jnp.where(t_is_lo, k_sw, k).astype(BF16)
        vd_ref[0, rows, :] = jnp.where(t_is_lo, v, v_sw).astype(BF16)
        vd_ref[1, rows, :] = jnp.where(t_is_lo, v_sw, v).astype(BF16)

    def sgu(t, g):
        cols = slice(g * CHUNK, (g + 1) * CHUNK)
        v_chunks = [vn_ref[t, b * CHUNK:(b + 1) * CHUNK, cols] for b in range(tm // CHUNK)]
        s_all = _dot(wmask[g], jnp.concatenate(v_chunks, axis=1))
        for b in range(tm // CHUNK):
            rows = slice(b * CHUNK, (b + 1) * CHUNK)
            s = s_all[:, b * CHUNK:(b + 1) * CHUNK] + bfull_ref[:, cols]
            ya_ref[t, rows, cols] = (u_ref[t, rows, cols] * s).astype(BF16)

    probs = {}

    def attn_scores(t, b, kv):
        rows = slice(b * BLOCK, (b + 1) * BLOCK)
        bb = t * blocks_per_tile + b
        q_parts = []
        for g in range(heads_per_kv):
            hh = kv * heads_per_kv + g
            src = qlo_ref if hh % 2 == 0 else qhi_ref
            c = hh // 2
            q_parts.append(src[t, rows, c * LANES:(c + 1) * LANES])
        q_stack = jnp.concatenate(q_parts, axis=0)
        keys = kd_ref[kv, bb * BLOCK:(bb + 2) * BLOCK, :]
        sc = lax.dot_general(q_stack, keys, (((1,), (1,)), ((), ())),
                             preferred_element_type=F32)
        p_parts = []
        inv_l = []
        for g in range(heads_per_kv):
            hh = kv * heads_per_kv + g
            sh = sc[g * BLOCK:(g + 1) * BLOCK]
            sg = jnp.where(from_prev, sh[:, :BLOCK], sh[:, BLOCK:]) + bias_ref[hh]
            if bb == 0:
                sg = jnp.where(no_prev, NEG_INF, sg)
            sink = sinks_ref[hh] * LOG2E
            m = jnp.maximum(jnp.max(sg, axis=-1, keepdims=True), sink)
            pf = jnp.exp2(sg - m)
            l = jnp.sum(pf, axis=-1, keepdims=True) + jnp.exp2(sink - m)
            p = pf.astype(BF16)
            zero = jnp.zeros_like(p)
            p_parts.append(jnp.concatenate([jnp.where(from_prev, p, zero), jnp.where(from_prev, zero, p)], axis=1))
            inv_l.append(1.0 / l)
        probs[t, b, kv] = (jnp.concatenate(p_parts, axis=0), inv_l)

    def attn_values(t, b, kv):
        rows = slice(b * BLOCK, (b + 1) * BLOCK)
        bb = t * blocks_per_tile + b
        p_stack, inv_l = probs.pop((t, b, kv))
        vals = vd_ref[kv, bb * BLOCK:(bb + 2) * BLOCK, :]
        o = _dot(p_stack, vals)
        for cc in range(heads_per_kv // 2):
            o_even = o[(2 * cc) * BLOCK:(2 * cc + 1) * BLOCK] * inv_l[2 * cc]
            o_odd = o[(2 * cc + 1) * BLOCK:(2 * cc + 2) * BLOCK] * inv_l[2 * cc + 1]
            col = kv * (heads_per_kv // 2) + cc
            yb_ref[t, rows, col * LANES:(col + 1) * LANES] = jnp.where(b_is_lo, o_even, o_odd).astype(BF16)

    def merge(t, c):
        cs = slice(c * MXU_COLS, (c + 1) * MXU_COLS)
        ga = _dot(h_ref[t], win_ref[:, OFF_GA + c * MXU_COLS:OFF_GA + (c + 1) * MXU_COLS])
        gb = _dot(h_ref[t], win_ref[:, OFF_GB + c * MXU_COLS:OFF_GB + (c + 1) * MXU_COLS])
        pa = _dot(ya_ref[t], wpa_ref[:, cs])
        pb = _dot(yb_ref[t], wpb_ref[:, cs])
        m_ref[t, :, cs] = (_sigmoid(ga) * pa + _sigmoid(gb) * pb).astype(BF16)

    def out_proj(t, c):
        cs = slice(c * MXU_COLS, (c + 1) * MXU_COLS)
        val = x_ref[t * tm:(t + 1) * tm, cs] + _dot(m_ref[t], wout_ref[:, cs])
        n_grp = tm // SUBLANES
        n_lt = D_MODEL // LANES
        for ll in range(MXU_COLS // LANES):
            lt = c * (MXU_COLS // LANES) + ll
            for g in range(n_grp):
                k, j0 = divmod(SUBLANES * g, n_grp)
                o_ref[t * n_lt + lt, pl.ds(SUBLANES * j0 + k, SUBLANES, stride=SUBLANES), :] = (
                    val[SUBLANES * g:SUBLANES * (g + 1), ll * LANES:(ll + 1) * LANES])

    P = functools.partial

    def stage1(t):
        return [P(proj_u, t), P(proj_v, t), P(proj_q, t), P(proj_kv, t)]

    def stage2(t):
        sgu_items = [P(sgu, t, g) for g in range(A_GROUPS)]
        def attn(b, kv):
            attn_scores(t, b, kv)
            attn_values(t, b, kv)

        attn_items = [P(attn, b, kv) for b in range(blocks_per_tile) for kv in range(N_KV_HEADS)]
        items = []
        for i, item in enumerate(attn_items):
            items += sgu_items[i * A_GROUPS // len(attn_items):(i + 1) * A_GROUPS // len(attn_items)] + [item]
        return items

    def stage3(t):
        return [P(merge, t, c) for c in range(n_out)]

    def stage4(t):
        return [P(out_proj, t, c) for c in range(n_out)]

    def interleave(heavy, light):
        n, m = len(heavy), len(light)
        done = 0
        for i, item in enumerate(heavy):
            item()
            upto = (i + 1) * m // n
            for light_item in light[done:upto]:
                light_item()
            done = upto

    def run(items):
        for item in items:
            item()

    norm(0)
    run(stage1(0))
    for t in range(n_tiles):
        heavy = []
        if t >= 2:
            heavy += stage4(t - 2)
        if t >= 1:
            heavy += stage3(t - 1)
        if t + 1 < n_tiles:
            norm(t + 1)
            heavy += stage1(t + 1)
        interleave(heavy, stage2(t)) if heavy else run(stage2(t))
    if n_tiles >= 2:
        run(stage4(n_tiles - 2))
    run(stage3(n_tiles - 1))
    run(stage4(n_tiles - 1))

    rows_all = n_tiles * tm
    kd_ref[:, 0:BLOCK, :] = kd_ref[:, rows_all:rows_all + BLOCK, :]
    vd_ref[:, 0:BLOCK, :] = vd_ref[:, rows_all:rows_all + BLOCK, :]


def _resident(shape):
    nd = len(shape)
    return pl.BlockSpec(shape, lambda *_: (0,) * nd, pipeline_mode=pl.Buffered(1))


def _token_mixer(x2d, batch, seq, sinks, g_mix, w_in, g_sgu, w_s, b_full, bias_tbl, w_pa, w_pb, w_out,
                 w_up_f32, w_down_f32):
    tm = TILE_ROWS
    rows = MIX_TILES * tm
    ns = seq // rows
    n_steps = batch * ns
    n_lt = D_MODEL // LANES
    row_spec = pl.BlockSpec((rows, D_MODEL), lambda b, s: (b * ns + s, 0))
    up_slab = D_MODEL // n_steps
    down_slab = BF16_ROWS * (D_FF // F_CHUNK)
    n_down_slabs = D_FF // down_slab
    assert D_MODEL % n_steps == 0 and up_slab % BF16_ROWS == 0 and D_FF % down_slab == 0 and n_down_slabs <= n_steps
    up_spec = pl.BlockSpec((up_slab, 2 * D_FF), lambda b, s: (b * ns + s, 0))
    down_spec = pl.BlockSpec((down_slab, D_MODEL), lambda b, s: (jnp.minimum(b * ns + s, n_down_slabs - 1), 0))
    return pl.pallas_call(
        _mixer_kernel,
        grid=(batch, ns),
        in_specs=[
            pl.BlockSpec(memory_space=pltpu.SMEM),
            row_spec,
            _resident((1, D_MODEL)),
            _resident((D_MODEL, IN_DIM)),
            _resident((1, A_WIDTH)),
            _resident((A_GROUPS, CHUNK, CHUNK)),
            _resident((CHUNK, A_WIDTH)),
            _resident((N_HEADS, BLOCK, BLOCK)),
            _resident((A_WIDTH, D_MODEL)),
            _resident((Q_DIM, D_MODEL)),
            _resident((D_MODEL, D_MODEL)),
            up_spec,
            down_spec,
        ],
        out_specs=[pl.BlockSpec((MIX_TILES * n_lt, tm, LANES), lambda b, s: (b * ns + s, 0, 0)), up_spec, down_spec],
        out_shape=[jax.ShapeDtypeStruct((x2d.shape[0] // tm * n_lt, tm, LANES), F32),
                   jax.ShapeDtypeStruct(w_up_f32.shape, BF16),
                   jax.ShapeDtypeStruct(w_down_f32.shape, BF16)],
        scratch_shapes=[
            pltpu.VMEM((MIX_TILES, tm, D_MODEL), BF16),
            pltpu.VMEM((MIX_TILES, tm, A_WIDTH), F32),
            pltpu.VMEM((MIX_TILES, tm, A_WIDTH), BF16),
            pltpu.VMEM((MIX_TILES, tm, Q_DIM), BF16),
            pltpu.VMEM((MIX_TILES, tm, Q_DIM), BF16),
            pltpu.VMEM((N_KV_HEADS, BLOCK + rows, LANES), BF16),
            pltpu.VMEM((N_KV_HEADS, BLOCK + rows, LANES), BF16),
            pltpu.VMEM((MIX_TILES, tm, A_WIDTH), BF16),
            pltpu.VMEM((MIX_TILES, tm, Q_DIM), BF16),
            pltpu.VMEM((MIX_TILES, tm, D_MODEL), BF16),
        ],
        compiler_params=pltpu.CompilerParams(
            dimension_semantics=("arbitrary", "arbitrary"),
            vmem_limit_bytes=VMEM_LIMIT),
        name="token_mixer",
    )(sinks, x2d, g_mix, w_in, g_sgu, w_s, b_full, bias_tbl, w_pa, w_pb, w_out, w_up_f32, w_down_f32)


def _ffn_kernel(x_ref, gffn_ref, wup_ref, wconv_ref, bconv_ref, wdown_ref, gfin_ref, o_ref,
                skew_ref, h_ref, carry_ref, act_ref, *, steps_per_seq):
    tm = TILE_ROWS
    n_tiles = o_ref.shape[0] // tm
    n_grp = tm // SUBLANES
    pitch = n_grp + SUBLANES
    n_lane_tiles = D_MODEL // LANES
    n_chunks = D_FF // F_CHUNK
    n_down = D_MODEL // MXU_COLS

    @pl.when(pl.program_id(0) % steps_per_seq == 0)
    def _():
        carry_ref[...] = jnp.zeros(carry_ref.shape, F32)

    def slot(t):
        return t % FFN_SLOTS

    def x_cols(t, lt0, lt1):
        return jnp.concatenate([x_ref[t * n_lane_tiles + lt] for lt in range(lt0, lt1)], axis=1)

    def prologue(t):
        x = x_cols(t, 0, n_lane_tiles)
        r = lax.rsqrt(jnp.mean(x * x, axis=-1, keepdims=True) + EPS)
        h_ref[slot(t)] = (x * r * gffn_ref[...]).astype(BF16)

    first_sublane = lax.broadcasted_iota(jnp.int32, (SUBLANES, F_CHUNK), 0) == 0

    def conv(t, cols):
        up = _dot(h_ref[slot(t)], wup_ref[:, cols])
        last = up[tm - SUBLANES:tm]
        last2 = up[tm - 2 * SUBLANES:tm - SUBLANES]

        def wrap(cur, prev):
            return jnp.where(first_sublane, pltpu.roll(prev, 1, axis=0), pltpu.roll(cur, 1, axis=0))

        m1 = wrap(last, carry_ref[1, :, cols])
        m2 = wrap(last2, carry_ref[0, :, cols])
        carry_ref[0, :, cols] = last2
        carry_ref[1, :, cols] = last
        up1 = jnp.concatenate([m1, up[:tm - SUBLANES]], axis=0)
        up2 = jnp.concatenate([m2, m1, up[:tm - 2 * SUBLANES]], axis=0)
        return (bconv_ref[:, cols] + wconv_ref[2:3, cols] * up
                + wconv_ref[1:2, cols] * up1 + wconv_ref[0:1, cols] * up2)

    def up_chunk(t, c):
        gcols = slice(c * F_CHUNK, (c + 1) * F_CHUNK)
        gate = conv(t, gcols)
        val = conv(t, slice(D_FF + c * F_CHUNK, D_FF + (c + 1) * F_CHUNK))
        hg = 0.5 * gate
        act_ref[slot(t), :, gcols] = ((hg + hg * jnp.tanh(hg)) * val).astype(BF16)

    def unpermute_store(t, lt, vals):
        for j in range(n_grp):
            skew_ref[slot(t), lt, pl.ds(j, SUBLANES, stride=pitch), :] = vals[SUBLANES * j:SUBLANES * (j + 1), :]

    def down_block(t, j):
        cs = slice(j * MXU_COLS, (j + 1) * MXU_COLS)
        lts = MXU_COLS // LANES
        y = x_cols(t, j * lts, (j + 1) * lts) + _dot(act_ref[slot(t)], wdown_ref[:, cs])
        for ll in range(lts):
            unpermute_store(t, j * lts + ll, y[:, ll * LANES:(ll + 1) * LANES])
        return jnp.sum(y * y, axis=-1, keepdims=True)

    def epilogue(t, ssqs):
        rf = lax.rsqrt(sum(ssqs) * (1.0 / D_MODEL) + EPS)
        unpermute_store(t, n_lane_tiles, jnp.broadcast_to(rf, (tm, LANES)))
        for k in range(SUBLANES):
            blk = slice(pitch * k, pitch * k + n_grp)
            rf_nat = skew_ref[slot(t), n_lane_tiles, blk, :]
            for lt in range(n_lane_tiles):
                cols = slice(lt * LANES, (lt + 1) * LANES)
                o_ref[t * tm + n_grp * k:t * tm + n_grp * (k + 1), cols] = (
                    skew_ref[slot(t), lt, blk, :] * rf_nat * gfin_ref[:, cols])

    down_before = {k * n_chunks // n_down: k for k in range(n_down)}
    assert len(down_before) == n_down
    prologue(0)
    for t in range(n_tiles):
        if t + 1 < n_tiles:
            prologue(t + 1)
        ys = {}
        for c in range(n_chunks):
            if t >= 1 and c in down_before:
                ys[down_before[c]] = down_block(t - 1, down_before[c])
            up_chunk(t, c)
        if t >= 1:
            epilogue(t - 1, [ys[j] for j in range(n_down)])
    last = n_tiles - 1
    epilogue(last, [down_block(last, j) for j in range(n_down)])


def _conv_ffn(x2d, batch, seq, g_ffn, w_up, w_conv, b_conv, w_down, g_final):
    rows = FFN_TILES * TILE_ROWS
    steps_per_seq = seq // rows
    row_spec = pl.BlockSpec((rows, D_MODEL), lambda i: (i, 0))
    skew_rows = TILE_ROWS + SUBLANES * SUBLANES
    return pl.pallas_call(
        functools.partial(_ffn_kernel, steps_per_seq=steps_per_seq),
        grid=(batch * steps_per_seq,),
        in_specs=[
            pl.BlockSpec((FFN_TILES * (D_MODEL // LANES), TILE_ROWS, LANES), lambda i: (i, 0, 0)),
            _resident((1, D_MODEL)),
            _resident((D_MODEL, 2 * D_FF)),
            _resident((CONV_WIDTH, 2 * D_FF)),
            _resident((1, 2 * D_FF)),
            _resident((D_FF, D_MODEL)),
            _resident((1, D_MODEL)),
        ],
        out_specs=row_spec,
        out_shape=jax.ShapeDtypeStruct((x2d.shape[0] // (D_MODEL // LANES) * TILE_ROWS, D_MODEL), F32),
        scratch_shapes=[
            pltpu.VMEM((FFN_SLOTS, D_MODEL // LANES + 1, skew_rows, LANES), F32),
            pltpu.VMEM((FFN_SLOTS, TILE_ROWS, D_MODEL), BF16),
            pltpu.VMEM((CONV_WIDTH - 1, SUBLANES, 2 * D_FF), F32),
            pltpu.VMEM((FFN_SLOTS, TILE_ROWS, D_FF), BF16),
        ],
        compiler_params=pltpu.CompilerParams(
            dimension_semantics=("arbitrary",),
            vmem_limit_bytes=VMEM_LIMIT),
        name="conv_ffn",
    )(x2d, g_ffn, w_up, w_conv, b_conv, w_down, g_final)


def kernel(x, g_mix, w_in, g_sgu, w_s, b_s, sinks, rel_bias, w_pa, w_pb, w_out,
           g_ffn, w_up, w_conv, b_conv, w_down, g_final):
    batch, seq, d = x.shape
    assert w_in.shape[0] == 1 and d == D_MODEL and seq % (max(FFN_TILES, MIX_TILES) * TILE_ROWS) == 0 and w_in.shape[2] == IN_DIM
    bias_tbl, (w_in_bf16, w_pa_bf16, w_pb_bf16, w_out_bf16) = _prep(
        rel_bias, (w_in[0], w_pa[0], w_pb[0], w_out[0]))
    x2d = x.reshape(batch * seq, d)
    b_full = jnp.repeat(jnp.transpose(b_s[0]), A_WIDTH // A_GROUPS, axis=1)
    x2d, w_up_bf16, w_down_bf16 = _token_mixer(
        x2d, batch, seq, sinks[0], g_mix[0][None, :], w_in_bf16, g_sgu[0][None, :],
        w_s[0], b_full, bias_tbl, w_pa_bf16, w_pb_bf16, w_out_bf16,
        w_up[0], w_down[0])
    x2d = _conv_ffn(
        x2d, batch, seq, g_ffn[0][None, :], w_up_bf16, w_conv[0], b_conv[0][None, :],
        w_down_bf16, g_final[None, :])
    return x2d.reshape(batch, seq, d)
```

```python
import functools
import math

import jax
import jax.numpy as jnp
import numpy as np
from jax import lax
from jax.experimental import pallas as pl
from jax.experimental.pallas import tpu as pltpu

D_MODEL = 1024
CHUNK = 128
A_GROUPS = 4
A_WIDTH = 512
N_HEADS = 8
N_KV_HEADS = 2
HEAD_DIM = 64
Q_DIM = N_HEADS * HEAD_DIM
KV_DIM = N_KV_HEADS * HEAD_DIM
WINDOW = 128
BLOCK = 128
N_BUCKETS = 32
MAX_DISTANCE = 128
D_FF = 2816
CONV_WIDTH = 3
EPS = 1e-6
NEG_INF = -1e30
LOG2E = math.log2(math.e)

OFF_PU = 0
OFF_PV = OFF_PU + A_WIDTH
OFF_Q = OFF_PV + A_WIDTH
OFF_K = OFF_Q + Q_DIM
OFF_V = OFF_K + KV_DIM
OFF_GA = OFF_V + KV_DIM
OFF_GB = OFF_GA + D_MODEL
IN_DIM = OFF_GB + D_MODEL

LANES = 128
SUBLANES = 8
MXU_COLS = 256
TILE_ROWS = 512
F_CHUNK = 256
FFN_TILES = 2
FFN_SLOTS = 2
MIX_TILES = 2
VMEM_BYTES = 64 * 1024 * 1024
VMEM_LIMIT = VMEM_BYTES - 8 * 1024 * 1024
BF16_ROWS = 2 * SUBLANES

F32 = jnp.float32
BF16 = jnp.bfloat16


def _gelu_tanh_x2(x):
    c = math.sqrt(2.0 / math.pi)
    return x + x * jnp.tanh(x * (c + (c * 0.044715) * (x * x)))


def _sigmoid(x):
    return 1.0 / (1.0 + jnp.exp(-x))


def _dot(a, b):
    return jnp.dot(a, b, preferred_element_type=F32)


def _band_buckets():
    assert WINDOW == BLOCK
    i = np.arange(BLOCK)[:, None]
    j = np.arange(BLOCK)[None, :]
    d = np.where(j > i, i + BLOCK - j, i - j)
    max_exact = N_BUCKETS // 2
    large = max_exact + (np.log(np.maximum(d, 1) / max_exact) / np.log(MAX_DISTANCE / max_exact)
                         * (N_BUCKETS - max_exact)).astype(np.int32)
    large = np.minimum(large, N_BUCKETS - 1)
    return np.where(d < max_exact, d, large).astype(np.int32)


def _prep_kernel(rb_ref, bucket_ref, *refs):
    n_w = (len(refs) - 1) // 2
    w_refs, o_ref, wb_refs = refs[:n_w], refs[n_w], refs[n_w + 1:]
    h = pl.program_id(0)
    bucket = bucket_ref[...]
    acc = jnp.zeros(bucket.shape, F32)
    for b in range(N_BUCKETS):
        acc = jnp.where(bucket == b, rb_ref[b, h], acc)
    o_ref[...] = acc * LOG2E
    for w_ref, wb_ref in zip(w_refs, wb_refs):
        wb_ref[...] = w_ref[...].astype(BF16)


def _prep(rel_bias, weights):
    buckets = jnp.asarray(_band_buckets())
    slab_specs = []
    for w in weights:
        assert w.shape[0] % (N_HEADS * BF16_ROWS) == 0
        slab_specs.append(pl.BlockSpec((w.shape[0] // N_HEADS, w.shape[1]), lambda h: (h, 0)))
    outs = pl.pallas_call(
        _prep_kernel,
        grid=(N_HEADS,),
        in_specs=[
            pl.BlockSpec(memory_space=pltpu.SMEM),
            pl.BlockSpec((BLOCK, BLOCK), lambda h: (0, 0)),
        ] + slab_specs,
        out_specs=[pl.BlockSpec((None, BLOCK, BLOCK), lambda h: (h, 0, 0))] + slab_specs,
        out_shape=[jax.ShapeDtypeStruct((N_HEADS, BLOCK, BLOCK), F32)]
        + [jax.ShapeDtypeStruct(w.shape, BF16) for w in weights],
        name="rel_bias_table",
    )(rel_bias, buckets, *weights)
    return outs[0], outs[1:]


def _mixer_kernel(sinks_ref, x_ref, gmix_ref, win_ref, gsgu_ref, ws_ref, bfull_ref, bias_ref,
                  wpa_ref, wpb_ref, wout_ref, wup_f32_ref, wdown_f32_ref, o_ref, wup_bf16_ref, wdown_bf16_ref,
                  h_ref, u_ref, vn_ref, qlo_ref, qhi_ref, kd_ref, vd_ref, ya_ref, yb_ref, m_ref):
    wup_bf16_ref[...] = wup_f32_ref[...].astype(BF16)
    wdown_bf16_ref[...] = wdown_f32_ref[...].astype(BF16)

    tm = TILE_ROWS
    n_tiles = x_ref.shape[0] // tm
    blocks_per_tile = tm // BLOCK
    s_idx = pl.program_id(1)
    heads_per_kv = N_HEADS // N_KV_HEADS
    n_out = D_MODEL // MXU_COLS

    @pl.when(s_idx == 0)
    def _():
        kd_ref[:, 0:BLOCK, :] = jnp.zeros((N_KV_HEADS, BLOCK, LANES), BF16)
        vd_ref[:, 0:BLOCK, :] = jnp.zeros((N_KV_HEADS, BLOCK, LANES), BF16)

    row_c = lax.broadcasted_iota(jnp.int32, (CHUNK, CHUNK), 0)
    col_c = lax.broadcasted_iota(jnp.int32, (CHUNK, CHUNK), 1)
    tril = col_c <= row_c
    wmask = [jnp.where(tril, 0.5 * ws_ref[g], 0.0).astype(BF16) for g in range(A_GROUPS)]
    from_prev = (lax.broadcasted_iota(jnp.int32, (BLOCK, BLOCK), 1)
                 > lax.broadcasted_iota(jnp.int32, (BLOCK, BLOCK), 0))
    no_prev = from_prev & (s_idx == 0)
    b_is_lo = lax.broadcasted_iota(jnp.int32, (BLOCK, LANES), 1) < HEAD_DIM
    q_is_lo = (lax.broadcasted_iota(jnp.int32, (tm, Q_DIM), 1) % LANES) < HEAD_DIM
    t_is_lo = lax.broadcasted_iota(jnp.int32, (tm, LANES), 1) < HEAD_DIM

    def norm(t):
        x = x_ref[t * tm:(t + 1) * tm, :]
        r = lax.rsqrt(jnp.mean(x * x, axis=-1, keepdims=True) + EPS)
        h_ref[t] = (x * r * gmix_ref[...]).astype(BF16)

    def proj_u(t):
        u_ref[t] = _gelu_tanh_x2(_dot(h_ref[t], win_ref[:, OFF_PU:OFF_PU + A_WIDTH]))

    def proj_v(t):
        pv = _gelu_tanh_x2(_dot(h_ref[t], win_ref[:, OFF_PV:OFF_PV + A_WIDTH]))
        rv = lax.rsqrt(jnp.mean(pv * pv, axis=-1, keepdims=True) + 4.0 * EPS)
        vn_ref[t] = (pv * rv * gsgu_ref[...]).astype(BF16)

    def proj_q(t):
        q = _dot(h_ref[t], win_ref[:, OFF_Q:OFF_Q + Q_DIM])
        qlo_ref[t] = jnp.where(q_is_lo, q, 0.0).astype(BF16)
        qhi_ref[t] = jnp.where(q_is_lo, 0.0, q).astype(BF16)

    def proj_kv(t):
        kvp = _dot(h_ref[t], win_ref[:, OFF_K:OFF_K + 2 * KV_DIM])
        k = kvp[:, :KV_DIM] * (HEAD_DIM ** -0.5 * LOG2E)
        v = kvp[:, KV_DIM:]
        k_sw = pltpu.roll(k, HEAD_DIM, axis=1)
        v_sw = pltpu.roll(v, HEAD_DIM, axis=1)
        rows = slice(BLOCK + t * tm, BLOCK + (t + 1) * tm)
        kd_ref[0, rows, :] = jnp.where(t_is_lo, k, k_sw).astype(BF16)
        kd_ref[1, rows, :] = jnp.where(t_is_lo, k_sw, k).astype(BF16)
        vd_ref[0, rows, :] = jnp.where(t_is_lo, v, v_sw).astype(BF16)
        vd_ref[1, rows, :] = jnp.where(t_is_lo, v_sw, v).astype(BF16)

    def sgu(t, g):
        cols = slice(g * CHUNK, (g + 1) * CHUNK)
        v_chunks = [vn_ref[t, b * CHUNK:(b + 1) * CHUNK, cols] for b in range(tm // CHUNK)]
        s_all = _dot(wmask[g], jnp.concatenate(v_chunks, axis=1))
        for b in range(tm // CHUNK):
            rows = slice(b * CHUNK, (b + 1) * CHUNK)
            s = s_all[:, b * CHUNK:(b + 1) * CHUNK] + bfull_ref[:, cols]
            ya_ref[t, rows, cols] = (u_ref[t, rows, cols] * s).astype(BF16)

    probs = {}

    def attn_scores(t, b, kv):
        rows = slice(b * BLOCK, (b + 1) * BLOCK)
        bb = t * blocks_per_tile + b
        q_parts = []
        for g in range(heads_per_kv):
            hh = kv * heads_per_kv + g
            src = qlo_ref if hh % 2 == 0 else qhi_ref
            c = hh // 2
            q_parts.append(src[t, rows, c * LANES:(c + 1) * LANES])
        q_stack = jnp.concatenate(q_parts, axis=0)
        keys = kd_ref[kv, bb * BLOCK:(bb + 2) * BLOCK, :]
        sc = lax.dot_general(q_stack, keys, (((1,), (1,)), ((), ())),
                             preferred_element_type=F32)
        p_parts = []
        inv_l = []
        for g in range(heads_per_kv):
            hh = kv * heads_per_kv + g
            sh = sc[g * BLOCK:(g + 1) * BLOCK]
            sg = jnp.where(from_prev, sh[:, :BLOCK], sh[:, BLOCK:]) + bias_ref[hh]
            if bb == 0:
                sg = jnp.where(no_prev, NEG_INF, sg)
            sink = sinks_ref[hh] * LOG2E
            m = jnp.maximum(jnp.max(sg, axis=-1, keepdims=True), sink)
            pf = jnp.exp2(sg - m)
            l = jnp.sum(pf, axis=-1, keepdims=True) + jnp.exp2(sink - m)
            p = pf.astype(BF16)
            zero = jnp.zeros_like(p)
            p_parts.append(jnp.concatenate([jnp.where(from_prev, p, zero), jnp.where(from_prev, zero, p)], axis=1))
            inv_l.append(1.0 / l)
        probs[t, b, kv] = (jnp.concatenate(p_parts, axis=0), inv_l)

    def attn_values(t, b, kv):
        rows = slice(b * BLOCK, (b + 1) * BLOCK)
        bb = t * blocks_per_tile + b
        p_stack, inv_l = probs.pop((t, b, kv))
        vals = vd_ref[kv, bb * BLOCK:(bb + 2) * BLOCK, :]
        o = _dot(p_stack, vals)
        for cc in range(heads_per_kv // 2):
            o_even = o[(2 * cc) * BLOCK:(2 * cc + 1) * BLOCK] * inv_l[2 * cc]
            o_odd = o[(2 * cc + 1) * BLOCK:(2 * cc + 2) * BLOCK] * inv_l[2 * cc + 1]
            col = kv * (heads_per_kv // 2) + cc
            yb_ref[t, rows, col * LANES:(col + 1) * LANES] = jnp.where(b_is_lo, o_even, o_odd).astype(BF16)

    def merge(t, c):
        cs = slice(c * MXU_COLS, (c + 1) * MXU_COLS)
        ga = _dot(h_ref[t], win_ref[:, OFF_GA + c * MXU_COLS:OFF_GA + (c + 1) * MXU_COLS])
        gb = _dot(h_ref[t], win_ref[:, OFF_GB + c * MXU_COLS:OFF_GB + (c + 1) * MXU_COLS])
        pa = _dot(ya_ref[t], wpa_ref[:, cs])
        pb = _dot(yb_ref[t], wpb_ref[:, cs])
        m_ref[t, :, cs] = (_sigmoid(ga) * pa + _sigmoid(gb) * pb).astype(BF16)

    def out_proj(t, c):
        cs = slice(c * MXU_COLS, (c + 1) * MXU_COLS)
        val = x_ref[t * tm:(t + 1) * tm, cs] + _dot(m_ref[t], wout_ref[:, cs])
        n_grp = tm // SUBLANES
        n_lt = D_MODEL // LANES
        for ll in range(MXU_COLS // LANES):
            lt = c * (MXU_COLS // LANES) + ll
            for g in range(n_grp):
                k, j0 = divmod(SUBLANES * g, n_grp)
                o_ref[t * n_lt + lt, pl.ds(SUBLANES * j0 + k, SUBLANES, stride=SUBLANES), :] = (
                    val[SUBLANES * g:SUBLANES * (g + 1), ll * LANES:(ll + 1) * LANES])

    P = functools.partial

    def stage1(t):
        return [P(proj_u, t), P(proj_v, t), P(proj_q, t), P(proj_kv, t)]

    def stage2(t):
        sgu_items = [P(sgu, t, g) for g in range(A_GROUPS)]
        def attn(b, kv):
            attn_scores(t, b, kv)
            attn_values(t, b, kv)

        attn_items = [P(attn, b, kv) for b in range(blocks_per_tile) for kv in range(N_KV_HEADS)]
        items = []
        for i, item in enumerate(attn_items):
            items += sgu_items[i * A_GROUPS // len(attn_items):(i + 1) * A_GROUPS // len(attn_items)] + [item]
        return items

    def stage3(t):
        return [P(merge, t, c) for c in range(n_out)]

    def stage4(t):
        return [P(out_proj, t, c) for c in range(n_out)]

    def interleave(heavy, light):
        n, m = len(heavy), len(light)
        done = 0
        for i, item in enumerate(heavy):
            item()
            upto = (i + 1) * m // n
            for light_item in light[done:upto]:
                light_item()
            done = upto

    def run(items):
        for item in items:
            item()

    norm(0)
    run(stage1(0))
    for t in range(n_tiles):
        heavy = []
        if t >= 2:
            heavy += stage4(t - 2)
        if t >= 1:
            heavy += stage3(t - 1)
        if t + 1 < n_tiles:
            norm(t + 1)
            heavy += stage1(t + 1)
        interleave(heavy, stage2(t)) if heavy else run(stage2(t))
    if n_tiles >= 2:
        run(stage4(n_tiles - 2))
    run(stage3(n_tiles - 1))
    run(stage4(n_tiles - 1))

    rows_all = n_tiles * tm
    kd_ref[:, 0:BLOCK, :] = kd_ref[:, rows_all:rows_all + BLOCK, :]
    vd_ref[:, 0:BLOCK, :] = vd_ref[:, rows_all:rows_all + BLOCK, :]


def _resident(shape):
    nd = len(shape)
    return pl.BlockSpec(shape, lambda *_: (0,) * nd, pipeline_mode=pl.Buffered(1))


def _token_mixer(x2d, batch, seq, sinks, g_mix, w_in, g_sgu, w_s, b_full, bias_tbl, w_pa, w_pb, w_out,
                 w_up_f32, w_down_f32):
    tm = TILE_ROWS
    rows = MIX_TILES * tm
    ns = seq // rows
    n_steps = batch * ns
    n_lt = D_MODEL // LANES
    row_spec = pl.BlockSpec((rows, D_MODEL), lambda b, s: (b * ns + s, 0))
    up_slab = D_MODEL // n_steps
    down_slab = BF16_ROWS * (D_FF // F_CHUNK)
    n_down_slabs = D_FF // down_slab
    assert D_MODEL % n_steps == 0 and up_slab % BF16_ROWS == 0 and D_FF % down_slab == 0 and n_down_slabs <= n_steps
    up_spec = pl.BlockSpec((up_slab, 2 * D_FF), lambda b, s: (b * ns + s, 0))
    down_spec = pl.BlockSpec((down_slab, D_MODEL), lambda b, s: (jnp.minimum(b * ns + s, n_down_slabs - 1), 0))
    return pl.pallas_call(
        _mixer_kernel,
        grid=(batch, ns),
        in_specs=[
            pl.BlockSpec(memory_space=pltpu.SMEM),
            row_spec,
            _resident((1, D_MODEL)),
            _resident((D_MODEL, IN_DIM)),
            _resident((1, A_WIDTH)),
            _resident((A_GROUPS, CHUNK, CHUNK)),
            _resident((CHUNK, A_WIDTH)),
            _resident((N_HEADS, BLOCK, BLOCK)),
            _resident((A_WIDTH, D_MODEL)),
            _resident((Q_DIM, D_MODEL)),
            _resident((D_MODEL, D_MODEL)),
            up_spec,
            down_spec,
        ],
        out_specs=[pl.BlockSpec((MIX_TILES * n_lt, tm, LANES), lambda b, s: (b * ns + s, 0, 0)), up_spec, down_spec],
        out_shape=[jax.ShapeDtypeStruct((x2d.shape[0] // tm * n_lt, tm, LANES), F32),
                   jax.ShapeDtypeStruct(w_up_f32.shape, BF16),
                   jax.ShapeDtypeStruct(w_down_f32.shape, BF16)],
        scratch_shapes=[
            pltpu.VMEM((MIX_TILES, tm, D_MODEL), BF16),
            pltpu.VMEM((MIX_TILES, tm, A_WIDTH), F32),
            pltpu.VMEM((MIX_TILES, tm, A_WIDTH), BF16),
            pltpu.VMEM((MIX_TILES, tm, Q_DIM), BF16),
            pltpu.VMEM((MIX_TILES, tm, Q_DIM), BF16),
            pltpu.VMEM((N_KV_HEADS, BLOCK + rows, LANES), BF16),
            pltpu.VMEM((N_KV_HEADS, BLOCK + rows, LANES), BF16),
            pltpu.VMEM((MIX_TILES, tm, A_WIDTH), BF16),
            pltpu.VMEM((MIX_TILES, tm, Q_DIM), BF16),
            pltpu.VMEM((MIX_TILES, tm, D_MODEL), BF16),
        ],
        compiler_params=pltpu.CompilerParams(
            dimension_semantics=("arbitrary", "arbitrary"),
            vmem_limit_bytes=VMEM_LIMIT),
        name="token_mixer",
    )(sinks, x2d, g_mix, w_in, g_sgu, w_s, b_full, bias_tbl, w_pa, w_pb, w_out, w_up_f32, w_down_f32)


def _ffn_kernel(x_ref, gffn_ref, wup_ref, wconv_ref, bconv_ref, wdown_ref, gfin_ref, o_ref,
                skew_ref, h_ref, carry_ref, act_ref, *, steps_per_seq):
    tm = TILE_ROWS
    n_tiles = o_ref.shape[0] // tm
    n_grp = tm // SUBLANES
    pitch = n_grp + SUBLANES
    n_lane_tiles = D_MODEL // LANES
    n_chunks = D_FF // F_CHUNK
    n_down = D_MODEL // MXU_COLS

    @pl.when(pl.program_id(0) % steps_per_seq == 0)
    def _():
        carry_ref[...] = jnp.zeros(carry_ref.shape, F32)

    def slot(t):
        return t % FFN_SLOTS

    def x_cols(t, lt0, lt1):
        return jnp.concatenate([x_ref[t * n_lane_tiles + lt] for lt in range(lt0, lt1)], axis=1)

    def prologue(t):
        x = x_cols(t, 0, n_lane_tiles)
        r = lax.rsqrt(jnp.mean(x * x, axis=-1, keepdims=True) + EPS)
        h_ref[slot(t)] = (x * r * gffn_ref[...]).astype(BF16)

    first_sublane = lax.broadcasted_iota(jnp.int32, (SUBLANES, F_CHUNK), 0) == 0

    def conv(t, cols):
        up = _dot(h_ref[slot(t)], wup_ref[:, cols])
        last = up[tm - SUBLANES:tm]
        last2 = up[tm - 2 * SUBLANES:tm - SUBLANES]

        def wrap(cur, prev):
            return jnp.where(first_sublane, pltpu.roll(prev, 1, axis=0), pltpu.roll(cur, 1, axis=0))

        m1 = wrap(last, carry_ref[1, :, cols])
        m2 = wrap(last2, carry_ref[0, :, cols])
        carry_ref[0, :, cols] = last2
        carry_ref[1, :, cols] = last
        up1 = jnp.concatenate([m1, up[:tm - SUBLANES]], axis=0)
        up2 = jnp.concatenate([m2, m1, up[:tm - 2 * SUBLANES]], axis=0)
        return (bconv_ref[:, cols] + wconv_ref[2:3, cols] * up
                + wconv_ref[1:2, cols] * up1 + wconv_ref[0:1, cols] * up2)

    def up_chunk(t, c):
        gcols = slice(c * F_CHUNK, (c + 1) * F_CHUNK)
        gate = conv(t, gcols)
        val = conv(t, slice(D_FF + c * F_CHUNK, D_FF + (c + 1) * F_CHUNK))
        hg = 0.5 * gate
        act_ref[slot(t), :, gcols] = ((hg + hg * jnp.tanh(hg)) * val).astype(BF16)

    def unpermute_store(t, lt, vals):
        for j in range(n_grp):
            skew_ref[slot(t), lt, pl.ds(j, SUBLANES, stride=pitch), :] = vals[SUBLANES * j:SUBLANES * (j + 1), :]

    def down_block(t, j):
        cs = slice(j * MXU_COLS, (j + 1) * MXU_COLS)
        lts = MXU_COLS // LANES
        y = x_cols(t, j * lts, (j + 1) * lts) + _dot(act_ref[slot(t)], wdown_ref[:, cs])
        for ll in range(lts):
            unpermute_store(t, j * lts + ll, y[:, ll * LANES:(ll + 1) * LANES])
        return jnp.sum(y * y, axis=-1, keepdims=True)

    def epilogue(t, ssqs):
        rf = lax.rsqrt(sum(ssqs) * (1.0 / D_MODEL) + EPS)
        unpermute_store(t, n_lane_tiles, jnp.broadcast_to(rf, (tm, LANES)))
        for k in range(SUBLANES):
            blk = slice(pitch * k, pitch * k + n_grp)
            rf_nat = skew_ref[slot(t), n_lane_tiles, blk, :]
            for lt in range(n_lane_tiles):
                cols = slice(lt * LANES, (lt + 1) * LANES)
                o_ref[t * tm + n_grp * k:t * tm + n_grp * (k + 1), cols] = (
                    skew_ref[slot(t), lt, blk, :] * rf_nat * gfin_ref[:, cols])

    down_before = {k * n_chunks // n_down: k for k in range(n_down)}
    assert len(down_before) == n_down
    prologue(0)
    for t in range(n_tiles):
        if t + 1 < n_tiles:
            prologue(t + 1)
        ys = {}
        for c in range(n_chunks):
            if t >= 1 and c in down_before:
                ys[down_before[c]] = down_block(t - 1, down_before[c])
            up_chunk(t, c)
        if t >= 1:
            epilogue(t - 1, [ys[j] for j in range(n_down)])
    last = n_tiles - 1
    epilogue(last, [down_block(last, j) for j in range(n_down)])


def _conv_ffn(x2d, batch, seq, g_ffn, w_up, w_conv, b_conv, w_down, g_final):
    rows = FFN_TILES * TILE_ROWS
    steps_per_seq = seq // rows
    row_spec = pl.BlockSpec((rows, D_MODEL), lambda i: (i, 0))
    skew_rows = TILE_ROWS + SUBLANES * SUBLANES
    return pl.pallas_call(
        functools.partial(_ffn_kernel, steps_per_seq=steps_per_seq),
        grid=(batch * steps_per_seq,),
        in_specs=[
            pl.BlockSpec((FFN_TILES * (D_MODEL // LANES), TILE_ROWS, LANES), lambda i: (i, 0, 0)),
            _resident((1, D_MODEL)),
            _resident((D_MODEL, 2 * D_FF)),
            _resident((CONV_WIDTH, 2 * D_FF)),
            _resident((1, 2 * D_FF)),
            _resident((D_FF, D_MODEL)),
            _resident((1, D_MODEL)),
        ],
        out_specs=row_spec,
        out_shape=jax.ShapeDtypeStruct((x2d.shape[0] // (D_MODEL // LANES) * TILE_ROWS, D_MODEL), F32),
        scratch_shapes=[
            pltpu.VMEM((FFN_SLOTS, D_MODEL // LANES + 1, skew_rows, LANES), F32),
            pltpu.VMEM((FFN_SLOTS, TILE_ROWS, D_MODEL), BF16),
            pltpu.VMEM((CONV_WIDTH - 1, SUBLANES, 2 * D_FF), F32),
            pltpu.VMEM((FFN_SLOTS, TILE_ROWS, D_FF), BF16),
        ],
        compiler_params=pltpu.CompilerParams(
            dimension_semantics=("arbitrary",),
            vmem_limit_bytes=VMEM_LIMIT),
        name="conv_ffn",
    )(x2d, g_ffn, w_up, w_conv, b_conv, w_down, g_final)


def kernel(x, g_mix, w_in, g_sgu, w_s, b_s, sinks, rel_bias, w_pa, w_pb, w_out,
           g_ffn, w_up, w_conv, b_conv, w_down, g_final):
    batch, seq, d = x.shape
    assert w_in.shape[0] == 1 and d == D_MODEL and seq % (max(FFN_TILES, MIX_TILES) * TILE_ROWS) == 0 and w_in.shape[2] == IN_DIM
    bias_tbl, (w_in_bf16, w_pa_bf16, w_pb_bf16, w_out_bf16) = _prep(
        rel_bias, (w_in[0], w_pa[0], w_pb[0], w_out[0]))
    x2d = x.reshape(batch * seq, d)
    b_full = 0.5 * jnp.repeat(jnp.transpose(b_s[0]), A_WIDTH // A_GROUPS, axis=1)
    x2d, w_up_bf16, w_down_bf16 = _token_mixer(
        x2d, batch, seq, sinks[0], g_mix[0][None, :], w_in_bf16, g_sgu[0][None, :],
        w_s[0], b_full, bias_tbl, w_pa_bf16, w_pb_bf16, w_out_bf16,
        w_up[0], w_down[0])
    x2d = _conv_ffn(
        x2d, batch, seq, g_ffn[0][None, :], w_up_bf16, w_conv[0], b_conv[0][None, :],
        w_down_bf16, g_final[None, :])
    return x2d.reshape(batch, seq, d)
```

```python
import functools
import math

import jax
import jax.numpy as jnp
import numpy as np
from jax import lax
from jax.experimental import pallas as pl
from jax.experimental.pallas import tpu as pltpu

D_MODEL = 1024
CHUNK = 128
A_GROUPS = 4
A_WIDTH = 512
N_HEADS = 8
N_KV_HEADS = 2
HEAD_DIM = 64
Q_DIM = N_HEADS * HEAD_DIM
KV_DIM = N_KV_HEADS * HEAD_DIM
WINDOW = 128
BLOCK = 128
N_BUCKETS = 32
MAX_DISTANCE = 128
D_FF = 2816
CONV_WIDTH = 3
EPS = 1e-6
NEG_INF = -1e30
LOG2E = math.log2(math.e)

OFF_PU = 0
OFF_PV = OFF_PU + A_WIDTH
OFF_Q = OFF_PV + A_WIDTH
OFF_K = OFF_Q + Q_DIM
OFF_V = OFF_K + KV_DIM
OFF_GA = OFF_V + KV_DIM
OFF_GB = OFF_GA + D_MODEL
IN_DIM = OFF_GB + D_MODEL

LANES = 128
SUBLANES = 8
MXU_COLS = 256
TILE_ROWS = 512
F_CHUNK = 256
FFN_TILES = 2
FFN_SLOTS = 2
MIX_TILES = 2
VMEM_BYTES = 64 * 1024 * 1024
VMEM_LIMIT = VMEM_BYTES - 8 * 1024 * 1024
BF16_ROWS = 2 * SUBLANES

F32 = jnp.float32
BF16 = jnp.bfloat16


def _gelu_tanh_x2(x):
    c = math.sqrt(2.0 / math.pi)
    return x + x * jnp.tanh(x * (c + (c * 0.044715) * (x * x)))


def _sigmoid(x):
    return 1.0 / (1.0 + jnp.exp(-x))


def _dot(a, b):
    return jnp.dot(a, b, preferred_element_type=F32)


def _band_buckets():
    assert WINDOW == BLOCK
    i = np.arange(BLOCK)[:, None]
    j = np.arange(BLOCK)[None, :]
    d = np.where(j > i, i + BLOCK - j, i - j)
    max_exact = N_BUCKETS // 2
    large = max_exact + (np.log(np.maximum(d, 1) / max_exact) / np.log(MAX_DISTANCE / max_exact)
                         * (N_BUCKETS - max_exact)).astype(np.int32)
    large = np.minimum(large, N_BUCKETS - 1)
    return np.where(d < max_exact, d, large).astype(np.int32)


def _prep_kernel(rb_ref, bucket_ref, *refs):
    n_w = (len(refs) - 1) // 2
    w_refs, o_ref, wb_refs = refs[:n_w], refs[n_w], refs[n_w + 1:]
    h = pl.program_id(0)
    bucket = bucket_ref[...]
    acc = jnp.zeros(bucket.shape, F32)
    for b in range(N_BUCKETS):
        acc = jnp.where(bucket == b, rb_ref[b, h], acc)
    o_ref[...] = acc * LOG2E
    for w_ref, wb_ref in zip(w_refs, wb_refs):
        wb_ref[...] = w_ref[...].astype(BF16)


def _prep(rel_bias, weights):
    buckets = jnp.asarray(_band_buckets())
    slab_specs = []
    for w in weights:
        assert w.shape[0] % (N_HEADS * BF16_ROWS) == 0
        slab_specs.append(pl.BlockSpec((w.shape[0] // N_HEADS, w.shape[1]), lambda h: (h, 0)))
    outs = pl.pallas_call(
        _prep_kernel,
        grid=(N_HEADS,),
        in_specs=[
            pl.BlockSpec(memory_space=pltpu.SMEM),
            pl.BlockSpec((BLOCK, BLOCK), lambda h: (0, 0)),
        ] + slab_specs,
        out_specs=[pl.BlockSpec((None, BLOCK, BLOCK), lambda h: (h, 0, 0))] + slab_specs,
        out_shape=[jax.ShapeDtypeStruct((N_HEADS, BLOCK, BLOCK), F32)]
        + [jax.ShapeDtypeStruct(w.shape, BF16) for w in weights],
        name="rel_bias_table",
    )(rel_bias, buckets, *weights)
    return outs[0], outs[1:]


def _mixer_kernel(sinks_ref, x_ref, gmix_ref, win_ref, gsgu_ref, ws_ref, bfull_ref, bias_ref,
                  wpa_ref, wpb_ref, wout_ref, wup_f32_ref, wdown_f32_ref, o_ref, wup_bf16_ref, wdown_bf16_ref,
                  h_ref, u_ref, vn_ref, qlo_ref, qhi_ref, kd_ref, vd_ref, ya_ref, yb_ref, m_ref):
    wup_bf16_ref[...] = wup_f32_ref[...].astype(BF16)
    wdown_bf16_ref[...] = wdown_f32_ref[...].astype(BF16)

    tm = TILE_ROWS
    n_tiles = x_ref.shape[0] // tm
    blocks_per_tile = tm // BLOCK
    s_idx = pl.program_id(1)
    heads_per_kv = N_HEADS // N_KV_HEADS
    n_out = D_MODEL // MXU_COLS

    @pl.when(s_idx == 0)
    def _():
        kd_ref[:, 0:BLOCK, :] = jnp.zeros((N_KV_HEADS, BLOCK, LANES), BF16)
        vd_ref[:, 0:BLOCK, :] = jnp.zeros((N_KV_HEADS, BLOCK, LANES), BF16)

    row_c = lax.broadcasted_iota(jnp.int32, (CHUNK, CHUNK), 0)
    col_c = lax.broadcasted_iota(jnp.int32, (CHUNK, CHUNK), 1)
    tril = col_c <= row_c
    wmask = [jnp.where(tril, 0.5 * ws_ref[g], 0.0).astype(BF16) for g in range(A_GROUPS)]
    from_prev = (lax.broadcasted_iota(jnp.int32, (BLOCK, BLOCK), 1)
                 > lax.broadcasted_iota(jnp.int32, (BLOCK, BLOCK), 0))
    no_prev = from_prev & (s_idx == 0)
    b_is_lo = lax.broadcasted_iota(jnp.int32, (BLOCK, LANES), 1) < HEAD_DIM
    q_is_lo = (lax.broadcasted_iota(jnp.int32, (tm, Q_DIM), 1) % LANES) < HEAD_DIM
    t_is_lo = lax.broadcasted_iota(jnp.int32, (tm, LANES), 1) < HEAD_DIM

    def norm(t):
        x = x_ref[t * tm:(t + 1) * tm, :]
        r = lax.rsqrt(jnp.mean(x * x, axis=-1, keepdims=True) + EPS)
        h_ref[t] = (x * r * gmix_ref[...]).astype(BF16)

    def proj_u(t):
        u_ref[t] = _gelu_tanh_x2(_dot(h_ref[t], win_ref[:, OFF_PU:OFF_PU + A_WIDTH]))

    def proj_v(t):
        pv = _gelu_tanh_x2(_dot(h_ref[t], win_ref[:, OFF_PV:OFF_PV + A_WIDTH]))
        rv = lax.rsqrt(jnp.mean(pv * pv, axis=-1, keepdims=True) + 4.0 * EPS)
        vn_ref[t] = (pv * rv * gsgu_ref[...]).astype(BF16)

    def proj_q(t):
        q = _dot(h_ref[t], win_ref[:, OFF_Q:OFF_Q + Q_DIM]).astype(BF16)
        zero = jnp.zeros_like(q)
        qlo_ref[t] = jnp.where(q_is_lo, q, zero)
        qhi_ref[t] = jnp.where(q_is_lo, zero, q)

    def proj_kv(t):
        kvp = _dot(h_ref[t], win_ref[:, OFF_K:OFF_K + 2 * KV_DIM])
        k = kvp[:, :KV_DIM] * (HEAD_DIM ** -0.5 * LOG2E)
        v = kvp[:, KV_DIM:]
        k_sw = pltpu.roll(k, HEAD_DIM, axis=1)
        v_sw = pltpu.roll(v, HEAD_DIM, axis=1)
        rows = slice(BLOCK + t * tm, BLOCK + (t + 1) * tm)
        kd_ref[0, rows, :] = jnp.where(t_is_lo, k, k_sw).astype(BF16)
        kd_ref[1, rows, :] = jnp.where(t_is_lo, k_sw, k).astype(BF16)
        vd_ref[0, rows, :] = jnp.where(t_is_lo, v, v_sw).astype(BF16)
        vd_ref[1, rows, :] = jnp.where(t_is_lo, v_sw, v).astype(BF16)

    def sgu(t, g):
        cols = slice(g * CHUNK, (g + 1) * CHUNK)
        v_chunks = [vn_ref[t, b * CHUNK:(b + 1) * CHUNK, cols] for b in range(tm // CHUNK)]
        s_all = _dot(wmask[g], jnp.concatenate(v_chunks, axis=1))
        for b in range(tm // CHUNK):
            rows = slice(b * CHUNK, (b + 1) * CHUNK)
            s = s_all[:, b * CHUNK:(b + 1) * CHUNK] + bfull_ref[:, cols]
            ya_ref[t, rows, cols] = (u_ref[t, rows, cols] * s).astype(BF16)

    probs = {}

    def attn_scores(t, b, kv):
        rows = slice(b * BLOCK, (b + 1) * BLOCK)
        bb = t * blocks_per_tile + b
        q_parts = []
        for g in range(heads_per_kv):
            hh = kv * heads_per_kv + g
            src = qlo_ref if hh % 2 == 0 else qhi_ref
            c = hh // 2
            q_parts.append(src[t, rows, c * LANES:(c + 1) * LANES])
        q_stack = jnp.concatenate(q_parts, axis=0)
        keys = kd_ref[kv, bb * BLOCK:(bb + 2) * BLOCK, :]
        sc = lax.dot_general(q_stack, keys, (((1,), (1,)), ((), ())),
                             preferred_element_type=F32)
        p_parts = []
        inv_l = []
        for g in range(heads_per_kv):
            hh = kv * heads_per_kv + g
            sh = sc[g * BLOCK:(g + 1) * BLOCK]
            sg = jnp.where(from_prev, sh[:, :BLOCK], sh[:, BLOCK:]) + bias_ref[hh]
            if bb == 0:
                sg = jnp.where(no_prev, NEG_INF, sg)
            sink = sinks_ref[hh] * LOG2E
            m = jnp.maximum(jnp.max(sg, axis=-1, keepdims=True), sink)
            pf = jnp.exp2(sg - m)
            l = jnp.sum(pf, axis=-1, keepdims=True) + jnp.exp2(sink - m)
            p = pf.astype(BF16)
            zero = jnp.zeros_like(p)
            p_parts.append(jnp.concatenate([jnp.where(from_prev, p, zero), jnp.where(from_prev, zero, p)], axis=1))
            inv_l.append(1.0 / l)
        probs[t, b, kv] = (jnp.concatenate(p_parts, axis=0), inv_l)

    def attn_values(t, b, kv):
        rows = slice(b * BLOCK, (b + 1) * BLOCK)
        bb = t * blocks_per_tile + b
        p_stack, inv_l = probs.pop((t, b, kv))
        vals = vd_ref[kv, bb * BLOCK:(bb + 2) * BLOCK, :]
        o = _dot(p_stack, vals)
        for cc in range(heads_per_kv // 2):
            o_even = o[(2 * cc) * BLOCK:(2 * cc + 1) * BLOCK] * inv_l[2 * cc]
            o_odd = o[(2 * cc + 1) * BLOCK:(2 * cc + 2) * BLOCK] * inv_l[2 * cc + 1]
            col = kv * (heads_per_kv // 2) + cc
            yb_ref[t, rows, col * LANES:(col + 1) * LANES] = jnp.where(b_is_lo, o_even, o_odd).astype(BF16)

    def merge(t, c):
        cs = slice(c * MXU_COLS, (c + 1) * MXU_COLS)
        ga = _dot(h_ref[t], win_ref[:, OFF_GA + c * MXU_COLS:OFF_GA + (c + 1) * MXU_COLS])
        gb = _dot(h_ref[t], win_ref[:, OFF_GB + c * MXU_COLS:OFF_GB + (c + 1) * MXU_COLS])
        pa = _dot(ya_ref[t], wpa_ref[:, cs])
        pb = _dot(yb_ref[t], wpb_ref[:, cs])
        m_ref[t, :, cs] = (_sigmoid(ga) * pa + _sigmoid(gb) * pb).astype(BF16)

    def out_proj(t, c):
        cs = slice(c * MXU_COLS, (c + 1) * MXU_COLS)
        val = x_ref[t * tm:(t + 1) * tm, cs] + _dot(m_ref[t], wout_ref[:, cs])
        n_grp = tm // SUBLANES
        n_lt = D_MODEL // LANES
        for ll in range(MXU_COLS // LANES):
            lt = c * (MXU_COLS // LANES) + ll
            for g in range(n_grp):
                k, j0 = divmod(SUBLANES * g, n_grp)
                o_ref[t * n_lt + lt, pl.ds(SUBLANES * j0 + k, SUBLANES, stride=SUBLANES), :] = (
                    val[SUBLANES * g:SUBLANES * (g + 1), ll * LANES:(ll + 1) * LANES])

    P = functools.partial

    def stage1(t):
        return [P(proj_u, t), P(proj_v, t), P(proj_q, t), P(proj_kv, t)]

    def stage2(t):
        sgu_items = [P(sgu, t, g) for g in range(A_GROUPS)]
        def attn(b, kv):
            attn_scores(t, b, kv)
            attn_values(t, b, kv)

        attn_items = [P(attn, b, kv) for b in range(blocks_per_tile) for kv in range(N_KV_HEADS)]
        items = []
        for i, item in enumerate(attn_items):
            items += sgu_items[i * A_GROUPS // len(attn_items):(i + 1) * A_GROUPS // len(attn_items)] + [item]
        return items

    def stage3(t):
        return [P(merge, t, c) for c in range(n_out)]

    def stage4(t):
        return [P(out_proj, t, c) for c in range(n_out)]

    def interleave(heavy, light):
        n, m = len(heavy), len(light)
        done = 0
        for i, item in enumerate(heavy):
            item()
            upto = (i + 1) * m // n
            for light_item in light[done:upto]:
                light_item()
            done = upto

    def run(items):
        for item in items:
            item()

    norm(0)
    run(stage1(0))
    for t in range(n_tiles):
        heavy = []
        if t >= 2:
            heavy += stage4(t - 2)
        if t >= 1:
            heavy += stage3(t - 1)
        if t + 1 < n_tiles:
            norm(t + 1)
            heavy += stage1(t + 1)
        interleave(heavy, stage2(t)) if heavy else run(stage2(t))
    if n_tiles >= 2:
        run(stage4(n_tiles - 2))
    run(stage3(n_tiles - 1))
    run(stage4(n_tiles - 1))

    rows_all = n_tiles * tm
    kd_ref[:, 0:BLOCK, :] = kd_ref[:, rows_all:rows_all + BLOCK, :]
    vd_ref[:, 0:BLOCK, :] = vd_ref[:, rows_all:rows_all + BLOCK, :]


def _resident(shape):
    nd = len(shape)
    return pl.BlockSpec(shape, lambda *_: (0,) * nd, pipeline_mode=pl.Buffered(1))


def _token_mixer(x2d, batch, seq, sinks, g_mix, w_in, g_sgu, w_s, b_full, bias_tbl, w_pa, w_pb, w_out,
                 w_up_f32, w_down_f32):
    tm = TILE_ROWS
    rows = MIX_TILES * tm
    ns = seq // rows
    n_steps = batch * ns
    n_lt = D_MODEL // LANES
    row_spec = pl.BlockSpec((rows, D_MODEL), lambda b, s: (b * ns + s, 0))
    up_slab = D_MODEL // n_steps
    down_slab = BF16_ROWS * (D_FF // F_CHUNK)
    n_down_slabs = D_FF // down_slab
    assert D_MODEL % n_steps == 0 and up_slab % BF16_ROWS == 0 and D_FF % down_slab == 0 and n_down_slabs <= n_steps
    up_spec = pl.BlockSpec((up_slab, 2 * D_FF), lambda b, s: (b * ns + s, 0))
    down_spec = pl.BlockSpec((down_slab, D_MODEL), lambda b, s: (jnp.minimum(b * ns + s, n_down_slabs - 1), 0))
    return pl.pallas_call(
        _mixer_kernel,
        grid=(batch, ns),
        in_specs=[
            pl.BlockSpec(memory_space=pltpu.SMEM),
            row_spec,
            _resident((1, D_MODEL)),
            _resident((D_MODEL, IN_DIM)),
            _resident((1, A_WIDTH)),
            _resident((A_GROUPS, CHUNK, CHUNK)),
            _resident((CHUNK, A_WIDTH)),
            _resident((N_HEADS, BLOCK, BLOCK)),
            _resident((A_WIDTH, D_MODEL)),
            _resident((Q_DIM, D_MODEL)),
            _resident((D_MODEL, D_MODEL)),
            up_spec,
            down_spec,
        ],
        out_specs=[pl.BlockSpec((MIX_TILES * n_lt, tm, LANES), lambda b, s: (b * ns + s, 0, 0)), up_spec, down_spec],
        out_shape=[jax.ShapeDtypeStruct((x2d.shape[0] // tm * n_lt, tm, LANES), F32),
                   jax.ShapeDtypeStruct(w_up_f32.shape, BF16),
                   jax.ShapeDtypeStruct(w_down_f32.shape, BF16)],
        scratch_shapes=[
            pltpu.VMEM((MIX_TILES, tm, D_MODEL), BF16),
            pltpu.VMEM((MIX_TILES, tm, A_WIDTH), F32),
            pltpu.VMEM((MIX_TILES, tm, A_WIDTH), BF16),
            pltpu.VMEM((MIX_TILES, tm, Q_DIM), BF16),
            pltpu.VMEM((MIX_TILES, tm, Q_DIM), BF16),
            pltpu.VMEM((N_KV_HEADS, BLOCK + rows, LANES), BF16),
            pltpu.VMEM((N_KV_HEADS, BLOCK + rows, LANES), BF16),
            pltpu.VMEM((MIX_TILES, tm, A_WIDTH), BF16),
            pltpu.VMEM((MIX_TILES, tm, Q_DIM), BF16),
            pltpu.VMEM((MIX_TILES, tm, D_MODEL), BF16),
        ],
        compiler_params=pltpu.CompilerParams(
            dimension_semantics=("arbitrary", "arbitrary"),
            vmem_limit_bytes=VMEM_LIMIT),
        name="token_mixer",
    )(sinks, x2d, g_mix, w_in, g_sgu, w_s, b_full, bias_tbl, w_pa, w_pb, w_out, w_up_f32, w_down_f32)


def _ffn_kernel(x_ref, gffn_ref, wup_ref, wconv_ref, bconv_ref, wdown_ref, gfin_ref, o_ref,
                skew_ref, h_ref, carry_ref, act_ref, *, steps_per_seq):
    tm = TILE_ROWS
    n_tiles = o_ref.shape[0] // tm
    n_grp = tm // SUBLANES
    pitch = n_grp + SUBLANES
    n_lane_tiles = D_MODEL // LANES
    n_chunks = D_FF // F_CHUNK
    n_down = D_MODEL // MXU_COLS

    @pl.when(pl.program_id(0) % steps_per_seq == 0)
    def _():
        carry_ref[...] = jnp.zeros(carry_ref.shape, F32)

    def slot(t):
        return t % FFN_SLOTS

    def x_cols(t, lt0, lt1):
        return jnp.concatenate([x_ref[t * n_lane_tiles + lt] for lt in range(lt0, lt1)], axis=1)

    def prologue(t):
        x = x_cols(t, 0, n_lane_tiles)
        r = lax.rsqrt(jnp.mean(x * x, axis=-1, keepdims=True) + EPS)
        h_ref[slot(t)] = (x * r * gffn_ref[...]).astype(BF16)

    first_sublane = lax.broadcasted_iota(jnp.int32, (SUBLANES, F_CHUNK), 0) == 0

    def conv(t, cols):
        up = _dot(h_ref[slot(t)], wup_ref[:, cols])
        last = up[tm - SUBLANES:tm]
        last2 = up[tm - 2 * SUBLANES:tm - SUBLANES]

        def wrap(cur, prev):
            return jnp.where(first_sublane, pltpu.roll(prev, 1, axis=0), pltpu.roll(cur, 1, axis=0))

        m1 = wrap(last, carry_ref[1, :, cols])
        m2 = wrap(last2, carry_ref[0, :, cols])
        carry_ref[0, :, cols] = last2
        carry_ref[1, :, cols] = last
        up1 = jnp.concatenate([m1, up[:tm - SUBLANES]], axis=0)
        up2 = jnp.concatenate([m2, m1, up[:tm - 2 * SUBLANES]], axis=0)
        return (bconv_ref[:, cols] + wconv_ref[2:3, cols] * up
                + wconv_ref[1:2, cols] * up1 + wconv_ref[0:1, cols] * up2)

    def up_chunk(t, c):
        gcols = slice(c * F_CHUNK, (c + 1) * F_CHUNK)
        gate = conv(t, gcols)
        val = conv(t, slice(D_FF + c * F_CHUNK, D_FF + (c + 1) * F_CHUNK))
        hg = 0.5 * gate
        act_ref[slot(t), :, gcols] = ((hg + hg * jnp.tanh(hg)) * val).astype(BF16)

    def unpermute_store(t, lt, vals):
        for j in range(n_grp):
            skew_ref[slot(t), lt, pl.ds(j, SUBLANES, stride=pitch), :] = vals[SUBLANES * j:SUBLANES * (j + 1), :]

    def down_block(t, j):
        cs = slice(j * MXU_COLS, (j + 1) * MXU_COLS)
        lts = MXU_COLS // LANES
        y = x_cols(t, j * lts, (j + 1) * lts) + _dot(act_ref[slot(t)], wdown_ref[:, cs])
        for ll in range(lts):
            unpermute_store(t, j * lts + ll, y[:, ll * LANES:(ll + 1) * LANES])
        return jnp.sum(y * y, axis=-1, keepdims=True)

    def epilogue(t, ssqs):
        rf = lax.rsqrt(sum(ssqs) * (1.0 / D_MODEL) + EPS)
        unpermute_store(t, n_lane_tiles, jnp.broadcast_to(rf, (tm, LANES)))
        for k in range(SUBLANES):
            blk = slice(pitch * k, pitch * k + n_grp)
            rf_nat = skew_ref[slot(t), n_lane_tiles, blk, :]
            for lt in range(n_lane_tiles):
                cols = slice(lt * LANES, (lt + 1) * LANES)
                o_ref[t * tm + n_grp * k:t * tm + n_grp * (k + 1), cols] = (
                    skew_ref[slot(t), lt, blk, :] * rf_nat * gfin_ref[:, cols])

    down_before = {k * n_chunks // n_down: k for k in range(n_down)}
    assert len(down_before) == n_down
    prologue(0)
    for t in range(n_tiles):
        if t + 1 < n_tiles:
            prologue(t + 1)
        ys = {}
        for c in range(n_chunks):
            if t >= 1 and c in down_before:
                ys[down_before[c]] = down_block(t - 1, down_before[c])
            up_chunk(t, c)
        if t >= 1:
            epilogue(t - 1, [ys[j] for j in range(n_down)])
    last = n_tiles - 1
    epilogue(last, [down_block(last, j) for j in range(n_down)])


def _conv_ffn(x2d, batch, seq, g_ffn, w_up, w_conv, b_conv, w_down, g_final):
    rows = FFN_TILES * TILE_ROWS
    steps_per_seq = seq // rows
    row_spec = pl.BlockSpec((rows, D_MODEL), lambda i: (i, 0))
    skew_rows = TILE_ROWS + SUBLANES * SUBLANES
    return pl.pallas_call(
        functools.partial(_ffn_kernel, steps_per_seq=steps_per_seq),
        grid=(batch * steps_per_seq,),
        in_specs=[
            pl.BlockSpec((FFN_TILES * (D_MODEL // LANES), TILE_ROWS, LANES), lambda i: (i, 0, 0)),
            _resident((1, D_MODEL)),
            _resident((D_MODEL, 2 * D_FF)),
            _resident((CONV_WIDTH, 2 * D_FF)),
            _resident((1, 2 * D_FF)),
            _resident((D_FF, D_MODEL)),
            _resident((1, D_MODEL)),
        ],
        out_specs=row_spec,
        out_shape=jax.ShapeDtypeStruct((x2d.shape[0] // (D_MODEL // LANES) * TILE_ROWS, D_MODEL), F32),
        scratch_shapes=[
            pltpu.VMEM((FFN_SLOTS, D_MODEL // LANES + 1, skew_rows, LANES), F32),
            pltpu.VMEM((FFN_SLOTS, TILE_ROWS, D_MODEL), BF16),
            pltpu.VMEM((CONV_WIDTH - 1, SUBLANES, 2 * D_FF), F32),
            pltpu.VMEM((FFN_SLOTS, TILE_ROWS, D_FF), BF16),
        ],
        compiler_params=pltpu.CompilerParams(
            dimension_semantics=("arbitrary",),
            vmem_limit_bytes=VMEM_LIMIT),
        name="conv_ffn",
    )(x2d, g_ffn, w_up, w_conv, b_conv, w_down, g_final)


def kernel(x, g_mix, w_in, g_sgu, w_s, b_s, sinks, rel_bias, w_pa, w_pb, w_out,
           g_ffn, w_up, w_conv, b_conv, w_down, g_final):
    batch, seq, d = x.shape
    assert w_in.shape[0] == 1 and d == D_MODEL and seq % (max(FFN_TILES, MIX_TILES) * TILE_ROWS) == 0 and w_in.shape[2] == IN_DIM
    bias_tbl, (w_in_bf16, w_pa_bf16, w_pb_bf16, w_out_bf16) = _prep(
        rel_bias, (w_in[0], w_pa[0], w_pb[0], w_out[0]))
    x2d = x.reshape(batch * seq, d)
    b_full = 0.5 * jnp.repeat(jnp.transpose(b_s[0]), A_WIDTH // A_GROUPS, axis=1)
    x2d, w_up_bf16, w_down_bf16 = _token_mixer(
        x2d, batch, seq, sinks[0], g_mix[0][None, :], w_in_bf16, g_sgu[0][None, :],
        w_s[0], b_full, bias_tbl, w_pa_bf16, w_pb_bf16, w_out_bf16,
        w_up[0], w_down[0])
    x2d = _conv_ffn(
        x2d, batch, seq, g_ffn[0][None, :], w_up_bf16, w_conv[0], b_conv[0][None, :],
        w_down_bf16, g_final[None, :])
    return x2d.reshape(batch, seq, d)
```
